```python
import jax, jax.numpy as jnp
from jax import lax
import numpy as np

D_MODEL = 1024
BATCH = 8
SEQ = 4096
DEPTH = 1

POOL_WIDTH = 256
POOL_WINDOWS = (2, 4, 8, 16)
POOL_GROUPS = len(POOL_WINDOWS)
POOL_GROUP_DIM = POOL_WIDTH // POOL_GROUPS
MLA_HEADS = 6
QK_NOPE_DIM = 128
QK_ROPE_DIM = 64
QK_HEAD_DIM = QK_NOPE_DIM + QK_ROPE_DIM
V_HEAD_DIM = 128
Q_LORA_RANK = 512
KV_LORA_RANK = 256
ROPE_THETA = 10000.0
Q_BLOCK = 128
IN_WIDTH = POOL_WIDTH + Q_LORA_RANK + KV_LORA_RANK + QK_ROPE_DIM
MIX_WIDTH = POOL_WIDTH + MLA_HEADS * V_HEAD_DIM
N_GROUPS = 4
EXPERTS_PER_GROUP = 8
N_EXPERTS = N_GROUPS * EXPERTS_PER_GROUP
TOP_K = 2
D_EXPERT = 256
ROW_BLOCK = 128
N_MOD = 6
EPS = 1e-6

kernel_name = "hybrid_pool_mla_hmoe_adaln"


def rmsnorm(x, g):
    xf = x.astype(jnp.float32)
    y = xf * lax.rsqrt(jnp.mean(xf * xf, axis=-1, keepdims=True) + EPS)
    return (y * g.astype(jnp.float32)).astype(x.dtype)


def rope_tables(positions, dim):
    inv_freq = ROPE_THETA ** (-(jnp.arange(0, dim, 2, dtype=jnp.float32) / dim))
    ang = positions.astype(jnp.float32)[..., None] * inv_freq
    return jnp.cos(ang), jnp.sin(ang)


def apply_rope(x, cos, sin):
    xf = x.astype(jnp.float32)
    x1, x2 = jnp.split(xf, 2, axis=-1)
    return jnp.concatenate([x1 * cos - x2 * sin, x2 * cos + x1 * sin], axis=-1).astype(x.dtype)


def pool_mixer(p, w_pool, pool_scale):
    B, S, C = p.shape
    cs = jnp.cumsum(p.astype(jnp.float32), axis=1)
    t = jnp.arange(1, S + 1, dtype=jnp.float32)
    means = []
    for gi, w in enumerate(POOL_WINDOWS):
        cg = cs[..., gi * POOL_GROUP_DIM:(gi + 1) * POOL_GROUP_DIM]
        lagged = jnp.pad(cg, ((0, 0), (w, 0), (0, 0)))[:, :S]
        cnt = jnp.minimum(t, float(w))[None, :, None]
        means.append((cg - lagged) / cnt)
    pooled = jnp.concatenate(means, axis=-1).astype(p.dtype) - p
    y = jnp.einsum('bsgc,gcd->bsgd', pooled.reshape(B, S, POOL_GROUPS, POOL_GROUP_DIM), w_pool)
    return y.reshape(B, S, C) * pool_scale


def mla_attention(c_q, c_kv, k_rope_in, positions, q_norm_g, w_uq, kv_norm_g, w_ukv):
    B, S, _ = c_q.shape
    q = (rmsnorm(c_q, q_norm_g) @ w_uq).reshape(B, S, MLA_HEADS, QK_HEAD_DIM)
    q_nope, q_rope = q[..., :QK_NOPE_DIM], q[..., QK_NOPE_DIM:]
    kv = (rmsnorm(c_kv, kv_norm_g) @ w_ukv).reshape(B, S, MLA_HEADS, QK_NOPE_DIM + V_HEAD_DIM)
    k_nope, v = kv[..., :QK_NOPE_DIM], kv[..., QK_NOPE_DIM:]
    cos, sin = rope_tables(positions, QK_ROPE_DIM)
    q_rope = apply_rope(q_rope, cos[:, :, None, :], sin[:, :, None, :])
    k_rope = apply_rope(k_rope_in, cos, sin)
    scale = QK_HEAD_DIM ** -0.5
    key_idx = jnp.arange(S)

    def block(i):
        start = i * Q_BLOCK
        qn = lax.dynamic_slice_in_dim(q_nope, start, Q_BLOCK, axis=1)
        qr = lax.dynamic_slice_in_dim(q_rope, start, Q_BLOCK, axis=1)
        s = (jnp.einsum('bqhd,bkhd->bhqk', qn, k_nope)
             + jnp.einsum('bqhd,bkd->bhqk', qr, k_rope)).astype(jnp.float32) * scale
        q_idx = start + jnp.arange(Q_BLOCK)
        s = jnp.where(key_idx[None, :] <= q_idx[:, None], s, -jnp.inf)
        pr = jax.nn.softmax(s, axis=-1).astype(v.dtype)
        return jnp.einsum('bhqk,bkhd->bqhd', pr, v)

    out = lax.map(block, jnp.arange(S // Q_BLOCK))
    return out.transpose(1, 0, 2, 3, 4).reshape(B, S, MLA_HEADS * V_HEAD_DIM)


def hierarchical_moe(h, w_group, b_group, w_router, b_router, w_gate_up, w_down):
    B, S, D = h.shape
    N = B * S
    hf = h.reshape(N, D)
    g_logits = (hf @ w_group).astype(jnp.float32)
    g_prob = jax.nn.softmax(g_logits, axis=-1)
    g_sel = jnp.argmax(g_logits + b_group.astype(jnp.float32), axis=-1)
    e_logits = (hf @ w_router).astype(jnp.float32).reshape(N, N_GROUPS, EXPERTS_PER_GROUP)
    e_in = jnp.take_along_axis(e_logits, g_sel[:, None, None], axis=1)[:, 0]
    b_in = b_router.astype(jnp.float32).reshape(N_GROUPS, EXPERTS_PER_GROUP)[g_sel]
    _, local_idx = lax.top_k(e_in + b_in, TOP_K)
    sel_prob = jnp.take_along_axis(jax.nn.softmax(e_in, axis=-1), local_idx, axis=-1)
    sel_prob = sel_prob / jnp.sum(sel_prob, axis=-1, keepdims=True)
    gp = jnp.take_along_axis(g_prob, g_sel[:, None], axis=-1)
    weights = gp * sel_prob
    expert_ids = g_sel[:, None] * EXPERTS_PER_GROUP + local_idx

    P = N * TOP_K
    flat_e = expert_ids.reshape(P).astype(jnp.int32)
    flat_tok = jnp.repeat(jnp.arange(N, dtype=jnp.int32), TOP_K)
    flat_w = weights.reshape(P)
    order = jnp.argsort(flat_e)
    sorted_e = flat_e[order]
    counts = jnp.zeros((N_EXPERTS,), jnp.int32).at[flat_e].add(1)
    padded = ((counts + ROW_BLOCK - 1) // ROW_BLOCK) * ROW_BLOCK
    starts = jnp.cumsum(counts) - counts
    pends = jnp.cumsum(padded)
    pstarts = pends - padded
    dest = pstarts[sorted_e] + (jnp.arange(P, dtype=jnp.int32) - starts[sorted_e])
    P_pad = P + N_EXPERTS * ROW_BLOCK
    n_rb = P_pad // ROW_BLOCK
    row_tok = jnp.full((P_pad,), N, jnp.int32).at[dest].set(flat_tok[order])
    row_w = jnp.zeros((P_pad,), h.dtype).at[dest].set(flat_w[order].astype(h.dtype))
    block_e = jnp.searchsorted(pends, jnp.arange(n_rb, dtype=jnp.int32) * ROW_BLOCK, side='right')
    block_e = jnp.minimum(block_e, N_EXPERTS - 1)
    x_pad = jnp.concatenate([hf, jnp.zeros((1, D), hf.dtype)], axis=0)
    xs = x_pad[row_tok].reshape(n_rb, ROW_BLOCK, D)

    def expert_block(args):
        xb, e = args
        gu = xb @ w_gate_up[e]
        gate, up = gu[:, :D_EXPERT], gu[:, D_EXPERT:]
        return (jax.nn.silu(gate) * up) @ w_down[e]

    ys = lax.map(expert_block, (xs, block_e)).reshape(P_pad, D) * row_w[:, None]
    out = jax.ops.segment_sum(ys, row_tok, num_segments=N + 1)[:N]
    return out.reshape(B, S, D)


def setup_inputs(seed: int = 0) -> dict:
    key = jax.random.key(seed)
    ks = jax.random.split(key, 24)
    L, D = DEPTH, D_MODEL
    f32 = jnp.float32

    def nrm(k, shape, fan_in):
        return jax.random.normal(k, shape, f32) * (fan_in ** -0.5)

    def gain(k, shape):
        return 1.0 + 0.02 * jax.random.normal(k, shape, f32)

    x = jax.random.normal(ks[0], (BATCH, SEQ, D), f32)
    c = jax.random.normal(ks[1], (BATCH, D), f32)
    offsets = jax.random.randint(ks[2], (BATCH, 1), 0, 1024, jnp.int32)
    positions = (offsets + jnp.arange(SEQ, dtype=jnp.int32)[None, :]).astype(jnp.int32)
    return {
        "x": x,
        "c": c,
        "positions": positions,
        "w_mod": nrm(ks[3], (L, D, N_MOD * D), D),
        "b_mod": 0.02 * jax.random.normal(ks[4], (L, N_MOD * D), f32),
        "norm_mix_g": gain(ks[5], (L, D)),
        "w_in": nrm(ks[6], (L, D, IN_WIDTH), D),
        "w_pool": nrm(ks[7], (L, POOL_GROUPS, POOL_GROUP_DIM, POOL_GROUP_DIM), POOL_GROUP_DIM),
        "pool_scale": gain(ks[8], (L, POOL_WIDTH)),
        "q_norm_g": gain(ks[9], (L, Q_LORA_RANK)),
        "w_uq": nrm(ks[10], (L, Q_LORA_RANK, MLA_HEADS * QK_HEAD_DIM), Q_LORA_RANK),
        "kv_norm_g": gain(ks[11], (L, KV_LORA_RANK)),
        "w_ukv": nrm(ks[12], (L, KV_LORA_RANK, MLA_HEADS * (QK_NOPE_DIM + V_HEAD_DIM)), KV_LORA_RANK),
        "w_o": nrm(ks[13], (L, MIX_WIDTH, D), MIX_WIDTH),
        "norm_ffn_g": gain(ks[14], (L, D)),
        "w_group": nrm(ks[15], (L, D, N_GROUPS), D),
        "b_group": 0.01 * jax.random.normal(ks[16], (L, N_GROUPS), f32),
        "w_router": nrm(ks[17], (L, D, N_EXPERTS), D),
        "b_router": 0.01 * jax.random.normal(ks[18], (L, N_EXPERTS), f32),
        "w_gate_up": nrm(ks[19], (L, N_EXPERTS, D, 2 * D_EXPERT), D),
        "w_down": nrm(ks[20], (L, N_EXPERTS, D_EXPERT, D), D_EXPERT),
        "final_g": gain(ks[21], (D,)),
    }


def reference(x, c, positions, w_mod, b_mod, norm_mix_g, w_in, w_pool, pool_scale,
              q_norm_g, w_uq, kv_norm_g, w_ukv, w_o, norm_ffn_g, w_group, b_group,
              w_router, b_router, w_gate_up, w_down, final_g):
    c_act = jax.nn.silu(c)
    cut1 = POOL_WIDTH
    cut2 = cut1 + Q_LORA_RANK
    cut3 = cut2 + KV_LORA_RANK
    for l in range(DEPTH):
        mod = (c_act @ w_mod[l] + b_mod[l])[:, None, :]
        shift_a, scale_a, gate_a, shift_f, scale_f, gate_f = jnp.split(mod, N_MOD, axis=-1)
        h = rmsnorm(x, norm_mix_g[l]) * (1.0 + scale_a) + shift_a
        u = h @ w_in[l]
        y_pool = pool_mixer(u[..., :cut1], w_pool[l], pool_scale[l])
        y_mla = mla_attention(u[..., cut1:cut2], u[..., cut2:cut3], u[..., cut3:], positions,
                              q_norm_g[l], w_uq[l], kv_norm_g[l], w_ukv[l])
        mix = jnp.concatenate([y_pool, y_mla], axis=-1) @ w_o[l]
        x = x + gate_a * mix
        h = rmsnorm(x, norm_ffn_g[l]) * (1.0 + scale_f) + shift_f
        x = x + gate_f * hierarchical_moe(h, w_group[l], b_group[l], w_router[l], b_router[l],
                                          w_gate_up[l], w_down[l])
    return rmsnorm(x, final_g)
```

```python
import functools

import jax
import jax.numpy as jnp
from jax import lax
from jax.experimental import pallas as pl
from jax.experimental.pallas import tpu as pltpu

F32 = jnp.float32
BF16 = jnp.bfloat16

POOL_WINDOWS = (2, 4, 8, 16)
POOL_GROUP_DIM = 64
POOL_WIDTH = 256
MLA_HEADS = 6
QK_NOPE_DIM = 128
QK_ROPE_DIM = 64
QK_HEAD_DIM = QK_NOPE_DIM + QK_ROPE_DIM
V_HEAD_DIM = 128
Q_LORA_RANK = 512
KV_LORA_RANK = 256
ROPE_THETA = 10000.0
N_GROUPS = 4
EXPERTS_PER_GROUP = 8
N_EXPERTS = N_GROUPS * EXPERTS_PER_GROUP
D_EXPERT = 256
N_MOD = 6
EPS = 1e-6

POOL_HALO = 32
ROW_TILE = 512
ATTN_TQ = 512
ATTN_TK = 512
EXPERT_ROWS = 256
VMEM_LIMIT = 56 * 1024 * 1024
NEG_BIG = -1e30


def _cparams(*sem):
    return pltpu.CompilerParams(dimension_semantics=sem, vmem_limit_bytes=VMEM_LIMIT)


def _rms(x, g):
    return x * lax.rsqrt(jnp.mean(x * x, axis=-1, keepdims=True) + EPS) * g


def _mod_kernel(c_ref, w_ref, b_ref, o_ref):
    c = c_ref[...]
    ca = c / (1.0 + jnp.exp(-c))
    o_ref[...] = jnp.dot(ca, w_ref[...], precision=lax.Precision.HIGHEST,
                         preferred_element_type=F32) + b_ref[...]


def _mod_call(c, w_mod, b_mod):
    B, D = c.shape
    n_out = w_mod.shape[1]
    tn = 512
    return pl.pallas_call(
        _mod_kernel,
        out_shape=jax.ShapeDtypeStruct((B, n_out), F32),
        grid=(n_out // tn,),
        in_specs=[pl.BlockSpec((B, D), lambda j: (0, 0)),
                  pl.BlockSpec((D, tn), lambda j: (0, j)),
                  pl.BlockSpec((1, tn), lambda j: (0, j))],
        out_specs=pl.BlockSpec((B, tn), lambda j: (0, j)),
        compiler_params=_cparams("arbitrary"),
        name="mod",
    )(c, w_mod, b_mod.reshape(1, n_out))


def _pre_kernel(x_ref, pos_ref, mod_ref, g_ref, win_ref, invf_ref, wpool_ref, pscale_ref,
                qg_ref, wuq_ref, kvg_ref, wukv_ref,
                q_ref, k_ref, v_ref, yp_ref,
                pbuf, b2, b4, b8):
    T = x_ref.shape[0]
    H = POOL_HALO
    i = pl.program_id(1)

    shift = mod_ref[0]
    scale = mod_ref[1]
    h = _rms(x_ref[...], g_ref[...]) * (1.0 + scale) + shift
    u = jnp.dot(h.astype(BF16), win_ref[...], preferred_element_type=F32)

    p = u[:, :POOL_WIDTH]

    @pl.when(i == 0)
    def _():
        pbuf[0:H, :] = jnp.zeros((H, POOL_WIDTH), F32)

    pbuf[H:H + T, :] = p
    b2[8:T + H, :] = pbuf[8:T + H, :] + pbuf[7:T + H - 1, :]
    b4[16:T + H, :] = b2[16:T + H, :] + b2[14:T + H - 2, :]
    b8[24:T + H, :] = b4[24:T + H, :] + b4[20:T + H - 4, :]
    s2 = b2[H:T + H, :]
    s4 = b4[H:T + H, :]
    s8 = b8[H:T + H, :]
    s16 = b8[H:T + H, :] + b8[H - 8:T + H - 8, :]
    pbuf[0:H, :] = pbuf[T:T + H, :]

    lane = lax.broadcasted_iota(jnp.int32, (T, POOL_WIDTH), 1)
    t1 = (lax.broadcasted_iota(jnp.int32, (T, 1), 0) + (i * T + 1)).astype(F32)
    inv2 = 1.0 / jnp.minimum(t1, 2.0)
    inv4 = 1.0 / jnp.minimum(t1, 4.0)
    inv8 = 1.0 / jnp.minimum(t1, 8.0)
    inv16 = 1.0 / jnp.minimum(t1, 16.0)
    mean = jnp.where(lane < 64, s2 * inv2,
                     jnp.where(lane < 128, s4 * inv4,
                               jnp.where(lane < 192, s8 * inv8, s16 * inv16)))
    pooled = mean - p
    yp = jnp.dot(pooled.astype(BF16), wpool_ref[...], preferred_element_type=F32) * pscale_ref[...]
    yp_ref[...] = yp.astype(yp_ref.dtype)

    ang = pos_ref[...].astype(F32) * invf_ref[...]
    cosv = jnp.cos(ang)
    sinv = jnp.sin(ang)
    half = lax.broadcasted_iota(jnp.int32, (T, QK_ROPE_DIM), 1) < (QK_ROPE_DIM // 2)
    sin_signed = jnp.where(half, -sinv, sinv)

    cq = u[:, POOL_WIDTH:POOL_WIDTH + Q_LORA_RANK]
    ckv = u[:, POOL_WIDTH + Q_LORA_RANK:POOL_WIDTH + Q_LORA_RANK + KV_LORA_RANK]
    base = POOL_WIDTH + Q_LORA_RANK + KV_LORA_RANK
    kr = u[:, base:base + QK_ROPE_DIM]
    kr_sw = u[:, base + 128:base + 128 + QK_ROPE_DIM]
    k_rope = (kr * cosv + kr_sw * sin_signed).astype(BF16)

    qa = jnp.dot(_rms(cq, qg_ref[...]).astype(BF16), wuq_ref[...], preferred_element_type=F32)
    kv = jnp.dot(_rms(ckv, kvg_ref[...]).astype(BF16), wukv_ref[...], preferred_element_type=F32)
    nq = MLA_HEADS * QK_NOPE_DIM
    nr = MLA_HEADS * QK_ROPE_DIM
    for hd in range(MLA_HEADS):
        qn = qa[:, hd * QK_NOPE_DIM:(hd + 1) * QK_NOPE_DIM]
        qr = qa[:, nq + hd * QK_ROPE_DIM:nq + (hd + 1) * QK_ROPE_DIM]
        qr_sw = qa[:, nq + nr + hd * QK_ROPE_DIM:nq + nr + (hd + 1) * QK_ROPE_DIM]
        q_rope = qr * cosv + qr_sw * sin_signed
        q_ref[hd, :, 0:QK_NOPE_DIM] = qn.astype(BF16)
        q_ref[hd, :, QK_NOPE_DIM:QK_HEAD_DIM] = q_rope.astype(BF16)
        kn = kv[:, hd * 256:hd * 256 + QK_NOPE_DIM]
        k_ref[hd, :, 0:QK_NOPE_DIM] = kn.astype(BF16)
        k_ref[hd, :, QK_NOPE_DIM:QK_HEAD_DIM] = k_rope
        v_ref[hd, :, :] = kv[:, hd * 256 + QK_NOPE_DIM:(hd + 1) * 256].astype(BF16)


def _pre_call(x, pos, mod4, norm_g, w_in_ext, inv_freq2, wpool_bd, pool_scale, qg, w_uq_ext, kvg, w_ukv,
              T):
    B, S, D = x.shape
    nH = MLA_HEADS
    const = lambda shape: pl.BlockSpec(shape, lambda b, i: (0,) * len(shape))
    return pl.pallas_call(
        _pre_kernel,
        out_shape=(jax.ShapeDtypeStruct((B, nH, S, QK_HEAD_DIM), BF16),
                   jax.ShapeDtypeStruct((B, nH, S, QK_HEAD_DIM), BF16),
                   jax.ShapeDtypeStruct((B, nH, S, V_HEAD_DIM), BF16),
                   jax.ShapeDtypeStruct((B, S, POOL_WIDTH), BF16)),
        grid=(B, S // T),
        in_specs=[pl.BlockSpec((None, T, D), lambda b, i: (b, i, 0)),
                  pl.BlockSpec((None, T, 1), lambda b, i: (b, i, 0)),
                  pl.BlockSpec((None, N_MOD, 1, D), lambda b, i: (b, 0, 0, 0)),
                  const((1, D)),
                  const(w_in_ext.shape),
                  const((1, QK_ROPE_DIM)),
                  const(wpool_bd.shape),
                  const((1, POOL_WIDTH)),
                  const((1, Q_LORA_RANK)),
                  const(w_uq_ext.shape),
                  const((1, KV_LORA_RANK)),
                  const(w_ukv.shape)],
        out_specs=(pl.BlockSpec((None, nH, T, QK_HEAD_DIM), lambda b, i: (b, 0, i, 0)),
                   pl.BlockSpec((None, nH, T, QK_HEAD_DIM), lambda b, i: (b, 0, i, 0)),
                   pl.BlockSpec((None, nH, T, V_HEAD_DIM), lambda b, i: (b, 0, i, 0)),
                   pl.BlockSpec((None, T, POOL_WIDTH), lambda b, i: (b, i, 0))),
        scratch_shapes=[pltpu.VMEM((T + POOL_HALO, POOL_WIDTH), F32)] * 4,
        compiler_params=_cparams("arbitrary", "arbitrary"),
        name="pre",
    )(x, pos, mod4, norm_g, w_in_ext, inv_freq2, wpool_bd, pool_scale, qg, w_uq_ext, kvg, w_ukv)


def _attn_kernel(q_ref, k_ref, v_ref, o_ref, m_sc, l_sc, acc_sc):
    tq = q_ref.shape[0]
    tk = ATTN_TK
    i = pl.program_id(2)
    q = q_ref[...]
    m_sc[...] = jnp.full(m_sc.shape, NEG_BIG, F32)
    l_sc[...] = jnp.zeros(l_sc.shape, F32)
    acc_sc[...] = jnp.zeros(acc_sc.shape, F32)

    def step(j, masked):
        start = pl.multiple_of(j * tk, tk)
        k = k_ref[pl.ds(start, tk), :]
        v = v_ref[pl.ds(start, tk), :]
        s = lax.dot_general(q, k, (((1,), (1,)), ((), ())), preferred_element_type=F32)
        if masked:
            row = lax.broadcasted_iota(jnp.int32, (tq, tk), 0) + i * tq
            col = lax.broadcasted_iota(jnp.int32, (tq, tk), 1) + j * tk
            s = jnp.where(col <= row, s, NEG_BIG)
        m_prev = m_sc[...]
        m_new = jnp.maximum(m_prev, jnp.max(s, axis=-1, keepdims=True))
        alpha = jnp.exp(m_prev - m_new)
        p = jnp.exp(s - m_new)
        l_sc[...] = alpha * l_sc[...] + jnp.sum(p, axis=-1, keepdims=True)
        acc_sc[...] = alpha * acc_sc[...] + jnp.dot(p.astype(BF16), v, preferred_element_type=F32)
        m_sc[...] = m_new

    n_full = i * (tq // tk)

    def body(j, c):
        step(j, False)
        return c

    lax.fori_loop(0, n_full, body, 0)
    for d in range(tq // tk):
        step(n_full + d, True)
    o_ref[...] = (acc_sc[...] * (1.0 / l_sc[...])).astype(o_ref.dtype)


def _attn_call(q, k, v):
    B, nH, S, _ = q.shape
    tq = ATTN_TQ
    return pl.pallas_call(
        _attn_kernel,
        out_shape=jax.ShapeDtypeStruct((B, S, nH * V_HEAD_DIM), BF16),
        grid=(B, nH, S // tq),
        in_specs=[pl.BlockSpec((None, None, tq, QK_HEAD_DIM), lambda b, h, i: (b, h, i, 0)),
                  pl.BlockSpec((None, None, S, QK_HEAD_DIM), lambda b, h, i: (b, h, 0, 0)),
                  pl.BlockSpec((None, None, S, V_HEAD_DIM), lambda b, h, i: (b, h, 0, 0))],
        out_specs=pl.BlockSpec((None, tq, V_HEAD_DIM), lambda b, h, i: (b, i, h)),
        scratch_shapes=[pltpu.VMEM((tq, 1), F32), pltpu.VMEM((tq, 1), F32),
                        pltpu.VMEM((tq, V_HEAD_DIM), F32)],
        compiler_params=_cparams("arbitrary", "arbitrary", "arbitrary"),
        name="attn",
    )(q, k, v)


def _post_kernel(x_ref, yp_ref, ya_ref, mod_ref, wo_p_ref, wo_a_ref, g_ref,
                 wgT_ref, bg_ref, wrT_ref, br_ref,
                 x1_ref, h2_ref, mi_ref, mw_ref, cnt_ref,
                 carry):
    T = x_ref.shape[0]
    step = pl.program_id(0)

    @pl.when(step == 0)
    def _():
        carry[...] = jnp.zeros(carry.shape, F32)

    gate_a = mod_ref[2]
    shift_f = mod_ref[3]
    scale_f = mod_ref[4]
    mix = (jnp.dot(yp_ref[...], wo_p_ref[...], preferred_element_type=F32)
           + jnp.dot(ya_ref[...], wo_a_ref[...], preferred_element_type=F32))
    x1 = x_ref[...] + gate_a * mix
    x1_ref[...] = x1
    h2 = _rms(x1, g_ref[...]) * (1.0 + scale_f) + shift_f
    h2_ref[...] = h2
    hb = h2.astype(BF16)

    nt = (((1,), (1,)), ((), ()))
    gl = lax.dot_general(wgT_ref[...], hb, nt, preferred_element_type=F32)
    el = lax.dot_general(wrT_ref[...], hb, nt, preferred_element_type=F32)

    r8 = lax.broadcasted_iota(jnp.int32, (8, T), 0)
    gvalid = r8 < N_GROUPS
    gmax = jnp.max(jnp.where(gvalid, gl, NEG_BIG), axis=0, keepdims=True)
    gexp = jnp.where(gvalid, jnp.exp(gl - gmax), 0.0)
    g_prob = gexp / jnp.sum(gexp, axis=0, keepdims=True)
    gb = jnp.where(gvalid, gl + bg_ref[...], NEG_BIG)
    gbmax = jnp.max(gb, axis=0, keepdims=True)
    g_sel = jnp.min(jnp.where(gb == gbmax, r8, 8), axis=0, keepdims=True)
    gp = jnp.sum(jnp.where(r8 == g_sel, g_prob, 0.0), axis=0, keepdims=True)

    e_in = jnp.zeros((EXPERTS_PER_GROUP, T), F32)
    b_in = jnp.zeros((EXPERTS_PER_GROUP, T), F32)
    br = br_ref[...]
    for g in range(N_GROUPS):
        sel = g_sel == g
        e_in = jnp.where(sel, el[g * 8:(g + 1) * 8, :], e_in)
        b_in = jnp.where(sel, br[g * 8:(g + 1) * 8, :], b_in)
    eb = e_in + b_in
    m1 = jnp.max(eb, axis=0, keepdims=True)
    i1 = jnp.min(jnp.where(eb == m1, r8, 8), axis=0, keepdims=True)
    eb2 = jnp.where(r8 == i1, NEG_BIG, eb)
    m2 = jnp.max(eb2, axis=0, keepdims=True)
    i2 = jnp.min(jnp.where(eb2 == m2, r8, 8), axis=0, keepdims=True)
    emax = jnp.max(e_in, axis=0, keepdims=True)
    eexp = jnp.exp(e_in - emax)
    sp = eexp / jnp.sum(eexp, axis=0, keepdims=True)
    p1 = jnp.sum(jnp.where(r8 == i1, sp, 0.0), axis=0, keepdims=True)
    p2 = jnp.sum(jnp.where(r8 == i2, sp, 0.0), axis=0, keepdims=True)
    tot = p1 + p2
    w1 = gp * (p1 / tot)
    w2 = gp * (p2 / tot)
    e1 = g_sel * EXPERTS_PER_GROUP + i1
    e2 = g_sel * EXPERTS_PER_GROUP + i2

    r32 = lax.broadcasted_iota(jnp.int32, (N_EXPERTS, T), 0)
    oh1 = r32 == e1
    oh2 = r32 == e2
    oh = jnp.where(oh1 | oh2, 1.0, 0.0)
    upper = jnp.where(lax.broadcasted_iota(jnp.int32, (T, T), 0) < lax.broadcasted_iota(jnp.int32, (T, T), 1),
                      1.0, 0.0).astype(BF16)
    before = jnp.dot(oh.astype(BF16), upper, preferred_element_type=F32) + carry[...]
    rank1 = jnp.sum(jnp.where(oh1, before, 0.0), axis=0, keepdims=True)
    rank2 = jnp.sum(jnp.where(oh2, before, 0.0), axis=0, keepdims=True)
    new_carry = carry[...] + jnp.sum(oh, axis=1, keepdims=True)
    carry[...] = new_carry
    cnt_ref[...] = new_carry.astype(jnp.int32)

    mi_ref[0:1, :] = e1
    mi_ref[1:2, :] = e2
    mi_ref[2:3, :] = rank1.astype(jnp.int32)
    mi_ref[3:4, :] = rank2.astype(jnp.int32)
    mw_ref[0:1, :] = w1
    mw_ref[1:2, :] = w2


def _post_call(x2, yp2, ya2, mod4, wo_p, wo_a, g, wgT, bg, wrT, br, T, tiles_per_batch):
    N, D = x2.shape
    const = lambda shape: pl.BlockSpec(shape, lambda i: (0,) * len(shape))
    return pl.pallas_call(
        _post_kernel,
        out_shape=(jax.ShapeDtypeStruct((N, D), F32),
                   jax.ShapeDtypeStruct((N, D), F32),
                   jax.ShapeDtypeStruct((4, N), jnp.int32),
                   jax.ShapeDtypeStruct((2, N), F32),
                   jax.ShapeDtypeStruct((N_EXPERTS, 1), jnp.int32)),
        grid=(N // T,),
        in_specs=[pl.BlockSpec((T, D), lambda i: (i, 0)),
                  pl.BlockSpec((T, POOL_WIDTH), lambda i: (i, 0)),
                  pl.BlockSpec((T, MLA_HEADS * V_HEAD_DIM), lambda i: (i, 0)),
                  pl.BlockSpec((None, N_MOD, 1, D), lambda i: (i // tiles_per_batch, 0, 0, 0)),
                  const(wo_p.shape), const(wo_a.shape), const((1, D)),
                  const(wgT.shape), const(bg.shape), const(wrT.shape), const(br.shape)],
        out_specs=(pl.BlockSpec((T, D), lambda i: (i, 0)),
                   pl.BlockSpec((T, D), lambda i: (i, 0)),
                   pl.BlockSpec((4, T), lambda i: (0, i)),
                   pl.BlockSpec((2, T), lambda i: (0, i)),
                   pl.BlockSpec((N_EXPERTS, 1), lambda i: (0, 0))),
        scratch_shapes=[pltpu.VMEM((N_EXPERTS, 1), F32)],
        compiler_params=_cparams("arbitrary"),
        name="post",
    )(x2, yp2, ya2, mod4, wo_p, wo_a, g, wgT, bg, wrT, br)


def _row_copy(src_ref, src_row, dst_ref, dst_row, sem):
    return pltpu.make_async_copy(src_ref.at[pl.ds(src_row, 1)], dst_ref.at[pl.ds(dst_row, 1)], sem)


def _dispatch_kernel(dest_ref, h_ref, xs_in_ref, xs_ref, sem):
    del xs_in_ref
    T = h_ref.shape[0]
    n_tok = dest_ref.shape[0] // 2
    base = pl.program_id(0) * T

    def issue(r, c):
        _row_copy(h_ref, r, xs_ref, dest_ref[base + r], sem).start()
        _row_copy(h_ref, r, xs_ref, dest_ref[n_tok + base + r], sem).start()
        return c

    lax.fori_loop(0, T, issue, 0)

    def drain(r, c):
        _row_copy(h_ref, 0, xs_ref, 0, sem).wait()
        _row_copy(h_ref, 0, xs_ref, 0, sem).wait()
        return c

    lax.fori_loop(0, T, drain, 0)


def _dispatch_call(dest, h2, xs_init, T):
    N, D = h2.shape
    return pl.pallas_call(
        _dispatch_kernel,
        out_shape=jax.ShapeDtypeStruct(xs_init.shape, xs_init.dtype),
        grid_spec=pltpu.PrefetchScalarGridSpec(
            num_scalar_prefetch=1,
            grid=(N // T,),
            in_specs=[pl.BlockSpec((T, D), lambda i, d: (i, 0)),
                      pl.BlockSpec(memory_space=pl.ANY)],
            out_specs=pl.BlockSpec(memory_space=pl.ANY),
            scratch_shapes=[pltpu.SemaphoreType.DMA]),
        input_output_aliases={2: 0},
        compiler_params=_cparams("arbitrary"),
        name="dispatch",
    )(dest, h2, xs_init)


def _expert_kernel(be_ref, xs_ref, wgu_ref, wd_ref, ys_ref):
    del be_ref
    gu = jnp.dot(xs_ref[...].astype(BF16), wgu_ref[...], preferred_element_type=F32)
    gate = gu[:, :D_EXPERT]
    up = gu[:, D_EXPERT:]
    act = gate / (1.0 + jnp.exp(-gate)) * up
    ys_ref[...] = jnp.dot(act.astype(BF16), wd_ref[...], preferred_element_type=F32)


def _expert_call(block_e, xs, wgu, wd, RB):
    P, D = xs.shape
    return pl.pallas_call(
        _expert_kernel,
        out_shape=jax.ShapeDtypeStruct((P, D), F32),
        grid_spec=pltpu.PrefetchScalarGridSpec(
            num_scalar_prefetch=1,
            grid=(P // RB,),
            in_specs=[pl.BlockSpec((RB, D), lambda i, be: (i, 0)),
                      pl.BlockSpec((None, D, 2 * D_EXPERT), lambda i, be: (be[i], 0, 0)),
                      pl.BlockSpec((None, D_EXPERT, D), lambda i, be: (be[i], 0, 0))],
            out_specs=pl.BlockSpec((RB, D), lambda i, be: (i, 0))),
        compiler_params=_cparams("arbitrary"),
        name="experts",
    )(block_e, xs, wgu, wd)


def _combine_kernel(dest_ref, x1_ref, w_ref, mod_ref, g_ref, ys_ref, o_ref, buf, sem):
    T = x1_ref.shape[0]
    n_tok = dest_ref.shape[0] // 2
    base = pl.program_id(0) * T

    def issue(r, c):
        _row_copy(ys_ref, dest_ref[base + r], buf.at[0], r, sem).start()
        _row_copy(ys_ref, dest_ref[n_tok + base + r], buf.at[1], r, sem).start()
        return c

    lax.fori_loop(0, T, issue, 0)

    def drain(r, c):
        _row_copy(ys_ref, 0, buf.at[0], 0, sem).wait()
        _row_copy(ys_ref, 0, buf.at[1], 0, sem).wait()
        return c

    lax.fori_loop(0, T, drain, 0)

    w = w_ref[...]
    moe = w[:, 0:1] * buf[0] + w[:, 1:2] * buf[1]
    gate_f = mod_ref[5]
    o_ref[...] = _rms(x1_ref[...] + gate_f * moe, g_ref[...])


def _combine_call(dest, x1, w_tok, mod4, final_g, ys, T, tiles_per_batch):
    N, D = x1.shape
    return pl.pallas_call(
        _combine_kernel,
        out_shape=jax.ShapeDtypeStruct((N, D), F32),
        grid_spec=pltpu.PrefetchScalarGridSpec(
            num_scalar_prefetch=1,
            grid=(N // T,),
            in_specs=[pl.BlockSpec((T, D), lambda i, d: (i, 0)),
                      pl.BlockSpec((T, 2), lambda i, d: (i, 0)),
                      pl.BlockSpec((None, N_MOD, 1, D), lambda i, d: (i // tiles_per_batch, 0, 0, 0)),
                      pl.BlockSpec((1, D), lambda i, d: (0, 0)),
                      pl.BlockSpec(memory_space=pl.ANY)],
            out_specs=pl.BlockSpec((T, D), lambda i, d: (i, 0)),
            scratch_shapes=[pltpu.VMEM((2, T, D), F32), pltpu.SemaphoreType.DMA]),
        compiler_params=_cparams("arbitrary"),
        name="combine",
    )(dest, x1, w_tok, mod4, final_g, ys)


def _swap_halves(w):
    half = w.shape[-1] // 2
    return jnp.concatenate([w[..., half:], w[..., :half]], axis=-1)


def kernel(x, c, positions, w_mod, b_mod, norm_mix_g, w_in, w_pool, pool_scale, q_norm_g, w_uq, kv_norm_g, w_ukv, w_o, norm_ffn_g, w_group, b_group, w_router, b_router, w_gate_up, w_down, final_g):
    B, S, D = x.shape
    N = B * S
    depth = w_mod.shape[0]
    T = ROW_TILE
    RB = EXPERT_ROWS
    assert depth == 1, "the final RMSNorm is fused into the layer's combine step"
    assert S % T == 0 and S % ATTN_TQ == 0 and ATTN_TQ % ATTN_TK == 0
    tiles_per_batch = S // T
    nH = MLA_HEADS

    inv_freq = ROPE_THETA ** (-(jnp.arange(0, QK_ROPE_DIM, 2, dtype=F32) / QK_ROPE_DIM))
    inv_freq2 = jnp.concatenate([inv_freq, inv_freq]).reshape(1, QK_ROPE_DIM)
    pos3 = positions.reshape(B, S, 1)
    cut1 = POOL_WIDTH
    cut2 = cut1 + Q_LORA_RANK
    cut3 = cut2 + KV_LORA_RANK

    for l in range(depth):
        mod4 = _mod_call(c, w_mod[l], b_mod[l]).reshape(B, N_MOD, 1, D)

        wi = w_in[l]
        zpad = jnp.zeros((D, 128 - QK_ROPE_DIM), F32)
        w_in_ext = jnp.concatenate(
            [wi[:, :cut3], wi[:, cut3:], zpad, _swap_halves(wi[:, cut3:]), zpad], axis=1).astype(BF16)
        wq = w_uq[l].reshape(Q_LORA_RANK, nH, QK_HEAD_DIM)
        wq_n = wq[:, :, :QK_NOPE_DIM].reshape(Q_LORA_RANK, nH * QK_NOPE_DIM)
        wq_r = wq[:, :, QK_NOPE_DIM:]
        w_uq_ext = jnp.concatenate(
            [wq_n, wq_r.reshape(Q_LORA_RANK, nH * QK_ROPE_DIM),
             _swap_halves(wq_r).reshape(Q_LORA_RANK, nH * QK_ROPE_DIM)], axis=1).astype(BF16)
        qg = (q_norm_g[l] * (QK_HEAD_DIM ** -0.5)).reshape(1, Q_LORA_RANK)
        wpool_bd = jnp.zeros((POOL_WIDTH, POOL_WIDTH), F32)
        for g in range(len(POOL_WINDOWS)):
            sl = slice(g * POOL_GROUP_DIM, (g + 1) * POOL_GROUP_DIM)
            wpool_bd = wpool_bd.at[sl, sl].set(w_pool[l, g])
        wpool_bd = wpool_bd.astype(BF16)

        q, k, v, yp = _pre_call(
            x, pos3, mod4, norm_mix_g[l].reshape(1, D), w_in_ext, inv_freq2, wpool_bd,
            pool_scale[l].reshape(1, POOL_WIDTH), qg, w_uq_ext, kv_norm_g[l].reshape(1, KV_LORA_RANK),
            w_ukv[l].astype(BF16), T)
        ya = _attn_call(q, k, v)

        wo = w_o[l].astype(BF16)
        wgT = jnp.zeros((8, D), F32).at[:N_GROUPS].set(w_group[l].T).astype(BF16)
        bg = jnp.zeros((8, 1), F32).at[:N_GROUPS, 0].set(b_group[l])
        x1, h2, meta_i, meta_w, counts = _post_call(
            x.reshape(N, D), yp.reshape(N, POOL_WIDTH), ya.reshape(N, nH * V_HEAD_DIM), mod4,
            wo[:POOL_WIDTH], wo[POOL_WIDTH:], norm_ffn_g[l].reshape(1, D),
            wgT, bg, w_router[l].T.astype(BF16), b_router[l].reshape(N_EXPERTS, 1), T, tiles_per_batch)

        counts = counts[:, 0]
        padded = ((counts + RB - 1) // RB) * RB
        pends = jnp.cumsum(padded)
        pstarts = pends - padded
        dest = (pstarts[meta_i[0:2]] + meta_i[2:4]).reshape(2 * N).astype(jnp.int32)
        P_pad = 2 * N + N_EXPERTS * RB
        n_rb = P_pad // RB
        block_e = jnp.searchsorted(pends, jnp.arange(n_rb, dtype=jnp.int32) * RB, side='right')
        block_e = jnp.minimum(block_e, N_EXPERTS - 1).astype(jnp.int32)

        xs = _dispatch_call(dest, h2, jnp.zeros((P_pad, D), F32), T)
        ys = _expert_call(block_e, xs, w_gate_up[l].astype(BF16), w_down[l].astype(BF16), RB)
        x = _combine_call(dest, x1, meta_w.T, mod4, final_g.reshape(1, D), ys, T, tiles_per_batch).reshape(B, S, D)
    return x
```

```python
import functools

import jax
import jax.numpy as jnp
from jax import lax
from jax.experimental import pallas as pl
from jax.experimental.pallas import tpu as pltpu

F32 = jnp.float32
BF16 = jnp.bfloat16

POOL_WINDOWS = (2, 4, 8, 16)
POOL_GROUP_DIM = 64
POOL_WIDTH = 256
MLA_HEADS = 6
QK_NOPE_DIM = 128
QK_ROPE_DIM = 64
QK_HEAD_DIM = QK_NOPE_DIM + QK_ROPE_DIM
V_HEAD_DIM = 128
Q_LORA_RANK = 512
KV_LORA_RANK = 256
ROPE_THETA = 10000.0
N_GROUPS = 4
EXPERTS_PER_GROUP = 8
N_EXPERTS = N_GROUPS * EXPERTS_PER_GROUP
D_EXPERT = 256
N_MOD = 6
EPS = 1e-6

POOL_HALO = 32
ROW_TILE = 512
ATTN_TQ = 512
ATTN_TK = 512
EXPERT_ROWS = 256
VMEM_LIMIT = 56 * 1024 * 1024
NEG_BIG = -1e30
LOG2_E = 1.4426950408889634


def _cparams(*sem):
    return pltpu.CompilerParams(dimension_semantics=sem, vmem_limit_bytes=VMEM_LIMIT)


def _rms(x, g):
    return x * lax.rsqrt(jnp.mean(x * x, axis=-1, keepdims=True) + EPS) * g


def _mod_kernel(c_ref, w_ref, b_ref, o_ref):
    c = c_ref[...]
    ca = c / (1.0 + jnp.exp(-c))
    o_ref[...] = jnp.dot(ca, w_ref[...], precision=lax.Precision.HIGHEST,
                         preferred_element_type=F32) + b_ref[...]


def _mod_call(c, w_mod, b_mod):
    B, D = c.shape
    n_out = w_mod.shape[1]
    tn = 512
    return pl.pallas_call(
        _mod_kernel,
        out_shape=jax.ShapeDtypeStruct((B, n_out), F32),
        grid=(n_out // tn,),
        in_specs=[pl.BlockSpec((B, D), lambda j: (0, 0)),
                  pl.BlockSpec((D, tn), lambda j: (0, j)),
                  pl.BlockSpec((1, tn), lambda j: (0, j))],
        out_specs=pl.BlockSpec((B, tn), lambda j: (0, j)),
        compiler_params=_cparams("arbitrary"),
        name="mod",
    )(c, w_mod, b_mod.reshape(1, n_out))


def _pre_kernel(x_ref, pos_ref, posr_ref, mod_ref, g_ref, win_ref, invf_ref, invfc_ref, wpool_ref, pscale_ref,
                qg_ref, wuqT_ref, kvg_ref, wuk_ref, wuvT_ref,
                qT_ref, k_ref, vT_ref, yp_ref,
                pbuf, b2, b4, b8):
    T = x_ref.shape[0]
    H = POOL_HALO
    i = pl.program_id(1)

    shift = mod_ref[0]
    scale = mod_ref[1]
    h = _rms(x_ref[...], g_ref[...]) * (1.0 + scale) + shift
    u = jnp.dot(h.astype(BF16), win_ref[...], preferred_element_type=F32)

    p = u[:, :POOL_WIDTH]

    @pl.when(i == 0)
    def _():
        pbuf[0:H, :] = jnp.zeros((H, POOL_WIDTH), F32)

    pbuf[H:H + T, :] = p
    b2[8:T + H, :] = pbuf[8:T + H, :] + pbuf[7:T + H - 1, :]
    b4[16:T + H, :] = b2[16:T + H, :] + b2[14:T + H - 2, :]
    b8[24:T + H, :] = b4[24:T + H, :] + b4[20:T + H - 4, :]
    s2 = b2[H:T + H, :]
    s4 = b4[H:T + H, :]
    s8 = b8[H:T + H, :]
    s16 = b8[H:T + H, :] + b8[H - 8:T + H - 8, :]
    pbuf[0:H, :] = pbuf[T:T + H, :]

    lane = lax.broadcasted_iota(jnp.int32, (T, POOL_WIDTH), 1)
    t1 = (lax.broadcasted_iota(jnp.int32, (T, 1), 0) + (i * T + 1)).astype(F32)
    inv2 = 1.0 / jnp.minimum(t1, 2.0)
    inv4 = 1.0 / jnp.minimum(t1, 4.0)
    inv8 = 1.0 / jnp.minimum(t1, 8.0)
    inv16 = 1.0 / jnp.minimum(t1, 16.0)
    mean = jnp.where(lane < 64, s2 * inv2,
                     jnp.where(lane < 128, s4 * inv4,
                               jnp.where(lane < 192, s8 * inv8, s16 * inv16)))
    pooled = mean - p
    yp = jnp.dot(pooled.astype(BF16), wpool_ref[...], preferred_element_type=F32) * pscale_ref[...]
    yp_ref[...] = yp.astype(yp_ref.dtype)

    half_dim = QK_ROPE_DIM // 2
    ang = pos_ref[...].astype(F32) * invf_ref[...]
    sin_k = jnp.where(lax.broadcasted_iota(jnp.int32, (T, QK_ROPE_DIM), 1) < half_dim, -jnp.sin(ang), jnp.sin(ang))
    cos_k = jnp.cos(ang)
    angT = invfc_ref[...] * posr_ref[...].astype(F32)
    sin_q = jnp.where(lax.broadcasted_iota(jnp.int32, (QK_ROPE_DIM, T), 0) < half_dim, -jnp.sin(angT), jnp.sin(angT))
    cos_q = jnp.cos(angT)

    cq = u[:, POOL_WIDTH:POOL_WIDTH + Q_LORA_RANK]
    ckv = u[:, POOL_WIDTH + Q_LORA_RANK:POOL_WIDTH + Q_LORA_RANK + KV_LORA_RANK]
    base = POOL_WIDTH + Q_LORA_RANK + KV_LORA_RANK
    kr = u[:, base:base + QK_ROPE_DIM]
    kr_sw = u[:, base + 128:base + 128 + QK_ROPE_DIM]
    k_rope = (kr * cos_k + kr_sw * sin_k).astype(BF16)

    nt = (((1,), (1,)), ((), ()))
    cqn = _rms(cq, qg_ref[...]).astype(BF16)
    ckvn = _rms(ckv, kvg_ref[...]).astype(BF16)
    qaT = lax.dot_general(wuqT_ref[...], cqn, nt, preferred_element_type=F32)
    kn = jnp.dot(ckvn, wuk_ref[...], preferred_element_type=F32)
    vT = lax.dot_general(wuvT_ref[...], ckvn, nt, preferred_element_type=F32)
    nq = MLA_HEADS * QK_NOPE_DIM
    nr = MLA_HEADS * QK_ROPE_DIM
    for hd in range(MLA_HEADS):
        qr = qaT[nq + hd * QK_ROPE_DIM:nq + (hd + 1) * QK_ROPE_DIM, :]
        qr_sw = qaT[nq + nr + hd * QK_ROPE_DIM:nq + nr + (hd + 1) * QK_ROPE_DIM, :]
        qT_ref[hd, 0:QK_NOPE_DIM, :] = qaT[hd * QK_NOPE_DIM:(hd + 1) * QK_NOPE_DIM, :].astype(BF16)
        qT_ref[hd, QK_NOPE_DIM:QK_HEAD_DIM, :] = (qr * cos_q + qr_sw * sin_q).astype(BF16)
        k_ref[hd, :, 0:QK_NOPE_DIM] = kn[:, hd * QK_NOPE_DIM:(hd + 1) * QK_NOPE_DIM].astype(BF16)
        k_ref[hd, :, QK_NOPE_DIM:QK_HEAD_DIM] = k_rope
        vT_ref[hd, :, :] = vT[hd * V_HEAD_DIM:(hd + 1) * V_HEAD_DIM, :].astype(BF16)


def _pre_call(x, pos, posr, mod4, norm_g, w_in_ext, inv_freq2, inv_freq2c, wpool_bd, pool_scale, qg, w_uqT, kvg,
              w_uk, w_uvT, T):
    B, S, D = x.shape
    nH = MLA_HEADS
    const = lambda shape: pl.BlockSpec(shape, lambda b, i: (0,) * len(shape))
    return pl.pallas_call(
        _pre_kernel,
        out_shape=(jax.ShapeDtypeStruct((B, nH, QK_HEAD_DIM, S), BF16),
                   jax.ShapeDtypeStruct((B, nH, S, QK_HEAD_DIM), BF16),
                   jax.ShapeDtypeStruct((B, nH, V_HEAD_DIM, S), BF16),
                   jax.ShapeDtypeStruct((B, S, POOL_WIDTH), BF16)),
        grid=(B, S // T),
        in_specs=[pl.BlockSpec((None, T, D), lambda b, i: (b, i, 0)),
                  pl.BlockSpec((None, T, 1), lambda b, i: (b, i, 0)),
                  pl.BlockSpec((None, 1, T), lambda b, i: (b, 0, i)),
                  pl.BlockSpec((None, N_MOD, 1, D), lambda b, i: (b, 0, 0, 0)),
                  const((1, D)),
                  const(w_in_ext.shape),
                  const((1, QK_ROPE_DIM)),
                  const((QK_ROPE_DIM, 1)),
                  const(wpool_bd.shape),
                  const((1, POOL_WIDTH)),
                  const((1, Q_LORA_RANK)),
                  const(w_uqT.shape),
                  const((1, KV_LORA_RANK)),
                  const(w_uk.shape),
                  const(w_uvT.shape)],
        out_specs=(pl.BlockSpec((None, nH, QK_HEAD_DIM, T), lambda b, i: (b, 0, 0, i)),
                   pl.BlockSpec((None, nH, T, QK_HEAD_DIM), lambda b, i: (b, 0, i, 0)),
                   pl.BlockSpec((None, nH, V_HEAD_DIM, T), lambda b, i: (b, 0, 0, i)),
                   pl.BlockSpec((None, T, POOL_WIDTH), lambda b, i: (b, i, 0))),
        scratch_shapes=[pltpu.VMEM((T + POOL_HALO, POOL_WIDTH), F32)] * 4,
        compiler_params=_cparams("arbitrary", "arbitrary"),
        name="pre",
    )(x, pos, posr, mod4, norm_g, w_in_ext, inv_freq2, inv_freq2c, wpool_bd, pool_scale, qg, w_uqT, kvg, w_uk, w_uvT)


def _attn_kernel(qT_ref, k_ref, vT_ref, o_ref, m_sc, l_sc, acc_sc):
    tq = qT_ref.shape[1]
    tk = ATTN_TK
    i = pl.program_id(2)
    qT = qT_ref[...]
    m_sc[...] = jnp.full(m_sc.shape, NEG_BIG, F32)
    l_sc[...] = jnp.zeros(l_sc.shape, F32)
    acc_sc[...] = jnp.zeros(acc_sc.shape, F32)

    def step(j, masked):
        start = pl.multiple_of(j * tk, tk)
        k = k_ref[pl.ds(start, tk), :]
        vT = vT_ref[:, pl.ds(start, tk)]
        sT = jnp.dot(k, qT, preferred_element_type=F32)
        if masked:
            kv_pos = lax.broadcasted_iota(jnp.int32, (tk, tq), 0) + j * tk
            q_pos = lax.broadcasted_iota(jnp.int32, (tk, tq), 1) + i * tq
            sT = jnp.where(kv_pos <= q_pos, sT, NEG_BIG)
        m_prev = m_sc[...]
        m_new = jnp.maximum(m_prev, jnp.max(sT, axis=0, keepdims=True))
        alpha = jnp.exp2(m_prev - m_new)
        pT = jnp.exp2(sT - m_new)
        l_sc[...] = alpha * l_sc[...] + jnp.sum(pT, axis=0, keepdims=True)
        acc_sc[...] = alpha * acc_sc[...] + jnp.dot(vT, pT.astype(BF16), preferred_element_type=F32)
        m_sc[...] = m_new

    n_full = i * (tq // tk)

    def body(j, c):
        step(j, False)
        return c

    lax.fori_loop(0, n_full, body, 0)
    for d in range(tq // tk):
        step(n_full + d, True)
    o_ref[...] = (acc_sc[...] * (1.0 / l_sc[...])).T.astype(o_ref.dtype)


def _attn_call(qT, k, vT):
    B, nH, S, _ = k.shape
    tq = ATTN_TQ
    return pl.pallas_call(
        _attn_kernel,
        out_shape=jax.ShapeDtypeStruct((B, S, nH * V_HEAD_DIM), BF16),
        grid=(B, nH, S // tq),
        in_specs=[pl.BlockSpec((None, None, QK_HEAD_DIM, tq), lambda b, h, i: (b, h, 0, i)),
                  pl.BlockSpec((None, None, S, QK_HEAD_DIM), lambda b, h, i: (b, h, 0, 0)),
                  pl.BlockSpec((None, None, V_HEAD_DIM, S), lambda b, h, i: (b, h, 0, 0))],
        out_specs=pl.BlockSpec((None, tq, V_HEAD_DIM), lambda b, h, i: (b, i, h)),
        scratch_shapes=[pltpu.VMEM((1, tq), F32), pltpu.VMEM((1, tq), F32),
                        pltpu.VMEM((V_HEAD_DIM, tq), F32)],
        compiler_params=_cparams("arbitrary", "arbitrary", "arbitrary"),
        name="attn",
    )(qT, k, vT)


def _post_kernel(x_ref, yp_ref, ya_ref, mod_ref, wo_p_ref, wo_a_ref, g_ref,
                 wgT_ref, bg_ref, wrT_ref, br_ref,
                 x1_ref, h2_ref, mi_ref, mw_ref, cnt_ref,
                 carry):
    T = x_ref.shape[0]
    step = pl.program_id(0)

    @pl.when(step == 0)
    def _():
        carry[...] = jnp.zeros(carry.shape, F32)

    gate_a = mod_ref[2]
    shift_f = mod_ref[3]
    scale_f = mod_ref[4]
    mix = (jnp.dot(yp_ref[...], wo_p_ref[...], preferred_element_type=F32)
           + jnp.dot(ya_ref[...], wo_a_ref[...], preferred_element_type=F32))
    x1 = x_ref[...] + gate_a * mix
    x1_ref[...] = x1
    h2 = _rms(x1, g_ref[...]) * (1.0 + scale_f) + shift_f
    h2_ref[...] = h2
    hb = h2.astype(BF16)

    nt = (((1,), (1,)), ((), ()))
    gl = lax.dot_general(wgT_ref[...], hb, nt, preferred_element_type=F32)
    el = lax.dot_general(wrT_ref[...], hb, nt, preferred_element_type=F32)

    r8 = lax.broadcasted_iota(jnp.int32, (8, T), 0)
    gvalid = r8 < N_GROUPS
    gmax = jnp.max(jnp.where(gvalid, gl, NEG_BIG), axis=0, keepdims=True)
    gexp = jnp.where(gvalid, jnp.exp(gl - gmax), 0.0)
    g_prob = gexp / jnp.sum(gexp, axis=0, keepdims=True)
    gb = jnp.where(gvalid, gl + bg_ref[...], NEG_BIG)
    gbmax = jnp.max(gb, axis=0, keepdims=True)
    g_sel = jnp.min(jnp.where(gb == gbmax, r8, 8), axis=0, keepdims=True)
    gp = jnp.sum(jnp.where(r8 == g_sel, g_prob, 0.0), axis=0, keepdims=True)

    e_in = jnp.zeros((EXPERTS_PER_GROUP, T), F32)
    b_in = jnp.zeros((EXPERTS_PER_GROUP, T), F32)
    br = br_ref[...]
    for g in range(N_GROUPS):
        sel = g_sel == g
        e_in = jnp.where(sel, el[g * 8:(g + 1) * 8, :], e_in)
        b_in = jnp.where(sel, br[g * 8:(g + 1) * 8, :], b_in)
    eb = e_in + b_in
    m1 = jnp.max(eb, axis=0, keepdims=True)
    i1 = jnp.min(jnp.where(eb == m1, r8, 8), axis=0, keepdims=True)
    eb2 = jnp.where(r8 == i1, NEG_BIG, eb)
    m2 = jnp.max(eb2, axis=0, keepdims=True)
    i2 = jnp.min(jnp.where(eb2 == m2, r8, 8), axis=0, keepdims=True)
    emax = jnp.max(e_in, axis=0, keepdims=True)
    eexp = jnp.exp(e_in - emax)
    sp = eexp / jnp.sum(eexp, axis=0, keepdims=True)
    p1 = jnp.sum(jnp.where(r8 == i1, sp, 0.0), axis=0, keepdims=True)
    p2 = jnp.sum(jnp.where(r8 == i2, sp, 0.0), axis=0, keepdims=True)
    tot = p1 + p2
    w1 = gp * (p1 / tot)
    w2 = gp * (p2 / tot)
    e1 = g_sel * EXPERTS_PER_GROUP + i1
    e2 = g_sel * EXPERTS_PER_GROUP + i2

    r32 = lax.broadcasted_iota(jnp.int32, (N_EXPERTS, T), 0)
    oh1 = r32 == e1
    oh2 = r32 == e2
    oh = jnp.where(oh1 | oh2, 1.0, 0.0)
    upper = jnp.where(lax.broadcasted_iota(jnp.int32, (T, T), 0) < lax.broadcasted_iota(jnp.int32, (T, T), 1),
                      1.0, 0.0).astype(BF16)
    before = jnp.dot(oh.astype(BF16), upper, preferred_element_type=F32) + carry[...]
    rank1 = jnp.sum(jnp.where(oh1, before, 0.0), axis=0, keepdims=True)
    rank2 = jnp.sum(jnp.where(oh2, before, 0.0), axis=0, keepdims=True)
    new_carry = carry[...] + jnp.sum(oh, axis=1, keepdims=True)
    carry[...] = new_carry
    cnt_ref[...] = new_carry.astype(jnp.int32)

    mi_ref[0:1, :] = e1
    mi_ref[1:2, :] = e2
    mi_ref[2:3, :] = rank1.astype(jnp.int32)
    mi_ref[3:4, :] = rank2.astype(jnp.int32)
    mw_ref[0:1, :] = w1
    mw_ref[1:2, :] = w2


def _post_call(x2, yp2, ya2, mod4, wo_p, wo_a, g, wgT, bg, wrT, br, T, tiles_per_batch):
    N, D = x2.shape
    const = lambda shape: pl.BlockSpec(shape, lambda i: (0,) * len(shape))
    return pl.pallas_call(
        _post_kernel,
        out_shape=(jax.ShapeDtypeStruct((N, D), F32),
                   jax.ShapeDtypeStruct((N, D), F32),
                   jax.ShapeDtypeStruct((4, N), jnp.int32),
                   jax.ShapeDtypeStruct((2, N), F32),
                   jax.ShapeDtypeStruct((N_EXPERTS, 1), jnp.int32)),
        grid=(N // T,),
        in_specs=[pl.BlockSpec((T, D), lambda i: (i, 0)),
                  pl.BlockSpec((T, POOL_WIDTH), lambda i: (i, 0)),
                  pl.BlockSpec((T, MLA_HEADS * V_HEAD_DIM), lambda i: (i, 0)),
                  pl.BlockSpec((None, N_MOD, 1, D), lambda i: (i // tiles_per_batch, 0, 0, 0)),
                  const(wo_p.shape), const(wo_a.shape), const((1, D)),
                  const(wgT.shape), const(bg.shape), const(wrT.shape), const(br.shape)],
        out_specs=(pl.BlockSpec((T, D), lambda i: (i, 0)),
                   pl.BlockSpec((T, D), lambda i: (i, 0)),
                   pl.BlockSpec((4, T), lambda i: (0, i)),
                   pl.BlockSpec((2, T), lambda i: (0, i)),
                   pl.BlockSpec((N_EXPERTS, 1), lambda i: (0, 0))),
        scratch_shapes=[pltpu.VMEM((N_EXPERTS, 1), F32)],
        compiler_params=_cparams("arbitrary"),
        name="post",
    )(x2, yp2, ya2, mod4, wo_p, wo_a, g, wgT, bg, wrT, br)


def _row_copy(src_ref, src_row, dst_ref, dst_row, sem):
    return pltpu.make_async_copy(src_ref.at[pl.ds(src_row, 1)], dst_ref.at[pl.ds(dst_row, 1)], sem)


def _dispatch_kernel(dest_ref, pends_ref, h_ref, xs_ref, zbuf, sem):
    T = h_ref.shape[0]
    RB = zbuf.shape[0]
    n_tok = dest_ref.shape[0] // 2
    step = pl.program_id(0)
    base = step * T

    @pl.when(step == 0)
    def _():
        zbuf[...] = jnp.zeros(zbuf.shape, zbuf.dtype)

        def seg_copy(e):
            start = pl.multiple_of(pends_ref[e] - RB, RB)
            return pltpu.make_async_copy(zbuf, xs_ref.at[pl.ds(start, RB)], sem)

        def nonempty(e):
            prev = jnp.where(e == 0, 0, pends_ref[jnp.maximum(e - 1, 0)])
            return pends_ref[e] > prev

        def fill(e, c):
            @pl.when(nonempty(e))
            def _():
                seg_copy(e).start()
            return c

        def fill_wait(e, c):
            @pl.when(nonempty(e))
            def _():
                seg_copy(e).wait()
            return c

        lax.fori_loop(0, N_EXPERTS, fill, 0)
        lax.fori_loop(0, N_EXPERTS, fill_wait, 0)

        def tail_copy(b):
            return pltpu.make_async_copy(zbuf, xs_ref.at[pl.ds(pl.multiple_of(b * RB, RB), RB)], sem)

        n_used = pends_ref[N_EXPERTS - 1] // RB
        n_blocks = xs_ref.shape[0] // RB
        lax.fori_loop(n_used, n_blocks, lambda b, c: (tail_copy(b).start(), c)[1], 0)
        lax.fori_loop(n_used, n_blocks, lambda b, c: (tail_copy(b).wait(), c)[1], 0)

    def issue(r, c):
        _row_copy(h_ref, r, xs_ref, dest_ref[base + r], sem).start()
        _row_copy(h_ref, r, xs_ref, dest_ref[n_tok + base + r], sem).start()
        return c

    lax.fori_loop(0, T, issue, 0)

    def drain(r, c):
        _row_copy(h_ref, 0, xs_ref, 0, sem).wait()
        _row_copy(h_ref, 0, xs_ref, 0, sem).wait()
        return c

    lax.fori_loop(0, T, drain, 0)


def _dispatch_call(dest, pends, h2, P_pad, T, RB):
    N, D = h2.shape
    return pl.pallas_call(
        _dispatch_kernel,
        out_shape=jax.ShapeDtypeStruct((P_pad, D), h2.dtype),
        grid_spec=pltpu.PrefetchScalarGridSpec(
            num_scalar_prefetch=2,
            grid=(N // T,),
            in_specs=[pl.BlockSpec((T, D), lambda i, d, p: (i, 0))],
            out_specs=pl.BlockSpec(memory_space=pl.ANY),
            scratch_shapes=[pltpu.VMEM((RB, D), h2.dtype), pltpu.SemaphoreType.DMA]),
        compiler_params=_cparams("arbitrary"),
        name="dispatch",
    )(dest, pends, h2)


def _expert_kernel(be_ref, nused_ref, xs_ref, wgu_ref, wd_ref, ys_ref):
    del be_ref
    used = pl.program_id(0) < nused_ref[0]

    @pl.when(used)
    def _():
        gu = jnp.dot(xs_ref[...].astype(BF16), wgu_ref[...], preferred_element_type=F32)
        gate = gu[:, :D_EXPERT]
        up = gu[:, D_EXPERT:]
        act = gate / (1.0 + jnp.exp(-gate)) * up
        ys_ref[...] = jnp.dot(act.astype(BF16), wd_ref[...], preferred_element_type=F32)

    @pl.when(jnp.logical_not(used))
    def _():
        ys_ref[...] = jnp.zeros(ys_ref.shape, ys_ref.dtype)


def _expert_call(block_e, n_used, xs, wgu, wd, RB):
    P, D = xs.shape
    row_map = lambda i, be, nu: (i, 0)
    in_map = lambda i, be, nu: (jnp.minimum(i, nu[0] - 1), 0)
    return pl.pallas_call(
        _expert_kernel,
        out_shape=jax.ShapeDtypeStruct((P, D), F32),
        grid_spec=pltpu.PrefetchScalarGridSpec(
            num_scalar_prefetch=2,
            grid=(P // RB,),
            in_specs=[pl.BlockSpec((RB, D), in_map),
                      pl.BlockSpec((None, D, 2 * D_EXPERT), lambda i, be, nu: (be[i], 0, 0)),
                      pl.BlockSpec((None, D_EXPERT, D), lambda i, be, nu: (be[i], 0, 0))],
            out_specs=pl.BlockSpec((RB, D), row_map)),
        compiler_params=_cparams("arbitrary"),
        name="experts",
    )(block_e, n_used, xs, wgu, wd)


def _combine_kernel(dest_ref, x1_ref, w_ref, mod_ref, g_ref, ys_ref, o_ref, buf, sem):
    T = x1_ref.shape[0]
    n_tok = dest_ref.shape[0] // 2
    base = pl.program_id(0) * T

    def issue(r, c):
        _row_copy(ys_ref, dest_ref[base + r], buf.at[0], r, sem).start()
        _row_copy(ys_ref, dest_ref[n_tok + base + r], buf.at[1], r, sem).start()
        return c

    lax.fori_loop(0, T, issue, 0)

    def drain(r, c):
        _row_copy(ys_ref, 0, buf.at[0], 0, sem).wait()
        _row_copy(ys_ref, 0, buf.at[1], 0, sem).wait()
        return c

    lax.fori_loop(0, T, drain, 0)

    w = w_ref[...]
    moe = w[:, 0:1] * buf[0] + w[:, 1:2] * buf[1]
    gate_f = mod_ref[5]
    o_ref[...] = _rms(x1_ref[...] + gate_f * moe, g_ref[...])


def _combine_call(dest, x1, w_tok, mod4, final_g, ys, T, tiles_per_batch):
    N, D = x1.shape
    return pl.pallas_call(
        _combine_kernel,
        out_shape=jax.ShapeDtypeStruct((N, D), F32),
        grid_spec=pltpu.PrefetchScalarGridSpec(
            num_scalar_prefetch=1,
            grid=(N // T,),
            in_specs=[pl.BlockSpec((T, D), lambda i, d: (i, 0)),
                      pl.BlockSpec((T, 2), lambda i, d: (i, 0)),
                      pl.BlockSpec((None, N_MOD, 1, D), lambda i, d: (i // tiles_per_batch, 0, 0, 0)),
                      pl.BlockSpec((1, D), lambda i, d: (0, 0)),
                      pl.BlockSpec(memory_space=pl.ANY)],
            out_specs=pl.BlockSpec((T, D), lambda i, d: (i, 0)),
            scratch_shapes=[pltpu.VMEM((2, T, D), F32), pltpu.SemaphoreType.DMA]),
        compiler_params=_cparams("arbitrary"),
        name="combine",
    )(dest, x1, w_tok, mod4, final_g, ys)


def _swap_halves(w):
    half = w.shape[-1] // 2
    return jnp.concatenate([w[..., half:], w[..., :half]], axis=-1)


def kernel(x, c, positions, w_mod, b_mod, norm_mix_g, w_in, w_pool, pool_scale, q_norm_g, w_uq, kv_norm_g, w_ukv, w_o, norm_ffn_g, w_group, b_group, w_router, b_router, w_gate_up, w_down, final_g):
    B, S, D = x.shape
    N = B * S
    depth = w_mod.shape[0]
    T = ROW_TILE
    RB = EXPERT_ROWS
    assert depth == 1, "the final RMSNorm is fused into the layer's combine step"
    assert S % T == 0 and S % ATTN_TQ == 0 and ATTN_TQ % ATTN_TK == 0
    tiles_per_batch = S // T
    nH = MLA_HEADS

    inv_freq = ROPE_THETA ** (-(jnp.arange(0, QK_ROPE_DIM, 2, dtype=F32) / QK_ROPE_DIM))
    inv_freq2 = jnp.concatenate([inv_freq, inv_freq]).reshape(1, QK_ROPE_DIM)
    pos3 = positions.reshape(B, S, 1)
    posr = positions.reshape(B, 1, S)
    cut1 = POOL_WIDTH
    cut2 = cut1 + Q_LORA_RANK
    cut3 = cut2 + KV_LORA_RANK

    for l in range(depth):
        mod4 = _mod_call(c, w_mod[l], b_mod[l]).reshape(B, N_MOD, 1, D)

        wi = w_in[l]
        zpad = jnp.zeros((D, 128 - QK_ROPE_DIM), F32)
        w_in_ext = jnp.concatenate(
            [wi[:, :cut3], wi[:, cut3:], zpad, _swap_halves(wi[:, cut3:]), zpad], axis=1).astype(BF16)
        wq = w_uq[l].reshape(Q_LORA_RANK, nH, QK_HEAD_DIM)
        wq_n = wq[:, :, :QK_NOPE_DIM].reshape(Q_LORA_RANK, nH * QK_NOPE_DIM)
        wq_r = wq[:, :, QK_NOPE_DIM:]
        w_uqT = jnp.concatenate(
            [wq_n, wq_r.reshape(Q_LORA_RANK, nH * QK_ROPE_DIM),
             _swap_halves(wq_r).reshape(Q_LORA_RANK, nH * QK_ROPE_DIM)], axis=1).T.astype(BF16)
        wkv = w_ukv[l].reshape(KV_LORA_RANK, nH, QK_NOPE_DIM + V_HEAD_DIM)
        w_uk = wkv[:, :, :QK_NOPE_DIM].reshape(KV_LORA_RANK, nH * QK_NOPE_DIM).astype(BF16)
        w_uvT = wkv[:, :, QK_NOPE_DIM:].reshape(KV_LORA_RANK, nH * V_HEAD_DIM).T.astype(BF16)
        qg = (q_norm_g[l] * (QK_HEAD_DIM ** -0.5 * LOG2_E)).reshape(1, Q_LORA_RANK)
        wpool_bd = jnp.zeros((POOL_WIDTH, POOL_WIDTH), F32)
        for g in range(len(POOL_WINDOWS)):
            sl = slice(g * POOL_GROUP_DIM, (g + 1) * POOL_GROUP_DIM)
            wpool_bd = wpool_bd.at[sl, sl].set(w_pool[l, g])
        wpool_bd = wpool_bd.astype(BF16)

        qT, k, vT, yp = _pre_call(
            x, pos3, posr, mod4, norm_mix_g[l].reshape(1, D), w_in_ext, inv_freq2, inv_freq2.reshape(QK_ROPE_DIM, 1),
            wpool_bd, pool_scale[l].reshape(1, POOL_WIDTH), qg, w_uqT, kv_norm_g[l].reshape(1, KV_LORA_RANK),
            w_uk, w_uvT, T)
        ya = _attn_call(qT, k, vT)

        wo = w_o[l].astype(BF16)
        wgT = jnp.zeros((8, D), F32).at[:N_GROUPS].set(w_group[l].T).astype(BF16)
        bg = jnp.zeros((8, 1), F32).at[:N_GROUPS, 0].set(b_group[l])
        x1, h2, meta_i, meta_w, counts = _post_call(
            x.reshape(N, D), yp.reshape(N, POOL_WIDTH), ya.reshape(N, nH * V_HEAD_DIM), mod4,
            wo[:POOL_WIDTH], wo[POOL_WIDTH:], norm_ffn_g[l].reshape(1, D),
            wgT, bg, w_router[l].T.astype(BF16), b_router[l].reshape(N_EXPERTS, 1), T, tiles_per_batch)

        counts = counts[:, 0]
        padded = ((counts + RB - 1) // RB) * RB
        pends = jnp.cumsum(padded).astype(jnp.int32)
        pstarts = pends - padded
        eids = jnp.arange(N_EXPERTS, dtype=jnp.int32)
        seg_start = jnp.sum(jnp.where(meta_i[0:2, :, None] == eids, pstarts, 0), axis=-1)
        dest = (seg_start + meta_i[2:4]).reshape(2 * N).astype(jnp.int32)
        P_pad = 2 * N + N_EXPERTS * RB
        n_rb = P_pad // RB
        n_used = (pends[-1:] // RB).astype(jnp.int32)
        block_start = jnp.minimum(jnp.arange(n_rb, dtype=jnp.int32), n_used - 1) * RB
        block_e = jnp.sum((pends[None, :] <= block_start[:, None]).astype(jnp.int32), axis=1)

        xs = _dispatch_call(dest, pends, h2, P_pad, T, RB)
        ys = _expert_call(block_e, n_used, xs, w_gate_up[l].astype(BF16), w_down[l].astype(BF16), RB)
        x = _combine_call(dest, x1, meta_w.T, mod4, final_g.reshape(1, D), ys, T, tiles_per_batch).reshape(B, S, D)
    return x
```

```python
import jax
import jax.numpy as jnp
from jax import lax
from jax.experimental import pallas as pl
from jax.experimental.pallas import tpu as pltpu

F32 = jnp.float32
BF16 = jnp.bfloat16
U32 = jnp.uint32
I32 = jnp.int32

POOL_WINDOWS = (2, 4, 8, 16)
POOL_GROUP_DIM = 64
POOL_WIDTH = 256
MLA_HEADS = 6
QK_NOPE_DIM = 128
QK_ROPE_DIM = 64
QK_HEAD_DIM = QK_NOPE_DIM + QK_ROPE_DIM
V_HEAD_DIM = 128
Q_LORA_RANK = 512
KV_LORA_RANK = 256
ROPE_THETA = 10000.0
N_GROUPS = 4
EXPERTS_PER_GROUP = 8
N_EXPERTS = N_GROUPS * EXPERTS_PER_GROUP
D_EXPERT = 256
N_MOD = 6
EPS = 1e-6

SUBLANES = 8
POOL_HALO = 32
ROW_TILE = 512
ATTN_TQ = 512
ATTN_TK = 512
EXPERT_ROWS = 256
LOCAL_ROWS = 2 * ROW_TILE + 256
assert LOCAL_ROWS >= 2 * ROW_TILE + N_EXPERTS * (SUBLANES - 1)
STRIP_BITS = (2 * ROW_TILE // SUBLANES).bit_length()
VMEM_LIMIT = 56 * 1024 * 1024
NEG_BIG = -1e30
LOG2_E = 1.4426950408889634
HI_MASK = 0xFFFF0000


def _cparams(*sem):
    return pltpu.CompilerParams(dimension_semantics=sem, vmem_limit_bytes=VMEM_LIMIT)


def _rms(x, g):
    return x * lax.rsqrt(jnp.mean(x * x, axis=-1, keepdims=True) + EPS) * g


def _pack_bf16_pair(lo, hi):
    return lax.bitcast_convert_type(hi, U32) | (lax.bitcast_convert_type(lo, U32) >> 16)


def _unpack_bf16_pair(w):
    lo = lax.bitcast_convert_type(w << 16, F32).astype(BF16)
    hi = lax.bitcast_convert_type(w & jnp.uint32(HI_MASK), F32).astype(BF16)
    return lo, hi


def _mod_kernel(c_ref, w_ref, b_ref, o_ref):
    c = c_ref[...]
    ca = c / (1.0 + jnp.exp(-c))
    o_ref[...] = jnp.dot(ca, w_ref[...], precision=lax.Precision.HIGHEST,
                         preferred_element_type=F32) + b_ref[...]


def _mod_call(c, w_mod, b_mod):
    B, D = c.shape
    n_out = w_mod.shape[1]
    tn = 512
    return pl.pallas_call(
        _mod_kernel,
        out_shape=jax.ShapeDtypeStruct((B, n_out), F32),
        grid=(n_out // tn,),
        in_specs=[pl.BlockSpec((B, D), lambda j: (0, 0)),
                  pl.BlockSpec((D, tn), lambda j: (0, j)),
                  pl.BlockSpec((1, tn), lambda j: (0, j))],
        out_specs=pl.BlockSpec((B, tn), lambda j: (0, j)),
        compiler_params=_cparams("arbitrary"),
        name="mod",
    )(c, w_mod, b_mod.reshape(1, n_out))


def _pre_kernel(x_ref, posr_ref, mod_ref, g_ref, win_ref, wkrT_ref, invfc_ref, wpool_ref, pscale_ref,
                qg_ref, wuqT_ref, kvg_ref, wuk_ref, wuvT_ref,
                qT_ref, k_ref, vT_ref, yp_ref,
                pbuf, b2, b4, b8):
    T = x_ref.shape[0]
    H = POOL_HALO
    i = pl.program_id(1)
    nt = (((1,), (1,)), ((), ()))

    shift = mod_ref[0]
    scale = mod_ref[1]
    hb = (_rms(x_ref[...], g_ref[...]) * (1.0 + scale) + shift).astype(BF16)
    u = jnp.dot(hb, win_ref[...], preferred_element_type=F32)
    krT = lax.dot_general(wkrT_ref[...], hb, nt, preferred_element_type=F32)

    p = u[:, :POOL_WIDTH]

    @pl.when(i == 0)
    def _():
        pbuf[0:H, :] = jnp.zeros((H, POOL_WIDTH), F32)

    pbuf[H:H + T, :] = p
    b2[8:T + H, :] = pbuf[8:T + H, :] + pbuf[7:T + H - 1, :]
    b4[16:T + H, :] = b2[16:T + H, :] + b2[14:T + H - 2, :]
    b8[24:T + H, :] = b4[24:T + H, :] + b4[20:T + H - 4, :]
    s2 = b2[H:T + H, :]
    s4 = b4[H:T + H, :]
    s8 = b8[H:T + H, :]
    s16 = b8[H:T + H, :] + b8[H - 8:T + H - 8, :]
    pbuf[0:H, :] = pbuf[T:T + H, :]

    lane = lax.broadcasted_iota(I32, (T, POOL_WIDTH), 1)
    t1 = (lax.broadcasted_iota(I32, (T, 1), 0) + (i * T + 1)).astype(F32)
    inv2 = 1.0 / jnp.minimum(t1, 2.0)
    inv4 = 1.0 / jnp.minimum(t1, 4.0)
    inv8 = 1.0 / jnp.minimum(t1, 8.0)
    inv16 = 1.0 / jnp.minimum(t1, 16.0)
    mean = jnp.where(lane < 64, s2 * inv2,
                     jnp.where(lane < 128, s4 * inv4,
                               jnp.where(lane < 192, s8 * inv8, s16 * inv16)))
    pooled = mean - p
    yp = jnp.dot(pooled.astype(BF16), wpool_ref[...], preferred_element_type=F32) * pscale_ref[...]
    yp_ref[...] = yp.astype(yp_ref.dtype)

    ang = invfc_ref[...] * posr_ref[...].astype(F32)
    cos_h = jnp.cos(ang)
    sin_h = jnp.sin(ang)
    cos_t = jnp.concatenate([cos_h, cos_h], axis=0)
    sin_t = jnp.concatenate([-sin_h, sin_h], axis=0)

    k_rope = (krT[0:QK_ROPE_DIM] * cos_t + krT[QK_ROPE_DIM:2 * QK_ROPE_DIM] * sin_t).T.astype(BF16)

    cq = u[:, POOL_WIDTH:POOL_WIDTH + Q_LORA_RANK]
    ckv = u[:, POOL_WIDTH + Q_LORA_RANK:POOL_WIDTH + Q_LORA_RANK + KV_LORA_RANK]
    cqn = _rms(cq, qg_ref[...]).astype(BF16)
    ckvn = _rms(ckv, kvg_ref[...]).astype(BF16)
    qaT = lax.dot_general(wuqT_ref[...], cqn, nt, preferred_element_type=F32)
    kn = jnp.dot(ckvn, wuk_ref[...], preferred_element_type=F32)
    vT = lax.dot_general(wuvT_ref[...], ckvn, nt, preferred_element_type=F32)
    nq = MLA_HEADS * QK_NOPE_DIM
    nr = MLA_HEADS * QK_ROPE_DIM
    for hd in range(MLA_HEADS):
        qr = qaT[nq + hd * QK_ROPE_DIM:nq + (hd + 1) * QK_ROPE_DIM, :]
        qr_sw = qaT[nq + nr + hd * QK_ROPE_DIM:nq + nr + (hd + 1) * QK_ROPE_DIM, :]
        qT_ref[hd, 0:QK_NOPE_DIM, :] = qaT[hd * QK_NOPE_DIM:(hd + 1) * QK_NOPE_DIM, :].astype(BF16)
        qT_ref[hd, QK_NOPE_DIM:QK_HEAD_DIM, :] = (qr * cos_t + qr_sw * sin_t).astype(BF16)
        k_ref[hd, :, 0:QK_NOPE_DIM] = kn[:, hd * QK_NOPE_DIM:(hd + 1) * QK_NOPE_DIM].astype(BF16)
        k_ref[hd, :, QK_NOPE_DIM:QK_HEAD_DIM] = k_rope
        vT_ref[hd, :, :] = vT[hd * V_HEAD_DIM:(hd + 1) * V_HEAD_DIM, :].astype(BF16)


def _pre_call(x, posr, mod4, norm_g, w_in_main, w_krT, inv_freq_col, wpool_bd, pool_scale, qg, w_uqT, kvg,
              w_uk, w_uvT, T):
    B, S, D = x.shape
    nH = MLA_HEADS
    const = lambda shape: pl.BlockSpec(shape, lambda b, i: (0,) * len(shape))
    return pl.pallas_call(
        _pre_kernel,
        out_shape=(jax.ShapeDtypeStruct((B, nH, QK_HEAD_DIM, S), BF16),
                   jax.ShapeDtypeStruct((B, nH, S, QK_HEAD_DIM), BF16),
                   jax.ShapeDtypeStruct((B, nH, V_HEAD_DIM, S), BF16),
                   jax.ShapeDtypeStruct((B, S, POOL_WIDTH), BF16)),
        grid=(B, S // T),
        in_specs=[pl.BlockSpec((None, T, D), lambda b, i: (b, i, 0)),
                  pl.BlockSpec((None, 1, T), lambda b, i: (b, 0, i)),
                  pl.BlockSpec((None, N_MOD, 1, D), lambda b, i: (b, 0, 0, 0)),
                  const((1, D)),
                  const(w_in_main.shape),
                  const(w_krT.shape),
                  const(inv_freq_col.shape),
                  const(wpool_bd.shape),
                  const((1, POOL_WIDTH)),
                  const((1, Q_LORA_RANK)),
                  const(w_uqT.shape),
                  const((1, KV_LORA_RANK)),
                  const(w_uk.shape),
                  const(w_uvT.shape)],
        out_specs=(pl.BlockSpec((None, nH, QK_HEAD_DIM, T), lambda b, i: (b, 0, 0, i)),
                   pl.BlockSpec((None, nH, T, QK_HEAD_DIM), lambda b, i: (b, 0, i, 0)),
                   pl.BlockSpec((None, nH, V_HEAD_DIM, T), lambda b, i: (b, 0, 0, i)),
                   pl.BlockSpec((None, T, POOL_WIDTH), lambda b, i: (b, i, 0))),
        scratch_shapes=[pltpu.VMEM((T + POOL_HALO, POOL_WIDTH), F32)] * 4,
        compiler_params=_cparams("arbitrary", "arbitrary"),
        name="pre",
    )(x, posr, mod4, norm_g, w_in_main, w_krT, inv_freq_col, wpool_bd, pool_scale, qg, w_uqT, kvg, w_uk, w_uvT)


def _attn_kernel(qT_ref, k_ref, vT_ref, o_ref, m_sc, l_sc, acc_sc):
    tq = qT_ref.shape[1]
    tk = ATTN_TK
    i = pl.program_id(2)
    qT = qT_ref[...]
    m_sc[...] = jnp.full(m_sc.shape, NEG_BIG, F32)
    l_sc[...] = jnp.zeros(l_sc.shape, F32)
    acc_sc[...] = jnp.zeros(acc_sc.shape, F32)

    def step(j, masked):
        start = pl.multiple_of(j * tk, tk)
        k = k_ref[pl.ds(start, tk), :]
        vT = vT_ref[:, pl.ds(start, tk)]
        sT = jnp.dot(k, qT, preferred_element_type=F32)
        if masked:
            kv_pos = lax.broadcasted_iota(I32, (tk, tq), 0) + j * tk
            q_pos = lax.broadcasted_iota(I32, (tk, tq), 1) + i * tq
            sT = jnp.where(kv_pos <= q_pos, sT, NEG_BIG)
        m_prev = m_sc[...]
        m_new = jnp.maximum(m_prev, jnp.max(sT, axis=0, keepdims=True))
        alpha = jnp.exp2(m_prev - m_new)
        pT = jnp.exp2(sT - m_new)
        l_sc[...] = alpha * l_sc[...] + jnp.sum(pT, axis=0, keepdims=True)
        acc_sc[...] = alpha * acc_sc[...] + jnp.dot(vT, pT.astype(BF16), preferred_element_type=F32)
        m_sc[...] = m_new

    n_full = i * (tq // tk)

    def body(j, c):
        step(j, False)
        return c

    lax.fori_loop(0, n_full, body, 0)
    for d in range(tq // tk):
        step(n_full + d, True)
    o_ref[...] = (acc_sc[...] * (1.0 / l_sc[...])).T.astype(o_ref.dtype)


def _attn_call(qT, k, vT):
    B, nH, S, _ = k.shape
    tq = ATTN_TQ
    return pl.pallas_call(
        _attn_kernel,
        out_shape=jax.ShapeDtypeStruct((B, S, nH * V_HEAD_DIM), BF16),
        grid=(B, nH, S // tq),
        in_specs=[pl.BlockSpec((None, None, QK_HEAD_DIM, tq), lambda b, h, i: (b, h, 0, i)),
                  pl.BlockSpec((None, None, S, QK_HEAD_DIM), lambda b, h, i: (b, h, 0, 0)),
                  pl.BlockSpec((None, None, V_HEAD_DIM, S), lambda b, h, i: (b, h, 0, 0))],
        out_specs=pl.BlockSpec((None, tq, V_HEAD_DIM), lambda b, h, i: (b, i, h)),
        scratch_shapes=[pltpu.VMEM((1, tq), F32), pltpu.VMEM((1, tq), F32),
                        pltpu.VMEM((V_HEAD_DIM, tq), F32)],
        compiler_params=_cparams("arbitrary", "arbitrary", "arbitrary"),
        name="attn",
    )(qT, k, vT)


def _post_kernel(x_ref, yp_ref, ya_ref, mod_ref, wo_p_ref, wo_a_ref, g_ref,
                 wgT_ref, bg_ref, wrT_ref, br_ref,
                 x1_ref, h2_ref, mi_ref, mw_ref, cnt_ref):
    T = x_ref.shape[0]

    gate_a = mod_ref[2]
    shift_f = mod_ref[3]
    scale_f = mod_ref[4]
    mix = (jnp.dot(yp_ref[...], wo_p_ref[...], preferred_element_type=F32)
           + jnp.dot(ya_ref[...], wo_a_ref[...], preferred_element_type=F32))
    x1 = x_ref[...] + gate_a * mix
    x1_ref[...] = x1
    hb = (_rms(x1, g_ref[...]) * (1.0 + scale_f) + shift_f).astype(BF16)
    h2_ref[...] = hb

    nt = (((1,), (1,)), ((), ()))
    gl = lax.dot_general(wgT_ref[...], hb, nt, preferred_element_type=F32)
    el = lax.dot_general(wrT_ref[...], hb, nt, preferred_element_type=F32)

    r8 = lax.broadcasted_iota(I32, (8, T), 0)
    gvalid = r8 < N_GROUPS
    gmax = jnp.max(jnp.where(gvalid, gl, NEG_BIG), axis=0, keepdims=True)
    gexp = jnp.where(gvalid, jnp.exp(gl - gmax), 0.0)
    g_prob = gexp / jnp.sum(gexp, axis=0, keepdims=True)
    gb = jnp.where(gvalid, gl + bg_ref[...], NEG_BIG)
    gbmax = jnp.max(gb, axis=0, keepdims=True)
    g_sel = jnp.min(jnp.where(gb == gbmax, r8, 8), axis=0, keepdims=True)
    gp = jnp.sum(jnp.where(r8 == g_sel, g_prob, 0.0), axis=0, keepdims=True)

    e_in = jnp.zeros((EXPERTS_PER_GROUP, T), F32)
    b_in = jnp.zeros((EXPERTS_PER_GROUP, T), F32)
    br = br_ref[...]
    for g in range(N_GROUPS):
        sel = g_sel == g
        e_in = jnp.where(sel, el[g * 8:(g + 1) * 8, :], e_in)
        b_in = jnp.where(sel, br[g * 8:(g + 1) * 8, :], b_in)
    eb = e_in + b_in
    m1 = jnp.max(eb, axis=0, keepdims=True)
    i1 = jnp.min(jnp.where(eb == m1, r8, 8), axis=0, keepdims=True)
    eb2 = jnp.where(r8 == i1, NEG_BIG, eb)
    m2 = jnp.max(eb2, axis=0, keepdims=True)
    i2 = jnp.min(jnp.where(eb2 == m2, r8, 8), axis=0, keepdims=True)
    emax = jnp.max(e_in, axis=0, keepdims=True)
    eexp = jnp.exp(e_in - emax)
    sp = eexp / jnp.sum(eexp, axis=0, keepdims=True)
    p1 = jnp.sum(jnp.where(r8 == i1, sp, 0.0), axis=0, keepdims=True)
    p2 = jnp.sum(jnp.where(r8 == i2, sp, 0.0), axis=0, keepdims=True)
    tot = p1 + p2
    w1 = gp * (p1 / tot)
    w2 = gp * (p2 / tot)
    e1 = g_sel * EXPERTS_PER_GROUP + i1
    e2 = g_sel * EXPERTS_PER_GROUP + i2

    r32 = lax.broadcasted_iota(I32, (N_EXPERTS, T), 0)
    oh1 = r32 == e1
    oh2 = r32 == e2
    oh = jnp.where(oh1 | oh2, 1.0, 0.0)
    upper = jnp.where(lax.broadcasted_iota(I32, (T, T), 0) < lax.broadcasted_iota(I32, (T, T), 1),
                      1.0, 0.0).astype(BF16)
    before = jnp.dot(oh.astype(BF16), upper, preferred_element_type=F32)
    cnt = jnp.sum(oh, axis=1, keepdims=True)
    run8 = jnp.floor((cnt + (SUBLANES - 1.0)) * (1.0 / SUBLANES))
    lower = jnp.where(lax.broadcasted_iota(I32, (N_EXPERTS, N_EXPERTS), 1)
                      < lax.broadcasted_iota(I32, (N_EXPERTS, N_EXPERTS), 0), 1.0, 0.0).astype(BF16)
    run_start = jnp.dot(lower, jnp.broadcast_to(run8, (N_EXPERTS, 128)).astype(BF16),
                        preferred_element_type=F32)[:, 0:1] * float(SUBLANES)
    pos = before + run_start
    row1 = jnp.sum(jnp.where(oh1, pos, 0.0), axis=0, keepdims=True)
    row2 = jnp.sum(jnp.where(oh2, pos, 0.0), axis=0, keepdims=True)
    cnt_ref[...] = cnt.astype(I32)

    mi_ref[0:1, :] = e1
    mi_ref[1:2, :] = e2
    mi_ref[2:3, :] = row1.astype(I32)
    mi_ref[3:4, :] = row2.astype(I32)
    mw_ref[0:1, :] = w1
    mw_ref[1:2, :] = w2


def _post_call(x2, yp2, ya2, mod4, wo_p, wo_a, g, wgT, bg, wrT, br, T, tiles_per_batch):
    N, D = x2.shape
    const = lambda shape: pl.BlockSpec(shape, lambda i: (0,) * len(shape))
    return pl.pallas_call(
        _post_kernel,
        out_shape=(jax.ShapeDtypeStruct((N, D), F32),
                   jax.ShapeDtypeStruct((N, D), BF16),
                   jax.ShapeDtypeStruct((4, N), I32),
                   jax.ShapeDtypeStruct((2, N), F32),
                   jax.ShapeDtypeStruct((N // T, N_EXPERTS, 1), I32)),
        grid=(N // T,),
        in_specs=[pl.BlockSpec((T, D), lambda i: (i, 0)),
                  pl.BlockSpec((T, POOL_WIDTH), lambda i: (i, 0)),
                  pl.BlockSpec((T, MLA_HEADS * V_HEAD_DIM), lambda i: (i, 0)),
                  pl.BlockSpec((None, N_MOD, 1, D), lambda i: (i // tiles_per_batch, 0, 0, 0)),
                  const(wo_p.shape), const(wo_a.shape), const((1, D)),
                  const(wgT.shape), const(bg.shape), const(wrT.shape), const(br.shape)],
        out_specs=(pl.BlockSpec((T, D), lambda i: (i, 0)),
                   pl.BlockSpec((T, D), lambda i: (i, 0)),
                   pl.BlockSpec((4, T), lambda i: (0, i)),
                   pl.BlockSpec((2, T), lambda i: (0, i)),
                   pl.BlockSpec((None, N_EXPERTS, 1), lambda i: (i, 0, 0))),
        compiler_params=_cparams("arbitrary"),
        name="post",
    )(x2, yp2, ya2, mod4, wo_p, wo_a, g, wgT, bg, wrT, br)


def _for_each_strip(tile, loc_ref, len_ref, dst_ref, fn):
    def per_expert(e, c):
        idx = tile * N_EXPERTS + e
        run_len = len_ref[idx]
        loc = loc_ref[idx]
        dst = dst_ref[idx]
        for b in reversed(range(STRIP_BITS)):
            n = SUBLANES << b
            done = run_len & ~(2 * n - 1)

            @pl.when((run_len & n) != 0)
            def _():
                fn(pl.multiple_of(loc + done, SUBLANES), pl.multiple_of(dst + done, SUBLANES), n)
        return c

    lax.fori_loop(0, N_EXPERTS, per_expert, 0)


def _dispatch_kernel(loc_ref, len_ref, dst_ref, pends_ref, mi_ref, h_ref, xs_ref, lbuf, zbuf, sems, zsem):
    T = h_ref.shape[0]
    RB = zbuf.shape[0]
    half = h_ref.shape[1] // 2
    step = pl.program_id(0)
    n_steps = pl.num_programs(0)
    slot = step % 2

    @pl.when(step == 0)
    def _():
        zbuf[...] = jnp.zeros(zbuf.shape, zbuf.dtype)

        def block_copy(start):
            return pltpu.make_async_copy(zbuf, xs_ref.at[pl.ds(pl.multiple_of(start, RB), RB)], zsem)

        def nonempty(e):
            prev = jnp.where(e == 0, 0, pends_ref[jnp.maximum(e - 1, 0)])
            return pends_ref[e] > prev

        def fill(e, c):
            @pl.when(nonempty(e))
            def _():
                block_copy(pends_ref[e] - RB).start()
            return c

        def fill_wait(e, c):
            @pl.when(nonempty(e))
            def _():
                block_copy(pends_ref[e] - RB).wait()
            return c

        lax.fori_loop(0, N_EXPERTS, fill, 0)
        n_used = pends_ref[N_EXPERTS - 1] // RB
        n_blocks = xs_ref.shape[0] // RB
        lax.fori_loop(n_used, n_blocks, lambda b, c: (block_copy(b * RB).start(), c)[1], 0)
        lax.fori_loop(0, N_EXPERTS, fill_wait, 0)
        lax.fori_loop(n_used, n_blocks, lambda b, c: (block_copy(b * RB).wait(), c)[1], 0)

    r = lax.broadcasted_iota(I32, (LOCAL_ROWS, T), 0)
    perm = jnp.where((r == mi_ref[2:3, :]) | (r == mi_ref[3:4, :]), 1.0, 0.0).astype(BF16)
    h = h_ref[...]
    lo = jnp.dot(perm, h[:, :half], preferred_element_type=F32)
    hi = jnp.dot(perm, h[:, half:], preferred_element_type=F32)
    lbuf[slot] = _pack_bf16_pair(lo, hi)

    def strip(tile_slot):
        def make(loc, dst, n):
            return pltpu.make_async_copy(lbuf.at[tile_slot, pl.ds(loc, n)], xs_ref.at[pl.ds(dst, n)],
                                         sems.at[tile_slot])
        return make

    _for_each_strip(step, loc_ref, len_ref, dst_ref, lambda *a: strip(slot)(*a).start())

    @pl.when(step > 0)
    def _():
        _for_each_strip(step - 1, loc_ref, len_ref, dst_ref, lambda *a: strip(1 - slot)(*a).wait())

    @pl.when(step == n_steps - 1)
    def _():
        _for_each_strip(step, loc_ref, len_ref, dst_ref, lambda *a: strip(slot)(*a).wait())


def _dispatch_call(loc, run_len, dst, pends, meta_i, h2, P_pad, T, RB):
    N, D = h2.shape
    return pl.pallas_call(
        _dispatch_kernel,
        out_shape=jax.ShapeDtypeStruct((P_pad, D // 2), U32),
        grid_spec=pltpu.PrefetchScalarGridSpec(
            num_scalar_prefetch=4,
            grid=(N // T,),
            in_specs=[pl.BlockSpec((4, T), lambda i, *_: (0, i)),
                      pl.BlockSpec((T, D), lambda i, *_: (i, 0))],
            out_specs=pl.BlockSpec(memory_space=pl.ANY),
            scratch_shapes=[pltpu.VMEM((2, LOCAL_ROWS, D // 2), U32),
                            pltpu.VMEM((RB, D // 2), U32),
                            pltpu.SemaphoreType.DMA((2,)),
                            pltpu.SemaphoreType.DMA]),
        compiler_params=_cparams("arbitrary"),
        name="dispatch",
    )(loc, run_len, dst, pends, meta_i, h2)


def _expert_kernel(be_ref, nused_ref, xs_ref, wgu_ref, wd_ref, ys_ref):
    del be_ref
    used = pl.program_id(0) < nused_ref[0]
    half = wgu_ref.shape[0] // 2

    @pl.when(used)
    def _():
        lo, hi = _unpack_bf16_pair(xs_ref[...])
        gu = (jnp.dot(lo, wgu_ref[0:half, :], preferred_element_type=F32)
              + jnp.dot(hi, wgu_ref[half:, :], preferred_element_type=F32))
        gate = gu[:, :D_EXPERT]
        up = gu[:, D_EXPERT:]
        act = gate / (1.0 + jnp.exp(-gate)) * up
        y = jnp.dot(act.astype(BF16), wd_ref[...], preferred_element_type=F32)
        yb = y.astype(BF16).astype(F32)
        ys_ref[...] = _pack_bf16_pair(yb[:, :half], yb[:, half:])

    @pl.when(jnp.logical_not(used))
    def _():
        ys_ref[...] = jnp.zeros(ys_ref.shape, ys_ref.dtype)


def _expert_call(block_e, n_used, xs, wgu, wd, RB):
    P, Dh = xs.shape
    D = 2 * Dh
    row_map = lambda i, be, nu: (i, 0)
    in_map = lambda i, be, nu: (jnp.minimum(i, nu[0] - 1), 0)
    return pl.pallas_call(
        _expert_kernel,
        out_shape=jax.ShapeDtypeStruct((P, Dh), U32),
        grid_spec=pltpu.PrefetchScalarGridSpec(
            num_scalar_prefetch=2,
            grid=(P // RB,),
            in_specs=[pl.BlockSpec((RB, Dh), in_map),
                      pl.BlockSpec((None, D, 2 * D_EXPERT), lambda i, be, nu: (be[i], 0, 0)),
                      pl.BlockSpec((None, D_EXPERT, D), lambda i, be, nu: (be[i], 0, 0))],
            out_specs=pl.BlockSpec((RB, Dh), row_map)),
        compiler_params=_cparams("arbitrary"),
        name="experts",
    )(block_e, n_used, xs, wgu, wd)


def _combine_kernel(loc_ref, len_ref, dst_ref, x1_ref, rows_ref, w_ref, mod_ref, g_ref, ys_ref, o_ref, ybuf, sems):
    T = x1_ref.shape[0]
    step = pl.program_id(0)
    n_steps = pl.num_programs(0)
    slot = step % 2

    def strip(tile_slot):
        def make(loc, dst, n):
            return pltpu.make_async_copy(ys_ref.at[pl.ds(dst, n)], ybuf.at[tile_slot, pl.ds(loc, n)],
                                         sems.at[tile_slot])
        return make

    @pl.when(step == 0)
    def _():
        ybuf[...] = jnp.zeros(ybuf.shape, ybuf.dtype)
        _for_each_strip(step, loc_ref, len_ref, dst_ref, lambda *a: strip(slot)(*a).start())

    @pl.when(step + 1 < n_steps)
    def _():
        _for_each_strip(step + 1, loc_ref, len_ref, dst_ref, lambda *a: strip(1 - slot)(*a).start())

    _for_each_strip(step, loc_ref, len_ref, dst_ref, lambda *a: strip(slot)(*a).wait())

    c = lax.broadcasted_iota(I32, (T, LOCAL_ROWS), 1)
    rows = rows_ref[...]
    w = w_ref[...]
    permw = jnp.where(c == rows[:, 0:1], w[:, 0:1], jnp.where(c == rows[:, 1:2], w[:, 1:2], 0.0)).astype(BF16)
    lo, hi = _unpack_bf16_pair(ybuf[slot])
    moe = jnp.concatenate([jnp.dot(permw, lo, preferred_element_type=F32),
                           jnp.dot(permw, hi, preferred_element_type=F32)], axis=1)
    gate_f = mod_ref[5]
    o_ref[...] = _rms(x1_ref[...] + gate_f * moe, g_ref[...])


def _combine_call(loc, run_len, dst, x1, rows_tok, w_tok, mod4, final_g, ys, T, tiles_per_batch):
    N, D = x1.shape
    return pl.pallas_call(
        _combine_kernel,
        out_shape=jax.ShapeDtypeStruct((N, D), F32),
        grid_spec=pltpu.PrefetchScalarGridSpec(
            num_scalar_prefetch=3,
            grid=(N // T,),
            in_specs=[pl.BlockSpec((T, D), lambda i, *_: (i, 0)),
                      pl.BlockSpec((T, 2), lambda i, *_: (i, 0)),
                      pl.BlockSpec((T, 2), lambda i, *_: (i, 0)),
                      pl.BlockSpec((None, N_MOD, 1, D), lambda i, *_: (i // tiles_per_batch, 0, 0, 0)),
                      pl.BlockSpec((1, D), lambda i, *_: (0, 0)),
                      pl.BlockSpec(memory_space=pl.ANY)],
            out_specs=pl.BlockSpec((T, D), lambda i, *_: (i, 0)),
            scratch_shapes=[pltpu.VMEM((2, LOCAL_ROWS, D // 2), U32), pltpu.SemaphoreType.DMA((2,))]),
        compiler_params=_cparams("arbitrary"),
        name="combine",
    )(loc, run_len, dst, x1, rows_tok, w_tok, mod4, final_g, ys)


def _swap_halves(w):
    half = w.shape[-1] // 2
    return jnp.concatenate([w[..., half:], w[..., :half]], axis=-1)


def _round_up(v, m):
    return (v + m - 1) // m * m


def kernel(x, c, positions, w_mod, b_mod, norm_mix_g, w_in, w_pool, pool_scale, q_norm_g, w_uq, kv_norm_g, w_ukv, w_o, norm_ffn_g, w_group, b_group, w_router, b_router, w_gate_up, w_down, final_g):
    B, S, D = x.shape
    N = B * S
    depth = w_mod.shape[0]
    T = ROW_TILE
    RB = EXPERT_ROWS
    assert depth == 1, "the final RMSNorm is fused into the layer's combine step"
    assert S % T == 0 and S % ATTN_TQ == 0 and ATTN_TQ % ATTN_TK == 0
    tiles_per_batch = S // T
    n_tiles = N // T
    nH = MLA_HEADS
    l = 0

    inv_freq = ROPE_THETA ** (-(jnp.arange(0, QK_ROPE_DIM, 2, dtype=F32) / QK_ROPE_DIM))
    posr = positions.reshape(B, 1, S)
    cut1 = POOL_WIDTH
    cut2 = cut1 + Q_LORA_RANK
    cut3 = cut2 + KV_LORA_RANK

    mod4 = _mod_call(c, w_mod[l], b_mod[l]).reshape(B, N_MOD, 1, D)

    wi = w_in[l]
    w_in_main = wi[:, :cut3].astype(BF16)
    w_krT = jnp.concatenate([wi[:, cut3:], _swap_halves(wi[:, cut3:])], axis=1).T.astype(BF16)
    wq = w_uq[l].reshape(Q_LORA_RANK, nH, QK_HEAD_DIM)
    wq_n = wq[:, :, :QK_NOPE_DIM].reshape(Q_LORA_RANK, nH * QK_NOPE_DIM)
    wq_r = wq[:, :, QK_NOPE_DIM:]
    w_uqT = jnp.concatenate(
        [wq_n, wq_r.reshape(Q_LORA_RANK, nH * QK_ROPE_DIM),
         _swap_halves(wq_r).reshape(Q_LORA_RANK, nH * QK_ROPE_DIM)], axis=1).T.astype(BF16)
    wkv = w_ukv[l].reshape(KV_LORA_RANK, nH, QK_NOPE_DIM + V_HEAD_DIM)
    w_uk = wkv[:, :, :QK_NOPE_DIM].reshape(KV_LORA_RANK, nH * QK_NOPE_DIM).astype(BF16)
    w_uvT = wkv[:, :, QK_NOPE_DIM:].reshape(KV_LORA_RANK, nH * V_HEAD_DIM).T.astype(BF16)
    qg = (q_norm_g[l] * (QK_HEAD_DIM ** -0.5 * LOG2_E)).reshape(1, Q_LORA_RANK)
    wpool_bd = jnp.zeros((POOL_WIDTH, POOL_WIDTH), F32)
    for g in range(len(POOL_WINDOWS)):
        sl = slice(g * POOL_GROUP_DIM, (g + 1) * POOL_GROUP_DIM)
        wpool_bd = wpool_bd.at[sl, sl].set(w_pool[l, g])
    wpool_bd = wpool_bd.astype(BF16)

    qT, k, vT, yp = _pre_call(
        x, posr, mod4, norm_mix_g[l].reshape(1, D), w_in_main, w_krT, inv_freq.reshape(QK_ROPE_DIM // 2, 1),
        wpool_bd, pool_scale[l].reshape(1, POOL_WIDTH), qg, w_uqT, kv_norm_g[l].reshape(1, KV_LORA_RANK),
        w_uk, w_uvT, T)
    ya = _attn_call(qT, k, vT)

    wo = w_o[l].astype(BF16)
    wgT = jnp.zeros((8, D), F32).at[:N_GROUPS].set(w_group[l].T).astype(BF16)
    bg = jnp.zeros((8, 1), F32).at[:N_GROUPS, 0].set(b_group[l])
    x1, h2, meta_i, meta_w, tile_cnt = _post_call(
        x.reshape(N, D), yp.reshape(N, POOL_WIDTH), ya.reshape(N, nH * V_HEAD_DIM), mod4,
        wo[:POOL_WIDTH], wo[POOL_WIDTH:], norm_ffn_g[l].reshape(1, D),
        wgT, bg, w_router[l].T.astype(BF16), b_router[l].reshape(N_EXPERTS, 1), T, tiles_per_batch)

    run_len = _round_up(tile_cnt[:, :, 0], SUBLANES)
    run_loc = jnp.cumsum(run_len, axis=1) - run_len
    seg_len = _round_up(jnp.sum(run_len, axis=0), RB)
    pends = jnp.cumsum(seg_len).astype(I32)
    run_dst = (pends - seg_len)[None, :] + jnp.cumsum(run_len, axis=0) - run_len
    flat = lambda a: a.reshape(n_tiles * N_EXPERTS).astype(I32)
    run_loc, run_len, run_dst = flat(run_loc), flat(run_len), flat(run_dst)
    P_pad = _round_up(2 * N + n_tiles * N_EXPERTS * (SUBLANES - 1), RB) + N_EXPERTS * RB
    n_rb = P_pad // RB
    n_used = (pends[-1:] // RB).astype(I32)
    block_start = jnp.minimum(jnp.arange(n_rb, dtype=I32), n_used - 1) * RB
    block_e = jnp.sum((pends[None, :] <= block_start[:, None]).astype(I32), axis=1)

    xs = _dispatch_call(run_loc, run_len, run_dst, pends, meta_i, h2, P_pad, T, RB)
    ys = _expert_call(block_e, n_used, xs, w_gate_up[l].astype(BF16), w_down[l].astype(BF16), RB)
    out = _combine_call(run_loc, run_len, run_dst, x1, meta_i[2:4].T, meta_w.T, mod4, final_g.reshape(1, D), ys,
                        T, tiles_per_batch)
    return out.reshape(B, S, D)
```

```python
import jax
import jax.numpy as jnp
from jax import lax
from jax.experimental import pallas as pl
from jax.experimental.pallas import tpu as pltpu

F32 = jnp.float32
BF16 = jnp.bfloat16
U32 = jnp.uint32
I32 = jnp.int32

POOL_WINDOWS = (2, 4, 8, 16)
POOL_GROUP_DIM = 64
POOL_WIDTH = 256
MLA_HEADS = 6
QK_NOPE_DIM = 128
QK_ROPE_DIM = 64
QK_HEAD_DIM = QK_NOPE_DIM + QK_ROPE_DIM
V_HEAD_DIM = 128
Q_LORA_RANK = 512
KV_LORA_RANK = 256
ROPE_THETA = 10000.0
N_GROUPS = 4
EXPERTS_PER_GROUP = 8
N_EXPERTS = N_GROUPS * EXPERTS_PER_GROUP
D_EXPERT = 256
N_MOD = 6
EPS = 1e-6

SUBLANES = 8
POOL_HALO = 32
ROW_TILE = 512
ATTN_TQ = 512
EXPERT_ROWS = 512
LOCAL_ROWS = 2 * ROW_TILE + 256
assert LOCAL_ROWS >= 2 * ROW_TILE + N_EXPERTS * (SUBLANES - 1)
STRIP_BITS = (2 * ROW_TILE // SUBLANES).bit_length()
VMEM_LIMIT = 56 * 1024 * 1024
NEG_BIG = -1e30
LOG2_E = 1.4426950408889634
HI_MASK = 0xFFFF0000


def _cparams(*sem):
    return pltpu.CompilerParams(dimension_semantics=sem, vmem_limit_bytes=VMEM_LIMIT)


def _rms(x, g):
    return x * lax.rsqrt(jnp.mean(x * x, axis=-1, keepdims=True) + EPS) * g


def _pack_bf16_pair(lo, hi):
    return lax.bitcast_convert_type(hi, U32) | (lax.bitcast_convert_type(lo, U32) >> 16)


def _unpack_bf16_pair(w):
    lo = lax.bitcast_convert_type(w << 16, F32).astype(BF16)
    hi = lax.bitcast_convert_type(w & jnp.uint32(HI_MASK), F32).astype(BF16)
    return lo, hi


def _mod_kernel(c_ref, w_ref, b_ref, o_ref):
    c = c_ref[...]
    ca = c / (1.0 + jnp.exp(-c))
    o_ref[...] = jnp.dot(ca, w_ref[...], precision=lax.Precision.HIGHEST,
                         preferred_element_type=F32) + b_ref[...]


def _mod_call(c, w_mod, b_mod):
    B, D = c.shape
    n_out = w_mod.shape[1]
    tn = 512
    return pl.pallas_call(
        _mod_kernel,
        out_shape=jax.ShapeDtypeStruct((B, n_out), F32),
        grid=(n_out // tn,),
        in_specs=[pl.BlockSpec((B, D), lambda j: (0, 0)),
                  pl.BlockSpec((D, tn), lambda j: (0, j)),
                  pl.BlockSpec((1, tn), lambda j: (0, j))],
        out_specs=pl.BlockSpec((B, tn), lambda j: (0, j)),
        compiler_params=_cparams("arbitrary"),
        name="mod",
    )(c, w_mod, b_mod.reshape(1, n_out))


def _pre_kernel(x_ref, posr_ref, mod_ref, g_ref, win_ref, wkrT_ref, invfc_ref, wpool_ref, pscale_ref,
                qg_ref, wuqT_ref, kvg_ref, wuk_ref, wuvT_ref,
                qT_ref, k_ref, vT_ref, yp_ref,
                pbuf, b2, b4, b8):
    T = x_ref.shape[0]
    H = POOL_HALO
    i = pl.program_id(1)
    nt = (((1,), (1,)), ((), ()))

    shift = mod_ref[0]
    scale = mod_ref[1]
    hb = (_rms(x_ref[...], g_ref[...]) * (1.0 + scale) + shift).astype(BF16)
    u = jnp.dot(hb, win_ref[...], preferred_element_type=F32)
    krT = lax.dot_general(wkrT_ref[...], hb, nt, preferred_element_type=F32)

    p = u[:, :POOL_WIDTH]

    @pl.when(i == 0)
    def _():
        pbuf[0:H, :] = jnp.zeros((H, POOL_WIDTH), F32)

    pbuf[H:H + T, :] = p
    b2[8:T + H, :] = pbuf[8:T + H, :] + pbuf[7:T + H - 1, :]
    b4[16:T + H, :] = b2[16:T + H, :] + b2[14:T + H - 2, :]
    b8[24:T + H, :] = b4[24:T + H, :] + b4[20:T + H - 4, :]
    s2 = b2[H:T + H, :]
    s4 = b4[H:T + H, :]
    s8 = b8[H:T + H, :]
    s16 = b8[H:T + H, :] + b8[H - 8:T + H - 8, :]
    pbuf[0:H, :] = pbuf[T:T + H, :]

    lane = lax.broadcasted_iota(I32, (T, POOL_WIDTH), 1)
    t1 = (lax.broadcasted_iota(I32, (T, 1), 0) + (i * T + 1)).astype(F32)
    inv2 = 1.0 / jnp.minimum(t1, 2.0)
    inv4 = 1.0 / jnp.minimum(t1, 4.0)
    inv8 = 1.0 / jnp.minimum(t1, 8.0)
    inv16 = 1.0 / jnp.minimum(t1, 16.0)
    mean = jnp.where(lane < 64, s2 * inv2,
                     jnp.where(lane < 128, s4 * inv4,
                               jnp.where(lane < 192, s8 * inv8, s16 * inv16)))
    pooled = mean - p
    yp = jnp.dot(pooled.astype(BF16), wpool_ref[...], preferred_element_type=F32) * pscale_ref[...]
    yp_ref[...] = yp.astype(yp_ref.dtype)

    ang = invfc_ref[...] * posr_ref[...].astype(F32)
    cos_h = jnp.cos(ang)
    sin_h = jnp.sin(ang)
    cos_t = jnp.concatenate([cos_h, cos_h], axis=0)
    sin_t = jnp.concatenate([-sin_h, sin_h], axis=0)

    k_rope = (krT[0:QK_ROPE_DIM] * cos_t + krT[QK_ROPE_DIM:2 * QK_ROPE_DIM] * sin_t).T.astype(BF16)

    cq = u[:, POOL_WIDTH:POOL_WIDTH + Q_LORA_RANK]
    ckv = u[:, POOL_WIDTH + Q_LORA_RANK:POOL_WIDTH + Q_LORA_RANK + KV_LORA_RANK]
    cqn = _rms(cq, qg_ref[...]).astype(BF16)
    ckvn = _rms(ckv, kvg_ref[...]).astype(BF16)
    qaT = lax.dot_general(wuqT_ref[...], cqn, nt, preferred_element_type=F32)
    kn = jnp.dot(ckvn, wuk_ref[...], preferred_element_type=F32)
    vT = lax.dot_general(wuvT_ref[...], ckvn, nt, preferred_element_type=F32)
    nq = MLA_HEADS * QK_NOPE_DIM
    nr = MLA_HEADS * QK_ROPE_DIM
    for hd in range(MLA_HEADS):
        qr = qaT[nq + hd * QK_ROPE_DIM:nq + (hd + 1) * QK_ROPE_DIM, :]
        qr_sw = qaT[nq + nr + hd * QK_ROPE_DIM:nq + nr + (hd + 1) * QK_ROPE_DIM, :]
        qT_ref[hd, 0:QK_NOPE_DIM, :] = qaT[hd * QK_NOPE_DIM:(hd + 1) * QK_NOPE_DIM, :].astype(BF16)
        qT_ref[hd, QK_NOPE_DIM:QK_HEAD_DIM, :] = (qr * cos_t + qr_sw * sin_t).astype(BF16)
        k_ref[hd, :, 0:QK_NOPE_DIM] = kn[:, hd * QK_NOPE_DIM:(hd + 1) * QK_NOPE_DIM].astype(BF16)
        k_ref[hd, :, QK_NOPE_DIM:QK_HEAD_DIM] = k_rope
        vT_ref[hd, :, :] = vT[hd * V_HEAD_DIM:(hd + 1) * V_HEAD_DIM, :].astype(BF16)


def _pre_call(x, posr, mod4, norm_g, w_in_main, w_krT, inv_freq_col, wpool_bd, pool_scale, qg, w_uqT, kvg,
              w_uk, w_uvT, T):
    B, S, D = x.shape
    nH = MLA_HEADS
    const = lambda shape: pl.BlockSpec(shape, lambda b, i: (0,) * len(shape))
    return pl.pallas_call(
        _pre_kernel,
        out_shape=(jax.ShapeDtypeStruct((B, nH, QK_HEAD_DIM, S), BF16),
                   jax.ShapeDtypeStruct((B, nH, S, QK_HEAD_DIM), BF16),
                   jax.ShapeDtypeStruct((B, nH, V_HEAD_DIM, S), BF16),
                   jax.ShapeDtypeStruct((B, S, POOL_WIDTH), BF16)),
        grid=(B, S // T),
        in_specs=[pl.BlockSpec((None, T, D), lambda b, i: (b, i, 0)),
                  pl.BlockSpec((None, 1, T), lambda b, i: (b, 0, i)),
                  pl.BlockSpec((None, N_MOD, 1, D), lambda b, i: (b, 0, 0, 0)),
                  const((1, D)),
                  const(w_in_main.shape),
                  const(w_krT.shape),
                  const(inv_freq_col.shape),
                  const(wpool_bd.shape),
                  const((1, POOL_WIDTH)),
                  const((1, Q_LORA_RANK)),
                  const(w_uqT.shape),
                  const((1, KV_LORA_RANK)),
                  const(w_uk.shape),
                  const(w_uvT.shape)],
        out_specs=(pl.BlockSpec((None, nH, QK_HEAD_DIM, T), lambda b, i: (b, 0, 0, i)),
                   pl.BlockSpec((None, nH, T, QK_HEAD_DIM), lambda b, i: (b, 0, i, 0)),
                   pl.BlockSpec((None, nH, V_HEAD_DIM, T), lambda b, i: (b, 0, 0, i)),
                   pl.BlockSpec((None, T, POOL_WIDTH), lambda b, i: (b, i, 0))),
        scratch_shapes=[pltpu.VMEM((T + POOL_HALO, POOL_WIDTH), F32)] * 4,
        compiler_params=_cparams("arbitrary", "arbitrary"),
        name="pre",
    )(x, posr, mod4, norm_g, w_in_main, w_krT, inv_freq_col, wpool_bd, pool_scale, qg, w_uqT, kvg, w_uk, w_uvT)


def _attn_kernel(qT_ref, k_ref, vT_ref, o_ref):
    S = k_ref.shape[0]
    t = ATTN_TQ
    diag = lax.broadcasted_iota(I32, (t, t), 0) <= lax.broadcasted_iota(I32, (t, t), 1)
    for i in range(S // t):
        qT = qT_ref[:, i * t:(i + 1) * t]
        m = jnp.full((1, t), NEG_BIG, F32)
        l = jnp.zeros((1, t), F32)
        acc = jnp.zeros((V_HEAD_DIM, t), F32)
        for j in range(i + 1):
            k = k_ref[j * t:(j + 1) * t, :]
            vT = vT_ref[:, j * t:(j + 1) * t]
            sT = jnp.dot(k, qT, preferred_element_type=F32)
            if j == i:
                sT = jnp.where(diag, sT, NEG_BIG)
            m_new = jnp.maximum(m, jnp.max(sT, axis=0, keepdims=True))
            alpha = jnp.exp2(m - m_new)
            pT = jnp.exp2(sT - m_new)
            l = alpha * l + jnp.sum(pT, axis=0, keepdims=True)
            acc = alpha * acc + jnp.dot(vT, pT.astype(BF16), preferred_element_type=F32)
            m = m_new
        o_ref[i * t:(i + 1) * t, :] = (acc * (1.0 / l)).T.astype(o_ref.dtype)


def _attn_call(qT, k, vT):
    B, nH, S, _ = k.shape
    return pl.pallas_call(
        _attn_kernel,
        out_shape=jax.ShapeDtypeStruct((B, S, nH * V_HEAD_DIM), BF16),
        grid=(B, nH),
        in_specs=[pl.BlockSpec((None, None, QK_HEAD_DIM, S), lambda b, h: (b, h, 0, 0)),
                  pl.BlockSpec((None, None, S, QK_HEAD_DIM), lambda b, h: (b, h, 0, 0)),
                  pl.BlockSpec((None, None, V_HEAD_DIM, S), lambda b, h: (b, h, 0, 0))],
        out_specs=pl.BlockSpec((None, S, V_HEAD_DIM), lambda b, h: (b, 0, h)),
        compiler_params=_cparams("arbitrary", "arbitrary"),
        name="attn",
    )(qT, k, vT)


def _post_kernel(x_ref, yp_ref, ya_ref, mod_ref, wo_p_ref, wo_a_ref, g_ref,
                 wgT_ref, bg_ref, wrT_ref, br_ref,
                 x1_ref, h2_ref, mi_ref, mw_ref, cnt_ref):
    T = x_ref.shape[0]

    gate_a = mod_ref[2]
    shift_f = mod_ref[3]
    scale_f = mod_ref[4]
    mix = (jnp.dot(yp_ref[...], wo_p_ref[...], preferred_element_type=F32)
           + jnp.dot(ya_ref[...], wo_a_ref[...], preferred_element_type=F32))
    x1 = x_ref[...] + gate_a * mix
    x1_ref[...] = x1
    hb = (_rms(x1, g_ref[...]) * (1.0 + scale_f) + shift_f).astype(BF16)
    h2_ref[...] = hb

    nt = (((1,), (1,)), ((), ()))
    gl = lax.dot_general(wgT_ref[...], hb, nt, preferred_element_type=F32)
    el = lax.dot_general(wrT_ref[...], hb, nt, preferred_element_type=F32)

    r8 = lax.broadcasted_iota(I32, (8, T), 0)
    gvalid = r8 < N_GROUPS
    gmax = jnp.max(jnp.where(gvalid, gl, NEG_BIG), axis=0, keepdims=True)
    gexp = jnp.where(gvalid, jnp.exp(gl - gmax), 0.0)
    g_prob = gexp / jnp.sum(gexp, axis=0, keepdims=True)
    gb = jnp.where(gvalid, gl + bg_ref[...], NEG_BIG)
    gbmax = jnp.max(gb, axis=0, keepdims=True)
    g_sel = jnp.min(jnp.where(gb == gbmax, r8, 8), axis=0, keepdims=True)
    gp = jnp.sum(jnp.where(r8 == g_sel, g_prob, 0.0), axis=0, keepdims=True)

    e_in = jnp.zeros((EXPERTS_PER_GROUP, T), F32)
    b_in = jnp.zeros((EXPERTS_PER_GROUP, T), F32)
    br = br_ref[...]
    for g in range(N_GROUPS):
        sel = g_sel == g
        e_in = jnp.where(sel, el[g * 8:(g + 1) * 8, :], e_in)
        b_in = jnp.where(sel, br[g * 8:(g + 1) * 8, :], b_in)
    eb = e_in + b_in
    m1 = jnp.max(eb, axis=0, keepdims=True)
    i1 = jnp.min(jnp.where(eb == m1, r8, 8), axis=0, keepdims=True)
    eb2 = jnp.where(r8 == i1, NEG_BIG, eb)
    m2 = jnp.max(eb2, axis=0, keepdims=True)
    i2 = jnp.min(jnp.where(eb2 == m2, r8, 8), axis=0, keepdims=True)
    emax = jnp.max(e_in, axis=0, keepdims=True)
    eexp = jnp.exp(e_in - emax)
    sp = eexp / jnp.sum(eexp, axis=0, keepdims=True)
    p1 = jnp.sum(jnp.where(r8 == i1, sp, 0.0), axis=0, keepdims=True)
    p2 = jnp.sum(jnp.where(r8 == i2, sp, 0.0), axis=0, keepdims=True)
    tot = p1 + p2
    w1 = gp * (p1 / tot)
    w2 = gp * (p2 / tot)
    e1 = g_sel * EXPERTS_PER_GROUP + i1
    e2 = g_sel * EXPERTS_PER_GROUP + i2

    r32 = lax.broadcasted_iota(I32, (N_EXPERTS, T), 0)
    oh1 = r32 == e1
    oh2 = r32 == e2
    oh = jnp.where(oh1 | oh2, 1.0, 0.0)
    upper = jnp.where(lax.broadcasted_iota(I32, (T, T), 0) < lax.broadcasted_iota(I32, (T, T), 1),
                      1.0, 0.0).astype(BF16)
    before = jnp.dot(oh.astype(BF16), upper, preferred_element_type=F32)
    cnt = jnp.sum(oh, axis=1, keepdims=True)
    run8 = jnp.floor((cnt + (SUBLANES - 1.0)) * (1.0 / SUBLANES))
    lower = jnp.where(lax.broadcasted_iota(I32, (N_EXPERTS, N_EXPERTS), 1)
                      < lax.broadcasted_iota(I32, (N_EXPERTS, N_EXPERTS), 0), 1.0, 0.0).astype(BF16)
    run_start = jnp.dot(lower, jnp.broadcast_to(run8, (N_EXPERTS, 128)).astype(BF16),
                        preferred_element_type=F32)[:, 0:1] * float(SUBLANES)
    pos = before + run_start
    row1 = jnp.sum(jnp.where(oh1, pos, 0.0), axis=0, keepdims=True)
    row2 = jnp.sum(jnp.where(oh2, pos, 0.0), axis=0, keepdims=True)
    cnt_ref[...] = cnt.astype(I32)

    mi_ref[0:1, :] = e1
    mi_ref[1:2, :] = e2
    mi_ref[2:3, :] = row1.astype(I32)
    mi_ref[3:4, :] = row2.astype(I32)
    mw_ref[0:1, :] = w1
    mw_ref[1:2, :] = w2


def _post_call(x2, yp2, ya2, mod4, wo_p, wo_a, g, wgT, bg, wrT, br, T, tiles_per_batch):
    N, D = x2.shape
    const = lambda shape: pl.BlockSpec(shape, lambda i: (0,) * len(shape))
    return pl.pallas_call(
        _post_kernel,
        out_shape=(jax.ShapeDtypeStruct((N, D), F32),
                   jax.ShapeDtypeStruct((N, D), BF16),
                   jax.ShapeDtypeStruct((4, N), I32),
                   jax.ShapeDtypeStruct((2, N), F32),
                   jax.ShapeDtypeStruct((N // T, N_EXPERTS, 1), I32)),
        grid=(N // T,),
        in_specs=[pl.BlockSpec((T, D), lambda i: (i, 0)),
                  pl.BlockSpec((T, POOL_WIDTH), lambda i: (i, 0)),
                  pl.BlockSpec((T, MLA_HEADS * V_HEAD_DIM), lambda i: (i, 0)),
                  pl.BlockSpec((None, N_MOD, 1, D), lambda i: (i // tiles_per_batch, 0, 0, 0)),
                  const(wo_p.shape), const(wo_a.shape), const((1, D)),
                  const(wgT.shape), const(bg.shape), const(wrT.shape), const(br.shape)],
        out_specs=(pl.BlockSpec((T, D), lambda i: (i, 0)),
                   pl.BlockSpec((T, D), lambda i: (i, 0)),
                   pl.BlockSpec((4, T), lambda i: (0, i)),
                   pl.BlockSpec((2, T), lambda i: (0, i)),
                   pl.BlockSpec((None, N_EXPERTS, 1), lambda i: (i, 0, 0))),
        compiler_params=_cparams("arbitrary"),
        name="post",
    )(x2, yp2, ya2, mod4, wo_p, wo_a, g, wgT, bg, wrT, br)


def _for_each_strip(tile, loc_ref, len_ref, dst_ref, fn):
    def per_expert(e, c):
        idx = tile * N_EXPERTS + e
        run_len = len_ref[idx]
        loc = loc_ref[idx]
        dst = dst_ref[idx]
        for b in reversed(range(STRIP_BITS)):
            n = SUBLANES << b
            done = run_len & ~(2 * n - 1)

            @pl.when((run_len & n) != 0)
            def _():
                fn(pl.multiple_of(loc + done, SUBLANES), pl.multiple_of(dst + done, SUBLANES), n)
        return c

    lax.fori_loop(0, N_EXPERTS, per_expert, 0)


def _dispatch_kernel(loc_ref, len_ref, dst_ref, pends_ref, mi_ref, h_ref, xs_ref, lbuf, zbuf, sems, zsem):
    T = h_ref.shape[0]
    RB = zbuf.shape[0]
    half = h_ref.shape[1] // 2
    step = pl.program_id(0)
    n_steps = pl.num_programs(0)
    slot = step % 2

    @pl.when(step == 0)
    def _():
        zbuf[...] = jnp.zeros(zbuf.shape, zbuf.dtype)

        def block_copy(start):
            return pltpu.make_async_copy(zbuf, xs_ref.at[pl.ds(pl.multiple_of(start, RB), RB)], zsem)

        def nonempty(e):
            prev = jnp.where(e == 0, 0, pends_ref[jnp.maximum(e - 1, 0)])
            return pends_ref[e] > prev

        def fill(e, c):
            @pl.when(nonempty(e))
            def _():
                block_copy(pends_ref[e] - RB).start()
            return c

        def fill_wait(e, c):
            @pl.when(nonempty(e))
            def _():
                block_copy(pends_ref[e] - RB).wait()
            return c

        lax.fori_loop(0, N_EXPERTS, fill, 0)
        n_used = pends_ref[N_EXPERTS - 1] // RB
        n_blocks = xs_ref.shape[0] // RB
        lax.fori_loop(n_used, n_blocks, lambda b, c: (block_copy(b * RB).start(), c)[1], 0)
        lax.fori_loop(0, N_EXPERTS, fill_wait, 0)
        lax.fori_loop(n_used, n_blocks, lambda b, c: (block_copy(b * RB).wait(), c)[1], 0)

    r = lax.broadcasted_iota(I32, (LOCAL_ROWS, T), 0)
    perm = jnp.where((r == mi_ref[2:3, :]) | (r == mi_ref[3:4, :]), 1.0, 0.0).astype(BF16)
    h = h_ref[...]
    lo = jnp.dot(perm, h[:, :half], preferred_element_type=F32)
    hi = jnp.dot(perm, h[:, half:], preferred_element_type=F32)
    lbuf[slot] = _pack_bf16_pair(lo, hi)

    def strip(tile_slot):
        def make(loc, dst, n):
            return pltpu.make_async_copy(lbuf.at[tile_slot, pl.ds(loc, n)], xs_ref.at[pl.ds(dst, n)],
                                         sems.at[tile_slot])
        return make

    _for_each_strip(step, loc_ref, len_ref, dst_ref, lambda *a: strip(slot)(*a).start())

    @pl.when(step > 0)
    def _():
        _for_each_strip(step - 1, loc_ref, len_ref, dst_ref, lambda *a: strip(1 - slot)(*a).wait())

    @pl.when(step == n_steps - 1)
    def _():
        _for_each_strip(step, loc_ref, len_ref, dst_ref, lambda *a: strip(slot)(*a).wait())


def _dispatch_call(loc, run_len, dst, pends, meta_i, h2, P_pad, T, RB):
    N, D = h2.shape
    return pl.pallas_call(
        _dispatch_kernel,
        out_shape=jax.ShapeDtypeStruct((P_pad, D // 2), U32),
        grid_spec=pltpu.PrefetchScalarGridSpec(
            num_scalar_prefetch=4,
            grid=(N // T,),
            in_specs=[pl.BlockSpec((4, T), lambda i, *_: (0, i)),
                      pl.BlockSpec((T, D), lambda i, *_: (i, 0))],
            out_specs=pl.BlockSpec(memory_space=pl.ANY),
            scratch_shapes=[pltpu.VMEM((2, LOCAL_ROWS, D // 2), U32),
                            pltpu.VMEM((RB, D // 2), U32),
                            pltpu.SemaphoreType.DMA((2,)),
                            pltpu.SemaphoreType.DMA]),
        compiler_params=_cparams("arbitrary"),
        name="dispatch",
    )(loc, run_len, dst, pends, meta_i, h2)


def _expert_kernel(be_ref, nused_ref, xs_ref, wgu_ref, wd_ref, ys_ref, wgu_bf, wd_bf):
    i = pl.program_id(0)
    used = i < nused_ref[0]
    half = wgu_ref.shape[0] // 2

    @pl.when(used & ((i == 0) | (be_ref[i] != be_ref[jnp.maximum(i - 1, 0)])))
    def _():
        wgu_bf[...] = wgu_ref[...].astype(BF16)
        wd_bf[...] = wd_ref[...].astype(BF16)

    @pl.when(used)
    def _():
        lo, hi = _unpack_bf16_pair(xs_ref[...])
        gu = (jnp.dot(lo, wgu_bf[0:half, :], preferred_element_type=F32)
              + jnp.dot(hi, wgu_bf[half:, :], preferred_element_type=F32))
        gate = gu[:, :D_EXPERT]
        up = gu[:, D_EXPERT:]
        act = gate / (1.0 + jnp.exp(-gate)) * up
        y = jnp.dot(act.astype(BF16), wd_bf[...], preferred_element_type=F32)
        yb = y.astype(BF16).astype(F32)
        ys_ref[...] = _pack_bf16_pair(yb[:, :half], yb[:, half:])

    @pl.when(jnp.logical_not(used))
    def _():
        ys_ref[...] = jnp.zeros(ys_ref.shape, ys_ref.dtype)


def _expert_call(block_e, n_used, xs, wgu, wd, RB):
    P, Dh = xs.shape
    D = 2 * Dh
    row_map = lambda i, be, nu: (i, 0)
    in_map = lambda i, be, nu: (jnp.minimum(i, nu[0] - 1), 0)
    return pl.pallas_call(
        _expert_kernel,
        out_shape=jax.ShapeDtypeStruct((P, Dh), U32),
        grid_spec=pltpu.PrefetchScalarGridSpec(
            num_scalar_prefetch=2,
            grid=(P // RB,),
            in_specs=[pl.BlockSpec((RB, Dh), in_map),
                      pl.BlockSpec((None, D, 2 * D_EXPERT), lambda i, be, nu: (be[i], 0, 0)),
                      pl.BlockSpec((None, D_EXPERT, D), lambda i, be, nu: (be[i], 0, 0))],
            out_specs=pl.BlockSpec((RB, Dh), row_map),
            scratch_shapes=[pltpu.VMEM((D, 2 * D_EXPERT), BF16), pltpu.VMEM((D_EXPERT, D), BF16)]),
        compiler_params=_cparams("arbitrary"),
        name="experts",
    )(block_e, n_used, xs, wgu, wd)


def _combine_kernel(loc_ref, len_ref, dst_ref, x1_ref, rows_ref, w_ref, mod_ref, g_ref, ys_ref, o_ref, ybuf, sems):
    T = x1_ref.shape[0]
    step = pl.program_id(0)
    n_steps = pl.num_programs(0)
    slot = step % 2

    def strip(tile_slot):
        def make(loc, dst, n):
            return pltpu.make_async_copy(ys_ref.at[pl.ds(dst, n)], ybuf.at[tile_slot, pl.ds(loc, n)],
                                         sems.at[tile_slot])
        return make

    @pl.when(step == 0)
    def _():
        ybuf[...] = jnp.zeros(ybuf.shape, ybuf.dtype)
        _for_each_strip(step, loc_ref, len_ref, dst_ref, lambda *a: strip(slot)(*a).start())

    @pl.when(step + 1 < n_steps)
    def _():
        _for_each_strip(step + 1, loc_ref, len_ref, dst_ref, lambda *a: strip(1 - slot)(*a).start())

    _for_each_strip(step, loc_ref, len_ref, dst_ref, lambda *a: strip(slot)(*a).wait())

    c = lax.broadcasted_iota(I32, (T, LOCAL_ROWS), 1)
    rows = rows_ref[...]
    w = w_ref[...]
    permw = jnp.where(c == rows[:, 0:1], w[:, 0:1], jnp.where(c == rows[:, 1:2], w[:, 1:2], 0.0)).astype(BF16)
    lo, hi = _unpack_bf16_pair(ybuf[slot])
    moe = jnp.concatenate([jnp.dot(permw, lo, preferred_element_type=F32),
                           jnp.dot(permw, hi, preferred_element_type=F32)], axis=1)
    gate_f = mod_ref[5]
    o_ref[...] = _rms(x1_ref[...] + gate_f * moe, g_ref[...])


def _combine_call(loc, run_len, dst, x1, rows_tok, w_tok, mod4, final_g, ys, T, tiles_per_batch):
    N, D = x1.shape
    return pl.pallas_call(
        _combine_kernel,
        out_shape=jax.ShapeDtypeStruct((N, D), F32),
        grid_spec=pltpu.PrefetchScalarGridSpec(
            num_scalar_prefetch=3,
            grid=(N // T,),
            in_specs=[pl.BlockSpec((T, D), lambda i, *_: (i, 0)),
                      pl.BlockSpec((T, 2), lambda i, *_: (i, 0)),
                      pl.BlockSpec((T, 2), lambda i, *_: (i, 0)),
                      pl.BlockSpec((None, N_MOD, 1, D), lambda i, *_: (i // tiles_per_batch, 0, 0, 0)),
                      pl.BlockSpec((1, D), lambda i, *_: (0, 0)),
                      pl.BlockSpec(memory_space=pl.ANY)],
            out_specs=pl.BlockSpec((T, D), lambda i, *_: (i, 0)),
            scratch_shapes=[pltpu.VMEM((2, LOCAL_ROWS, D // 2), U32), pltpu.SemaphoreType.DMA((2,))]),
        compiler_params=_cparams("arbitrary"),
        name="combine",
    )(loc, run_len, dst, x1, rows_tok, w_tok, mod4, final_g, ys)


def _swap_halves(w):
    half = w.shape[-1] // 2
    return jnp.concatenate([w[..., half:], w[..., :half]], axis=-1)


def _round_up(v, m):
    return (v + m - 1) // m * m


def kernel(x, c, positions, w_mod, b_mod, norm_mix_g, w_in, w_pool, pool_scale, q_norm_g, w_uq, kv_norm_g, w_ukv, w_o, norm_ffn_g, w_group, b_group, w_router, b_router, w_gate_up, w_down, final_g):
    B, S, D = x.shape
    N = B * S
    depth = w_mod.shape[0]
    T = ROW_TILE
    RB = EXPERT_ROWS
    assert depth == 1, "the final RMSNorm is fused into the layer's combine step"
    assert S % T == 0 and S % ATTN_TQ == 0
    tiles_per_batch = S // T
    n_tiles = N // T
    nH = MLA_HEADS
    l = 0

    inv_freq = ROPE_THETA ** (-(jnp.arange(0, QK_ROPE_DIM, 2, dtype=F32) / QK_ROPE_DIM))
    posr = positions.reshape(B, 1, S)
    cut1 = POOL_WIDTH
    cut2 = cut1 + Q_LORA_RANK
    cut3 = cut2 + KV_LORA_RANK

    mod4 = _mod_call(c, w_mod[l], b_mod[l]).reshape(B, N_MOD, 1, D)

    wi = w_in[l]
    w_in_main = wi[:, :cut3].astype(BF16)
    w_krT = jnp.concatenate([wi[:, cut3:], _swap_halves(wi[:, cut3:])], axis=1).T.astype(BF16)
    wq = w_uq[l].reshape(Q_LORA_RANK, nH, QK_HEAD_DIM)
    wq_n = wq[:, :, :QK_NOPE_DIM].reshape(Q_LORA_RANK, nH * QK_NOPE_DIM)
    wq_r = wq[:, :, QK_NOPE_DIM:]
    w_uqT = jnp.concatenate(
        [wq_n, wq_r.reshape(Q_LORA_RANK, nH * QK_ROPE_DIM),
         _swap_halves(wq_r).reshape(Q_LORA_RANK, nH * QK_ROPE_DIM)], axis=1).T.astype(BF16)
    wkv = w_ukv[l].reshape(KV_LORA_RANK, nH, QK_NOPE_DIM + V_HEAD_DIM)
    w_uk = wkv[:, :, :QK_NOPE_DIM].reshape(KV_LORA_RANK, nH * QK_NOPE_DIM).astype(BF16)
    w_uvT = wkv[:, :, QK_NOPE_DIM:].reshape(KV_LORA_RANK, nH * V_HEAD_DIM).T.astype(BF16)
    qg = (q_norm_g[l] * (QK_HEAD_DIM ** -0.5 * LOG2_E)).reshape(1, Q_LORA_RANK)
    wpool_bd = jnp.zeros((POOL_WIDTH, POOL_WIDTH), F32)
    for g in range(len(POOL_WINDOWS)):
        sl = slice(g * POOL_GROUP_DIM, (g + 1) * POOL_GROUP_DIM)
        wpool_bd = wpool_bd.at[sl, sl].set(w_pool[l, g])
    wpool_bd = wpool_bd.astype(BF16)

    qT, k, vT, yp = _pre_call(
        x, posr, mod4, norm_mix_g[l].reshape(1, D), w_in_main, w_krT, inv_freq.reshape(QK_ROPE_DIM // 2, 1),
        wpool_bd, pool_scale[l].reshape(1, POOL_WIDTH), qg, w_uqT, kv_norm_g[l].reshape(1, KV_LORA_RANK),
        w_uk, w_uvT, T)
    ya = _attn_call(qT, k, vT)

    wo = w_o[l].astype(BF16)
    wgT = jnp.zeros((8, D), F32).at[:N_GROUPS].set(w_group[l].T).astype(BF16)
    bg = jnp.zeros((8, 1), F32).at[:N_GROUPS, 0].set(b_group[l])
    x1, h2, meta_i, meta_w, tile_cnt = _post_call(
        x.reshape(N, D), yp.reshape(N, POOL_WIDTH), ya.reshape(N, nH * V_HEAD_DIM), mod4,
        wo[:POOL_WIDTH], wo[POOL_WIDTH:], norm_ffn_g[l].reshape(1, D),
        wgT, bg, w_router[l].T.astype(BF16), b_router[l].reshape(N_EXPERTS, 1), T, tiles_per_batch)

    run_len = _round_up(tile_cnt[:, :, 0], SUBLANES)
    run_loc = jnp.cumsum(run_len, axis=1) - run_len
    seg_len = _round_up(jnp.sum(run_len, axis=0), RB)
    pends = jnp.cumsum(seg_len).astype(I32)
    run_dst = (pends - seg_len)[None, :] + jnp.cumsum(run_len, axis=0) - run_len
    flat = lambda a: a.reshape(n_tiles * N_EXPERTS).astype(I32)
    run_loc, run_len, run_dst = flat(run_loc), flat(run_len), flat(run_dst)
    P_pad = _round_up(2 * N + n_tiles * N_EXPERTS * (SUBLANES - 1), RB) + N_EXPERTS * RB
    n_rb = P_pad // RB
    n_used = (pends[-1:] // RB).astype(I32)
    block_start = jnp.minimum(jnp.arange(n_rb, dtype=I32), n_used - 1) * RB
    block_e = jnp.sum((pends[None, :] <= block_start[:, None]).astype(I32), axis=1)

    xs = _dispatch_call(run_loc, run_len, run_dst, pends, meta_i, h2, P_pad, T, RB)
    ys = _expert_call(block_e, n_used, xs, w_gate_up[l], w_down[l], RB)
    out = _combine_call(run_loc, run_len, run_dst, x1, meta_i[2:4].T, meta_w.T, mod4, final_g.reshape(1, D), ys,
                        T, tiles_per_batch)
    return out.reshape(B, S, D)
```

```python
import jax
import jax.numpy as jnp
from jax import lax
from jax.experimental import pallas as pl
from jax.experimental.pallas import tpu as pltpu

F32 = jnp.float32
BF16 = jnp.bfloat16
U32 = jnp.uint32
I32 = jnp.int32

POOL_WINDOWS = (2, 4, 8, 16)
POOL_GROUP_DIM = 64
POOL_WIDTH = 256
MLA_HEADS = 6
QK_NOPE_DIM = 128
QK_ROPE_DIM = 64
QK_HEAD_DIM = QK_NOPE_DIM + QK_ROPE_DIM
V_HEAD_DIM = 128
Q_LORA_RANK = 512
KV_LORA_RANK = 256
ROPE_THETA = 10000.0
N_GROUPS = 4
EXPERTS_PER_GROUP = 8
N_EXPERTS = N_GROUPS * EXPERTS_PER_GROUP
D_EXPERT = 256
N_MOD = 6
EPS = 1e-6

SUBLANES = 8
POOL_HALO = 32
ROW_TILE = 512
ATTN_TQ = 2048
ATTN_TK = 512
EXPERT_ROWS = 512
LOCAL_ROWS = 2 * ROW_TILE + 256
assert LOCAL_ROWS >= 2 * ROW_TILE + N_EXPERTS * (SUBLANES - 1)
STRIP_BITS = (2 * ROW_TILE // SUBLANES).bit_length()
MAX_STRIPS = 128
assert MAX_STRIPS >= 2 * N_EXPERTS + (LOCAL_ROWS // SUBLANES - 3 * N_EXPERTS) // 4
DISPATCH_SLOTS = 3
VMEM_LIMIT = 56 * 1024 * 1024
NEG_BIG = -1e30
LOG2_E = 1.4426950408889634
HI_MASK = 0xFFFF0000


def _cparams(*sem):
    return pltpu.CompilerParams(dimension_semantics=sem, vmem_limit_bytes=VMEM_LIMIT)


def _rms(x, g):
    return x * lax.rsqrt(jnp.mean(x * x, axis=-1, keepdims=True) + EPS) * g


def _pack_bf16_pair(lo, hi):
    return lax.bitcast_convert_type(hi, U32) | (lax.bitcast_convert_type(lo, U32) >> 16)


def _unpack_bf16_pair(w):
    lo = lax.bitcast_convert_type(w << 16, F32).astype(BF16)
    hi = lax.bitcast_convert_type(w & jnp.uint32(HI_MASK), F32).astype(BF16)
    return lo, hi


def _mod_kernel(c_ref, w_ref, b_ref, o_ref):
    c = c_ref[...]
    ca = c / (1.0 + jnp.exp(-c))
    o_ref[...] = jnp.dot(ca, w_ref[...], precision=lax.Precision.HIGHEST,
                         preferred_element_type=F32) + b_ref[...]


def _mod_call(c, w_mod, b_mod):
    B, D = c.shape
    n_out = w_mod.shape[1]
    tn = 512
    return pl.pallas_call(
        _mod_kernel,
        out_shape=jax.ShapeDtypeStruct((B, n_out), F32),
        grid=(n_out // tn,),
        in_specs=[pl.BlockSpec((B, D), lambda j: (0, 0)),
                  pl.BlockSpec((D, tn), lambda j: (0, j)),
                  pl.BlockSpec((1, tn), lambda j: (0, j))],
        out_specs=pl.BlockSpec((B, tn), lambda j: (0, j)),
        compiler_params=_cparams("arbitrary"),
        name="mod",
    )(c, w_mod, b_mod.reshape(1, n_out))


def _pre_kernel(x_ref, posr_ref, mod_ref, g_ref, win_ref, wkrT_ref, invfc_ref, wpool_ref, pscale_ref,
                qg_ref, wuqT_ref, kvg_ref, wuk_ref, wuvT_ref,
                qT_ref, k_ref, vT_ref, yp_ref,
                pbuf, b2, b4, b8):
    T = x_ref.shape[0]
    H = POOL_HALO
    i = pl.program_id(1)
    nt = (((1,), (1,)), ((), ()))

    shift = mod_ref[0]
    scale = mod_ref[1]
    hb = (_rms(x_ref[...], g_ref[...]) * (1.0 + scale) + shift).astype(BF16)
    u = jnp.dot(hb, win_ref[...], preferred_element_type=F32)
    krT = lax.dot_general(wkrT_ref[...], hb, nt, preferred_element_type=F32)

    p = u[:, :POOL_WIDTH]

    @pl.when(i == 0)
    def _():
        pbuf[0:H, :] = jnp.zeros((H, POOL_WIDTH), F32)

    pbuf[H:H + T, :] = p
    b2[8:T + H, :] = pbuf[8:T + H, :] + pbuf[7:T + H - 1, :]
    b4[16:T + H, :] = b2[16:T + H, :] + b2[14:T + H - 2, :]
    b8[24:T + H, :] = b4[24:T + H, :] + b4[20:T + H - 4, :]
    s2 = b2[H:T + H, :]
    s4 = b4[H:T + H, :]
    s8 = b8[H:T + H, :]
    s16 = b8[H:T + H, :] + b8[H - 8:T + H - 8, :]
    pbuf[0:H, :] = pbuf[T:T + H, :]

    lane = lax.broadcasted_iota(I32, (T, POOL_WIDTH), 1)
    t1 = (lax.broadcasted_iota(I32, (T, 1), 0) + (i * T + 1)).astype(F32)
    inv2 = 1.0 / jnp.minimum(t1, 2.0)
    inv4 = 1.0 / jnp.minimum(t1, 4.0)
    inv8 = 1.0 / jnp.minimum(t1, 8.0)
    inv16 = 1.0 / jnp.minimum(t1, 16.0)
    mean = jnp.where(lane < 64, s2 * inv2,
                     jnp.where(lane < 128, s4 * inv4,
                               jnp.where(lane < 192, s8 * inv8, s16 * inv16)))
    pooled = mean - p
    yp = jnp.dot(pooled.astype(BF16), wpool_ref[...], preferred_element_type=F32) * pscale_ref[...]
    yp_ref[...] = yp.astype(yp_ref.dtype)

    ang = invfc_ref[...] * posr_ref[...].astype(F32)
    cos_h = jnp.cos(ang)
    sin_h = jnp.sin(ang)
    cos_t = jnp.concatenate([cos_h, cos_h], axis=0)
    sin_t = jnp.concatenate([-sin_h, sin_h], axis=0)

    k_rope = (krT[0:QK_ROPE_DIM] * cos_t + krT[QK_ROPE_DIM:2 * QK_ROPE_DIM] * sin_t).T.astype(BF16)

    cq = u[:, POOL_WIDTH:POOL_WIDTH + Q_LORA_RANK]
    ckv = u[:, POOL_WIDTH + Q_LORA_RANK:POOL_WIDTH + Q_LORA_RANK + KV_LORA_RANK]
    cqn = _rms(cq, qg_ref[...]).astype(BF16)
    ckvn = _rms(ckv, kvg_ref[...]).astype(BF16)
    qaT = lax.dot_general(wuqT_ref[...], cqn, nt, preferred_element_type=F32)
    kn = jnp.dot(ckvn, wuk_ref[...], preferred_element_type=F32)
    vT = lax.dot_general(wuvT_ref[...], ckvn, nt, preferred_element_type=F32)
    nq = MLA_HEADS * QK_NOPE_DIM
    nr = MLA_HEADS * QK_ROPE_DIM
    for hd in range(MLA_HEADS):
        qr = qaT[nq + hd * QK_ROPE_DIM:nq + (hd + 1) * QK_ROPE_DIM, :]
        qr_sw = qaT[nq + nr + hd * QK_ROPE_DIM:nq + nr + (hd + 1) * QK_ROPE_DIM, :]
        qT_ref[hd, 0:QK_NOPE_DIM, :] = qaT[hd * QK_NOPE_DIM:(hd + 1) * QK_NOPE_DIM, :].astype(BF16)
        qT_ref[hd, QK_NOPE_DIM:QK_HEAD_DIM, :] = (qr * cos_t + qr_sw * sin_t).astype(BF16)
        k_ref[hd, :, 0:QK_NOPE_DIM] = kn[:, hd * QK_NOPE_DIM:(hd + 1) * QK_NOPE_DIM].astype(BF16)
        k_ref[hd, :, QK_NOPE_DIM:QK_HEAD_DIM] = k_rope
        vT_ref[hd, :, :] = vT[hd * V_HEAD_DIM:(hd + 1) * V_HEAD_DIM, :].astype(BF16)


def _pre_call(x, posr, mod4, norm_g, w_in_main, w_krT, inv_freq_col, wpool_bd, pool_scale, qg, w_uqT, kvg,
              w_uk, w_uvT, T):
    B, S, D = x.shape
    nH = MLA_HEADS
    const = lambda shape: pl.BlockSpec(shape, lambda b, i: (0,) * len(shape))
    return pl.pallas_call(
        _pre_kernel,
        out_shape=(jax.ShapeDtypeStruct((B, nH, QK_HEAD_DIM, S), BF16),
                   jax.ShapeDtypeStruct((B, nH, S, QK_HEAD_DIM), BF16),
                   jax.ShapeDtypeStruct((B, nH, V_HEAD_DIM, S), BF16),
                   jax.ShapeDtypeStruct((B, S, POOL_WIDTH), BF16)),
        grid=(B, S // T),
        in_specs=[pl.BlockSpec((None, T, D), lambda b, i: (b, i, 0)),
                  pl.BlockSpec((None, 1, T), lambda b, i: (b, 0, i)),
                  pl.BlockSpec((None, N_MOD, 1, D), lambda b, i: (b, 0, 0, 0)),
                  const((1, D)),
                  const(w_in_main.shape),
                  const(w_krT.shape),
                  const(inv_freq_col.shape),
                  const(wpool_bd.shape),
                  const((1, POOL_WIDTH)),
                  const((1, Q_LORA_RANK)),
                  const(w_uqT.shape),
                  const((1, KV_LORA_RANK)),
                  const(w_uk.shape),
                  const(w_uvT.shape)],
        out_specs=(pl.BlockSpec((None, nH, QK_HEAD_DIM, T), lambda b, i: (b, 0, 0, i)),
                   pl.BlockSpec((None, nH, T, QK_HEAD_DIM), lambda b, i: (b, 0, i, 0)),
                   pl.BlockSpec((None, nH, V_HEAD_DIM, T), lambda b, i: (b, 0, 0, i)),
                   pl.BlockSpec((None, T, POOL_WIDTH), lambda b, i: (b, i, 0))),
        scratch_shapes=[pltpu.VMEM((T + POOL_HALO, POOL_WIDTH), F32)] * 4,
        compiler_params=_cparams("arbitrary", "arbitrary"),
        name="pre",
    )(x, posr, mod4, norm_g, w_in_main, w_krT, inv_freq_col, wpool_bd, pool_scale, qg, w_uqT, kvg, w_uk, w_uvT)


def _attn_kernel(qT_ref, k_ref, vT_ref, o_ref):
    S = k_ref.shape[0]
    tq, tk = min(ATTN_TQ, S), ATTN_TK
    diag = lax.broadcasted_iota(I32, (tk, tk), 0) <= lax.broadcasted_iota(I32, (tk, tk), 1)
    for i in range(S // tq):
        qT = qT_ref[:, i * tq:(i + 1) * tq]
        m = jnp.full((1, tq), NEG_BIG, F32)
        l = jnp.zeros((1, tq), F32)
        acc = jnp.zeros((V_HEAD_DIM, tq), F32)
        for j in range((i + 1) * (tq // tk)):
            k = k_ref[j * tk:(j + 1) * tk, :]
            vT = vT_ref[:, j * tk:(j + 1) * tk]
            d = max(j * tk - i * tq, 0)
            sT = jnp.dot(k, qT[:, d:], preferred_element_type=F32)
            if j * tk >= i * tq:
                masked = jnp.where(diag, sT[:, :tk], NEG_BIG)
                sT = masked if sT.shape[1] == tk else jnp.concatenate([masked, sT[:, tk:]], axis=1)
            m_new = jnp.maximum(m[:, d:], jnp.max(sT, axis=0, keepdims=True))
            alpha = jnp.exp2(m[:, d:] - m_new)
            pT = jnp.exp2(sT - m_new)
            l_new = alpha * l[:, d:] + jnp.sum(pT, axis=0, keepdims=True)
            acc_new = alpha * acc[:, d:] + jnp.dot(vT, pT.astype(BF16), preferred_element_type=F32)
            if d:
                m_new = jnp.concatenate([m[:, :d], m_new], axis=1)
                l_new = jnp.concatenate([l[:, :d], l_new], axis=1)
                acc_new = jnp.concatenate([acc[:, :d], acc_new], axis=1)
            m, l, acc = m_new, l_new, acc_new
        o_ref[i * tq:(i + 1) * tq, :] = (acc * (1.0 / l)).T.astype(o_ref.dtype)


def _attn_call(qT, k, vT):
    B, nH, S, _ = k.shape
    return pl.pallas_call(
        _attn_kernel,
        out_shape=jax.ShapeDtypeStruct((B, S, nH * V_HEAD_DIM), BF16),
        grid=(B, nH),
        in_specs=[pl.BlockSpec((None, None, QK_HEAD_DIM, S), lambda b, h: (b, h, 0, 0)),
                  pl.BlockSpec((None, None, S, QK_HEAD_DIM), lambda b, h: (b, h, 0, 0)),
                  pl.BlockSpec((None, None, V_HEAD_DIM, S), lambda b, h: (b, h, 0, 0))],
        out_specs=pl.BlockSpec((None, S, V_HEAD_DIM), lambda b, h: (b, 0, h)),
        compiler_params=_cparams("arbitrary", "arbitrary"),
        name="attn",
    )(qT, k, vT)


def _post_kernel(x_ref, yp_ref, ya_ref, mod_ref, wo_p_ref, wo_a_ref, g_ref,
                 wgT_ref, bg_ref, wrT_ref, br_ref,
                 x1_ref, h2_ref, mi_ref, mw_ref, cnt_ref):
    T = x_ref.shape[0]

    gate_a = mod_ref[2]
    shift_f = mod_ref[3]
    scale_f = mod_ref[4]
    mix = (jnp.dot(yp_ref[...], wo_p_ref[...], preferred_element_type=F32)
           + jnp.dot(ya_ref[...], wo_a_ref[...], preferred_element_type=F32))
    x1 = x_ref[...] + gate_a * mix
    x1_ref[...] = x1
    hb = (_rms(x1, g_ref[...]) * (1.0 + scale_f) + shift_f).astype(BF16)
    h2_ref[...] = hb

    nt = (((1,), (1,)), ((), ()))
    gl = lax.dot_general(wgT_ref[...], hb, nt, preferred_element_type=F32)
    el = lax.dot_general(wrT_ref[...], hb, nt, preferred_element_type=F32)

    r8 = lax.broadcasted_iota(I32, (8, T), 0)
    gvalid = r8 < N_GROUPS
    gmax = jnp.max(jnp.where(gvalid, gl, NEG_BIG), axis=0, keepdims=True)
    gexp = jnp.where(gvalid, jnp.exp(gl - gmax), 0.0)
    g_prob = gexp / jnp.sum(gexp, axis=0, keepdims=True)
    gb = jnp.where(gvalid, gl + bg_ref[...], NEG_BIG)
    gbmax = jnp.max(gb, axis=0, keepdims=True)
    g_sel = jnp.min(jnp.where(gb == gbmax, r8, 8), axis=0, keepdims=True)
    gp = jnp.sum(jnp.where(r8 == g_sel, g_prob, 0.0), axis=0, keepdims=True)

    e_in = jnp.zeros((EXPERTS_PER_GROUP, T), F32)
    b_in = jnp.zeros((EXPERTS_PER_GROUP, T), F32)
    br = br_ref[...]
    for g in range(N_GROUPS):
        sel = g_sel == g
        e_in = jnp.where(sel, el[g * 8:(g + 1) * 8, :], e_in)
        b_in = jnp.where(sel, br[g * 8:(g + 1) * 8, :], b_in)
    eb = e_in + b_in
    m1 = jnp.max(eb, axis=0, keepdims=True)
    i1 = jnp.min(jnp.where(eb == m1, r8, 8), axis=0, keepdims=True)
    eb2 = jnp.where(r8 == i1, NEG_BIG, eb)
    m2 = jnp.max(eb2, axis=0, keepdims=True)
    i2 = jnp.min(jnp.where(eb2 == m2, r8, 8), axis=0, keepdims=True)
    emax = jnp.max(e_in, axis=0, keepdims=True)
    eexp = jnp.exp(e_in - emax)
    sp = eexp / jnp.sum(eexp, axis=0, keepdims=True)
    p1 = jnp.sum(jnp.where(r8 == i1, sp, 0.0), axis=0, keepdims=True)
    p2 = jnp.sum(jnp.where(r8 == i2, sp, 0.0), axis=0, keepdims=True)
    tot = p1 + p2
    w1 = gp * (p1 / tot)
    w2 = gp * (p2 / tot)
    e1 = g_sel * EXPERTS_PER_GROUP + i1
    e2 = g_sel * EXPERTS_PER_GROUP + i2

    r32 = lax.broadcasted_iota(I32, (N_EXPERTS, T), 0)
    oh1 = r32 == e1
    oh2 = r32 == e2
    oh = jnp.where(oh1 | oh2, 1.0, 0.0)
    upper = jnp.where(lax.broadcasted_iota(I32, (T, T), 0) < lax.broadcasted_iota(I32, (T, T), 1),
                      1.0, 0.0).astype(BF16)
    before = jnp.dot(oh.astype(BF16), upper, preferred_element_type=F32)
    cnt = jnp.sum(oh, axis=1, keepdims=True)
    run8 = jnp.floor((cnt + (SUBLANES - 1.0)) * (1.0 / SUBLANES))
    lower = jnp.where(lax.broadcasted_iota(I32, (N_EXPERTS, N_EXPERTS), 1)
                      < lax.broadcasted_iota(I32, (N_EXPERTS, N_EXPERTS), 0), 1.0, 0.0).astype(BF16)
    run_start = jnp.dot(lower, jnp.broadcast_to(run8, (N_EXPERTS, 128)).astype(BF16),
                        preferred_element_type=F32)[:, 0:1] * float(SUBLANES)
    pos = before + run_start
    row1 = jnp.sum(jnp.where(oh1, pos, 0.0), axis=0, keepdims=True)
    row2 = jnp.sum(jnp.where(oh2, pos, 0.0), axis=0, keepdims=True)
    cnt_ref[...] = cnt.astype(I32)

    mi_ref[0:1, :] = e1
    mi_ref[1:2, :] = e2
    mi_ref[2:3, :] = row1.astype(I32)
    mi_ref[3:4, :] = row2.astype(I32)
    mw_ref[0:1, :] = w1
    mw_ref[1:2, :] = w2


def _post_call(x2, yp2, ya2, mod4, wo_p, wo_a, g, wgT, bg, wrT, br, T, tiles_per_batch):
    N, D = x2.shape
    const = lambda shape: pl.BlockSpec(shape, lambda i: (0,) * len(shape))
    return pl.pallas_call(
        _post_kernel,
        out_shape=(jax.ShapeDtypeStruct((N, D), F32),
                   jax.ShapeDtypeStruct((N, D), BF16),
                   jax.ShapeDtypeStruct((4, N), I32),
                   jax.ShapeDtypeStruct((2, N), F32),
                   jax.ShapeDtypeStruct((N // T, N_EXPERTS, 1), I32)),
        grid=(N // T,),
        in_specs=[pl.BlockSpec((T, D), lambda i: (i, 0)),
                  pl.BlockSpec((T, POOL_WIDTH), lambda i: (i, 0)),
                  pl.BlockSpec((T, MLA_HEADS * V_HEAD_DIM), lambda i: (i, 0)),
                  pl.BlockSpec((None, N_MOD, 1, D), lambda i: (i // tiles_per_batch, 0, 0, 0)),
                  const(wo_p.shape), const(wo_a.shape), const((1, D)),
                  const(wgT.shape), const(bg.shape), const(wrT.shape), const(br.shape)],
        out_specs=(pl.BlockSpec((T, D), lambda i: (i, 0)),
                   pl.BlockSpec((T, D), lambda i: (i, 0)),
                   pl.BlockSpec((4, T), lambda i: (0, i)),
                   pl.BlockSpec((2, T), lambda i: (0, i)),
                   pl.BlockSpec((None, N_EXPERTS, 1), lambda i: (i, 0, 0))),
        compiler_params=_cparams("arbitrary"),
        name="post",
    )(x2, yp2, ya2, mod4, wo_p, wo_a, g, wgT, bg, wrT, br)


def _for_each_strip(tile, src_ref, dst_ref, n_ref, fn):
    first = tile * MAX_STRIPS
    for b in range(STRIP_BITS):
        last = first + n_ref[tile * STRIP_BITS + b]

        def body(r, c, n=SUBLANES << b):
            fn(pl.multiple_of(src_ref[r], SUBLANES), pl.multiple_of(dst_ref[r], SUBLANES), n)
            return c

        lax.fori_loop(first, last, body, 0)
        first = last


def _dispatch_kernel(src_ref, dst_ref, n_ref, pends_ref, padlen_ref, paddst_ref, mi_ref, h_ref, xs_ref,
                     lbuf, zbuf, sems, zsem):
    T = h_ref.shape[0]
    RB = zbuf.shape[0]
    half = h_ref.shape[1] // 2
    step = pl.program_id(0)
    n_steps = pl.num_programs(0)
    slot = step % DISPATCH_SLOTS

    def zero_fill(act):
        def per_expert(e, c):
            pad = padlen_ref[e]
            dst = paddst_ref[e]
            for b in reversed(range((RB // SUBLANES).bit_length() - 1)):
                n = SUBLANES << b
                done = pad & ~(2 * n - 1)

                @pl.when((pad & n) != 0)
                def _():
                    act(pltpu.make_async_copy(zbuf.at[pl.ds(0, n)],
                                              xs_ref.at[pl.ds(pl.multiple_of(dst + done, SUBLANES), n)], zsem))
            return c

        lax.fori_loop(0, N_EXPERTS, per_expert, 0)

        def per_block(b, c):
            act(pltpu.make_async_copy(zbuf, xs_ref.at[pl.ds(pl.multiple_of(b * RB, RB), RB)], zsem))
            return c

        lax.fori_loop(pends_ref[N_EXPERTS - 1] // RB, xs_ref.shape[0] // RB, per_block, 0)

    @pl.when(step == 0)
    def _():
        zbuf[...] = jnp.zeros(zbuf.shape, zbuf.dtype)
        zero_fill(lambda cp: cp.start())

    r = lax.broadcasted_iota(I32, (LOCAL_ROWS, T), 0)
    perm = jnp.where((r == mi_ref[2:3, :]) | (r == mi_ref[3:4, :]), 1.0, 0.0).astype(BF16)
    h = h_ref[...]
    lo = jnp.dot(perm, h[:, :half], preferred_element_type=F32)
    hi = jnp.dot(perm, h[:, half:], preferred_element_type=F32)
    lbuf[slot] = _pack_bf16_pair(lo, hi)

    def strip(tile_slot):
        def make(loc, dst, n):
            return pltpu.make_async_copy(lbuf.at[tile_slot, pl.ds(loc, n)], xs_ref.at[pl.ds(dst, n)],
                                         sems.at[tile_slot])
        return make

    def wait_tile(tile):
        _for_each_strip(tile, src_ref, dst_ref, n_ref, lambda *a: strip(tile % DISPATCH_SLOTS)(*a).wait())

    _for_each_strip(step, src_ref, dst_ref, n_ref, lambda *a: strip(slot)(*a).start())

    @pl.when(step >= DISPATCH_SLOTS - 1)
    def _():
        wait_tile(step - (DISPATCH_SLOTS - 1))

    @pl.when(step == n_steps - 1)
    def _():
        for back in reversed(range(DISPATCH_SLOTS - 1)):
            @pl.when(step >= back)
            def _():
                wait_tile(step - back)
        zero_fill(lambda cp: cp.wait())


def _dispatch_call(strip_src, strip_dst, strip_n, pends, pad_len, pad_dst, meta_i, h2, P_pad, T, RB):
    N, D = h2.shape
    return pl.pallas_call(
        _dispatch_kernel,
        out_shape=jax.ShapeDtypeStruct((P_pad, D // 2), U32),
        grid_spec=pltpu.PrefetchScalarGridSpec(
            num_scalar_prefetch=6,
            grid=(N // T,),
            in_specs=[pl.BlockSpec((4, T), lambda i, *_: (0, i)),
                      pl.BlockSpec((T, D), lambda i, *_: (i, 0))],
            out_specs=pl.BlockSpec(memory_space=pl.ANY),
            scratch_shapes=[pltpu.VMEM((DISPATCH_SLOTS, LOCAL_ROWS, D // 2), U32),
                            pltpu.VMEM((RB, D // 2), U32),
                            pltpu.SemaphoreType.DMA((DISPATCH_SLOTS,)),
                            pltpu.SemaphoreType.DMA]),
        compiler_params=_cparams("arbitrary"),
        name="dispatch",
    )(strip_src, strip_dst, strip_n, pends, pad_len, pad_dst, meta_i, h2)


def _expert_kernel(be_ref, nused_ref, xs_ref, wgu_ref, wd_ref, ys_ref, wgu_bf, wd_bf):
    i = pl.program_id(0)
    used = i < nused_ref[0]
    half = wgu_ref.shape[0] // 2

    @pl.when(used & ((i == 0) | (be_ref[i] != be_ref[jnp.maximum(i - 1, 0)])))
    def _():
        wgu_bf[...] = wgu_ref[...].astype(BF16)
        wd_bf[...] = wd_ref[...].astype(BF16)

    @pl.when(used)
    def _():
        lo, hi = _unpack_bf16_pair(xs_ref[...])
        gu = (jnp.dot(lo, wgu_bf[0:half, :], preferred_element_type=F32)
              + jnp.dot(hi, wgu_bf[half:, :], preferred_element_type=F32))
        gate = gu[:, :D_EXPERT]
        up = gu[:, D_EXPERT:]
        act = gate / (1.0 + jnp.exp(-gate)) * up
        y = jnp.dot(act.astype(BF16), wd_bf[...], preferred_element_type=F32)
        yb = y.astype(BF16).astype(F32)
        ys_ref[...] = _pack_bf16_pair(yb[:, :half], yb[:, half:])

    @pl.when(jnp.logical_not(used))
    def _():
        ys_ref[...] = jnp.zeros(ys_ref.shape, ys_ref.dtype)


def _expert_call(block_e, n_used, xs, wgu, wd, RB):
    P, Dh = xs.shape
    D = 2 * Dh
    row_map = lambda i, be, nu: (i, 0)
    in_map = lambda i, be, nu: (jnp.minimum(i, nu[0] - 1), 0)
    return pl.pallas_call(
        _expert_kernel,
        out_shape=jax.ShapeDtypeStruct((P, Dh), U32),
        grid_spec=pltpu.PrefetchScalarGridSpec(
            num_scalar_prefetch=2,
            grid=(P // RB,),
            in_specs=[pl.BlockSpec((RB, Dh), in_map),
                      pl.BlockSpec((None, D, 2 * D_EXPERT), lambda i, be, nu: (be[i], 0, 0)),
                      pl.BlockSpec((None, D_EXPERT, D), lambda i, be, nu: (be[i], 0, 0))],
            out_specs=pl.BlockSpec((RB, Dh), row_map),
            scratch_shapes=[pltpu.VMEM((D, 2 * D_EXPERT), BF16), pltpu.VMEM((D_EXPERT, D), BF16)]),
        compiler_params=_cparams("arbitrary"),
        name="experts",
    )(block_e, n_used, xs, wgu, wd)


def _combine_kernel(src_ref, dst_ref, n_ref, x1_ref, rows_ref, w_ref, mod_ref, g_ref, ys_ref, o_ref, ybuf, sems):
    T = x1_ref.shape[0]
    step = pl.program_id(0)
    n_steps = pl.num_programs(0)
    slot = step % 2

    def strip(tile_slot):
        def make(loc, dst, n):
            return pltpu.make_async_copy(ys_ref.at[pl.ds(dst, n)], ybuf.at[tile_slot, pl.ds(loc, n)],
                                         sems.at[tile_slot])
        return make

    @pl.when(step == 0)
    def _():
        ybuf[...] = jnp.zeros(ybuf.shape, ybuf.dtype)
        _for_each_strip(step, src_ref, dst_ref, n_ref, lambda *a: strip(slot)(*a).start())

    @pl.when(step + 1 < n_steps)
    def _():
        _for_each_strip(step + 1, src_ref, dst_ref, n_ref, lambda *a: strip(1 - slot)(*a).start())

    _for_each_strip(step, src_ref, dst_ref, n_ref, lambda *a: strip(slot)(*a).wait())

    c = lax.broadcasted_iota(I32, (T, LOCAL_ROWS), 1)
    rows = rows_ref[...]
    w = w_ref[...]
    permw = jnp.where(c == rows[:, 0:1], w[:, 0:1], jnp.where(c == rows[:, 1:2], w[:, 1:2], 0.0)).astype(BF16)
    lo, hi = _unpack_bf16_pair(ybuf[slot])
    moe = jnp.concatenate([jnp.dot(permw, lo, preferred_element_type=F32),
                           jnp.dot(permw, hi, preferred_element_type=F32)], axis=1)
    gate_f = mod_ref[5]
    o_ref[...] = _rms(x1_ref[...] + gate_f * moe, g_ref[...])


def _combine_call(strip_src, strip_dst, strip_n, x1, rows_tok, w_tok, mod4, final_g, ys, T, tiles_per_batch):
    N, D = x1.shape
    return pl.pallas_call(
        _combine_kernel,
        out_shape=jax.ShapeDtypeStruct((N, D), F32),
        grid_spec=pltpu.PrefetchScalarGridSpec(
            num_scalar_prefetch=3,
            grid=(N // T,),
            in_specs=[pl.BlockSpec((T, D), lambda i, *_: (i, 0)),
                      pl.BlockSpec((T, 2), lambda i, *_: (i, 0)),
                      pl.BlockSpec((T, 2), lambda i, *_: (i, 0)),
                      pl.BlockSpec((None, N_MOD, 1, D), lambda i, *_: (i // tiles_per_batch, 0, 0, 0)),
                      pl.BlockSpec((1, D), lambda i, *_: (0, 0)),
                      pl.BlockSpec(memory_space=pl.ANY)],
            out_specs=pl.BlockSpec((T, D), lambda i, *_: (i, 0)),
            scratch_shapes=[pltpu.VMEM((2, LOCAL_ROWS, D // 2), U32), pltpu.SemaphoreType.DMA((2,))]),
        compiler_params=_cparams("arbitrary"),
        name="combine",
    )(strip_src, strip_dst, strip_n, x1, rows_tok, w_tok, mod4, final_g, ys)


def _swap_halves(w):
    half = w.shape[-1] // 2
    return jnp.concatenate([w[..., half:], w[..., :half]], axis=-1)


def _round_up(v, m):
    return (v + m - 1) // m * m


def kernel(x, c, positions, w_mod, b_mod, norm_mix_g, w_in, w_pool, pool_scale, q_norm_g, w_uq, kv_norm_g, w_ukv, w_o, norm_ffn_g, w_group, b_group, w_router, b_router, w_gate_up, w_down, final_g):
    B, S, D = x.shape
    N = B * S
    depth = w_mod.shape[0]
    T = ROW_TILE
    RB = EXPERT_ROWS
    assert depth == 1, "the final RMSNorm is fused into the layer's combine step"
    assert S % T == 0 and S % min(ATTN_TQ, S) == 0 and min(ATTN_TQ, S) % ATTN_TK == 0
    tiles_per_batch = S // T
    n_tiles = N // T
    nH = MLA_HEADS
    l = 0

    inv_freq = ROPE_THETA ** (-(jnp.arange(0, QK_ROPE_DIM, 2, dtype=F32) / QK_ROPE_DIM))
    posr = positions.reshape(B, 1, S)
    cut1 = POOL_WIDTH
    cut2 = cut1 + Q_LORA_RANK
    cut3 = cut2 + KV_LORA_RANK

    mod4 = _mod_call(c, w_mod[l], b_mod[l]).reshape(B, N_MOD, 1, D)

    wi = w_in[l]
    w_in_main = wi[:, :cut3].astype(BF16)
    w_krT = jnp.concatenate([wi[:, cut3:], _swap_halves(wi[:, cut3:])], axis=1).T.astype(BF16)
    wq = w_uq[l].reshape(Q_LORA_RANK, nH, QK_HEAD_DIM)
    wq_n = wq[:, :, :QK_NOPE_DIM].reshape(Q_LORA_RANK, nH * QK_NOPE_DIM)
    wq_r = wq[:, :, QK_NOPE_DIM:]
    w_uqT = jnp.concatenate(
        [wq_n, wq_r.reshape(Q_LORA_RANK, nH * QK_ROPE_DIM),
         _swap_halves(wq_r).reshape(Q_LORA_RANK, nH * QK_ROPE_DIM)], axis=1).T.astype(BF16)
    wkv = w_ukv[l].reshape(KV_LORA_RANK, nH, QK_NOPE_DIM + V_HEAD_DIM)
    w_uk = wkv[:, :, :QK_NOPE_DIM].reshape(KV_LORA_RANK, nH * QK_NOPE_DIM).astype(BF16)
    w_uvT = wkv[:, :, QK_NOPE_DIM:].reshape(KV_LORA_RANK, nH * V_HEAD_DIM).T.astype(BF16)
    qg = (q_norm_g[l] * (QK_HEAD_DIM ** -0.5 * LOG2_E)).reshape(1, Q_LORA_RANK)
    wpool_bd = jnp.zeros((POOL_WIDTH, POOL_WIDTH), F32)
    for g in range(len(POOL_WINDOWS)):
        sl = slice(g * POOL_GROUP_DIM, (g + 1) * POOL_GROUP_DIM)
        wpool_bd = wpool_bd.at[sl, sl].set(w_pool[l, g])
    wpool_bd = wpool_bd.astype(BF16)

    qT, k, vT, yp = _pre_call(
        x, posr, mod4, norm_mix_g[l].reshape(1, D), w_in_main, w_krT, inv_freq.reshape(QK_ROPE_DIM // 2, 1),
        wpool_bd, pool_scale[l].reshape(1, POOL_WIDTH), qg, w_uqT, kv_norm_g[l].reshape(1, KV_LORA_RANK),
        w_uk, w_uvT, T)
    ya = _attn_call(qT, k, vT)

    wo = w_o[l].astype(BF16)
    wgT = jnp.zeros((8, D), F32).at[:N_GROUPS].set(w_group[l].T).astype(BF16)
    bg = jnp.zeros((8, 1), F32).at[:N_GROUPS, 0].set(b_group[l])
    x1, h2, meta_i, meta_w, tile_cnt = _post_call(
        x.reshape(N, D), yp.reshape(N, POOL_WIDTH), ya.reshape(N, nH * V_HEAD_DIM), mod4,
        wo[:POOL_WIDTH], wo[POOL_WIDTH:], norm_ffn_g[l].reshape(1, D),
        wgT, bg, w_router[l].T.astype(BF16), b_router[l].reshape(N_EXPERTS, 1), T, tiles_per_batch)

    run_len = _round_up(tile_cnt[:, :, 0], SUBLANES)
    run_loc = jnp.cumsum(run_len, axis=1) - run_len
    seg_rows = jnp.sum(run_len, axis=0)
    seg_len = _round_up(seg_rows, RB)
    pends = jnp.cumsum(seg_len).astype(I32)
    pstarts = pends - seg_len
    run_dst = pstarts[None, :] + jnp.cumsum(run_len, axis=0) - run_len
    pad_len = (seg_len - seg_rows).astype(I32)
    pad_dst = (pstarts + seg_rows).astype(I32)
    bits = jnp.arange(STRIP_BITS, dtype=I32)
    size = SUBLANES << bits
    valid = (run_len[:, None, :] & size[None, :, None]) != 0
    done = run_len[:, None, :] & ~(2 * size[None, :, None] - 1)
    flat = lambda a: a.reshape(n_tiles, STRIP_BITS * N_EXPERTS)
    valid_f = flat(valid)
    pos = jnp.cumsum(valid_f.astype(I32), axis=1) - valid_f.astype(I32)
    pick = valid_f[:, None, :] & (pos[:, None, :] == jnp.arange(MAX_STRIPS, dtype=I32)[None, :, None])
    gather = lambda a: jnp.sum(jnp.where(pick, flat(a)[:, None, :], 0), axis=-1).reshape(-1).astype(I32)
    strip_src = gather(run_loc[:, None, :] + done)
    strip_dst = gather(run_dst[:, None, :] + done)
    strip_n = jnp.sum(valid, axis=2).reshape(-1).astype(I32)
    P_pad = _round_up(2 * N + n_tiles * N_EXPERTS * (SUBLANES - 1), RB) + N_EXPERTS * RB
    n_rb = P_pad // RB
    n_used = (pends[-1:] // RB).astype(I32)
    block_start = jnp.minimum(jnp.arange(n_rb, dtype=I32), n_used - 1) * RB
    block_e = jnp.sum((pends[None, :] <= block_start[:, None]).astype(I32), axis=1)

    xs = _dispatch_call(strip_src, strip_dst, strip_n, pends, pad_len, pad_dst, meta_i, h2, P_pad, T, RB)
    ys = _expert_call(block_e, n_used, xs, w_gate_up[l], w_down[l], RB)
    out = _combine_call(strip_src, strip_dst, strip_n, x1, meta_i[2:4].T, meta_w.T, mod4, final_g.reshape(1, D), ys,
                        T, tiles_per_batch)
    return out.reshape(B, S, D)
```

```python
import jax
import jax.numpy as jnp
from jax import lax
from jax.experimental import pallas as pl
from jax.experimental.pallas import tpu as pltpu

F32 = jnp.float32
BF16 = jnp.bfloat16
U32 = jnp.uint32
I32 = jnp.int32

POOL_WINDOWS = (2, 4, 8, 16)
POOL_GROUP_DIM = 64
POOL_WIDTH = 256
MLA_HEADS = 6
QK_NOPE_DIM = 128
QK_ROPE_DIM = 64
QK_HEAD_DIM = QK_NOPE_DIM + QK_ROPE_DIM
V_HEAD_DIM = 128
V_ROWS = 144
Q_LORA_RANK = 512
KV_LORA_RANK = 256
ROPE_THETA = 10000.0
N_GROUPS = 4
EXPERTS_PER_GROUP = 8
N_EXPERTS = N_GROUPS * EXPERTS_PER_GROUP
D_EXPERT = 256
N_MOD = 6
EPS = 1e-6

SUBLANES = 8
POOL_HALO = 32
ROW_TILE = 512
ATTN_TQ = 2048
ATTN_TK = 512
EXPERT_ROWS = 512
LOCAL_ROWS = 2 * ROW_TILE + 256
assert LOCAL_ROWS >= 2 * ROW_TILE + N_EXPERTS * (SUBLANES - 1)
STRIP_BITS = (2 * ROW_TILE // SUBLANES).bit_length()
MAX_STRIPS = 128
assert MAX_STRIPS >= 2 * N_EXPERTS + (LOCAL_ROWS // SUBLANES - 3 * N_EXPERTS) // 4
DISPATCH_SLOTS = 3
VMEM_LIMIT = 56 * 1024 * 1024
NEG_BIG = -1e30
LOG2_E = 1.4426950408889634
HI_MASK = 0xFFFF0000


def _cparams(*sem):
    return pltpu.CompilerParams(dimension_semantics=sem, vmem_limit_bytes=VMEM_LIMIT)


def _rms(x, g):
    return x * lax.rsqrt(jnp.mean(x * x, axis=-1, keepdims=True) + EPS) * g


def _pack_bf16_pair(lo, hi):
    return lax.bitcast_convert_type(hi, U32) | (lax.bitcast_convert_type(lo, U32) >> 16)


def _unpack_bf16_pair(w):
    lo = lax.bitcast_convert_type(w << 16, F32).astype(BF16)
    hi = lax.bitcast_convert_type(w & jnp.uint32(HI_MASK), F32).astype(BF16)
    return lo, hi


def _mod_kernel(c_ref, w_ref, b_ref, o_ref):
    c = c_ref[...]
    ca = c / (1.0 + jnp.exp(-c))
    o_ref[...] = jnp.dot(ca, w_ref[...], precision=lax.Precision.HIGHEST,
                         preferred_element_type=F32) + b_ref[...]


def _mod_call(c, w_mod, b_mod):
    B, D = c.shape
    n_out = w_mod.shape[1]
    tn = 512
    return pl.pallas_call(
        _mod_kernel,
        out_shape=jax.ShapeDtypeStruct((B, n_out), F32),
        grid=(n_out // tn,),
        in_specs=[pl.BlockSpec((B, D), lambda j: (0, 0)),
                  pl.BlockSpec((D, tn), lambda j: (0, j)),
                  pl.BlockSpec((1, tn), lambda j: (0, j))],
        out_specs=pl.BlockSpec((B, tn), lambda j: (0, j)),
        compiler_params=_cparams("arbitrary"),
        name="mod",
    )(c, w_mod, b_mod.reshape(1, n_out))


def _pre_kernel(x_ref, posr_ref, mod_ref, g_ref, win_ref, wkrT_ref, invfc_ref, wpool_ref, pscale_ref,
                qg_ref, wuqT_ref, kvg_ref, wuk_ref, wuvT_ref,
                qT_ref, k_ref, vT_ref, yp_ref,
                pbuf, b2, b4, b8):
    T = x_ref.shape[0]
    H = POOL_HALO
    i = pl.program_id(1)
    nt = (((1,), (1,)), ((), ()))

    shift = mod_ref[0]
    scale = mod_ref[1]
    hb = (_rms(x_ref[...], g_ref[...]) * (1.0 + scale) + shift).astype(BF16)
    u = jnp.dot(hb, win_ref[...], preferred_element_type=F32)
    krT = lax.dot_general(wkrT_ref[...], hb, nt, preferred_element_type=F32)

    p = u[:, :POOL_WIDTH]

    @pl.when(i == 0)
    def _():
        pbuf[0:H, :] = jnp.zeros((H, POOL_WIDTH), F32)

    pbuf[H:H + T, :] = p
    b2[8:T + H, :] = pbuf[8:T + H, :] + pbuf[7:T + H - 1, :]
    b4[16:T + H, :] = b2[16:T + H, :] + b2[14:T + H - 2, :]
    b8[24:T + H, :] = b4[24:T + H, :] + b4[20:T + H - 4, :]
    s2 = b2[H:T + H, :]
    s4 = b4[H:T + H, :]
    s8 = b8[H:T + H, :]
    s16 = b8[H:T + H, :] + b8[H - 8:T + H - 8, :]
    pbuf[0:H, :] = pbuf[T:T + H, :]

    lane = lax.broadcasted_iota(I32, (T, POOL_WIDTH), 1)
    t1 = (lax.broadcasted_iota(I32, (T, 1), 0) + (i * T + 1)).astype(F32)
    inv2 = 1.0 / jnp.minimum(t1, 2.0)
    inv4 = 1.0 / jnp.minimum(t1, 4.0)
    inv8 = 1.0 / jnp.minimum(t1, 8.0)
    inv16 = 1.0 / jnp.minimum(t1, 16.0)
    mean = jnp.where(lane < 64, s2 * inv2,
                     jnp.where(lane < 128, s4 * inv4,
                               jnp.where(lane < 192, s8 * inv8, s16 * inv16)))
    pooled = mean - p
    yp = jnp.dot(pooled.astype(BF16), wpool_ref[...], preferred_element_type=F32) * pscale_ref[...]
    yp_ref[...] = yp.astype(yp_ref.dtype)

    ang = invfc_ref[...] * posr_ref[...].astype(F32)
    cos_h = jnp.cos(ang)
    sin_h = jnp.sin(ang)
    hr = QK_ROPE_DIM // 2

    def rope(xt):
        x1, x2 = xt[0:hr], xt[hr:]
        return x1 * cos_h - x2 * sin_h, x2 * cos_h + x1 * sin_h

    k_rope = jnp.concatenate(rope(krT), axis=0).T.astype(BF16)

    cq = u[:, POOL_WIDTH:POOL_WIDTH + Q_LORA_RANK]
    ckv = u[:, POOL_WIDTH + Q_LORA_RANK:POOL_WIDTH + Q_LORA_RANK + KV_LORA_RANK]
    cqn = _rms(cq, qg_ref[...]).astype(BF16)
    ckvn = _rms(ckv, kvg_ref[...]).astype(BF16)
    qaT = lax.dot_general(wuqT_ref[...], cqn, nt, preferred_element_type=F32)
    kn = jnp.dot(ckvn, wuk_ref[...], preferred_element_type=F32)
    vT = lax.dot_general(wuvT_ref[...], ckvn, nt, preferred_element_type=F32)
    nq = MLA_HEADS * QK_NOPE_DIM
    ones_rows = jnp.where(lax.broadcasted_iota(I32, (V_ROWS - V_HEAD_DIM, T), 0) == 0, 1.0, 0.0).astype(BF16)
    for hd in range(MLA_HEADS):
        q1, q2 = rope(qaT[nq + hd * QK_ROPE_DIM:nq + (hd + 1) * QK_ROPE_DIM, :])
        qT_ref[hd, 0:QK_NOPE_DIM, :] = qaT[hd * QK_NOPE_DIM:(hd + 1) * QK_NOPE_DIM, :].astype(BF16)
        qT_ref[hd, QK_NOPE_DIM:QK_NOPE_DIM + hr, :] = q1.astype(BF16)
        qT_ref[hd, QK_NOPE_DIM + hr:QK_HEAD_DIM, :] = q2.astype(BF16)
        k_ref[hd, :, 0:QK_NOPE_DIM] = kn[:, hd * QK_NOPE_DIM:(hd + 1) * QK_NOPE_DIM].astype(BF16)
        k_ref[hd, :, QK_NOPE_DIM:QK_HEAD_DIM] = k_rope
        vT_ref[hd, 0:V_HEAD_DIM, :] = vT[hd * V_HEAD_DIM:(hd + 1) * V_HEAD_DIM, :].astype(BF16)
        vT_ref[hd, V_HEAD_DIM:V_ROWS, :] = ones_rows


def _pre_call(x, posr, mod4, norm_g, w_in_main, w_krT, inv_freq_col, wpool_bd, pool_scale, qg, w_uqT, kvg,
              w_uk, w_uvT, T):
    B, S, D = x.shape
    nH = MLA_HEADS
    const = lambda shape: pl.BlockSpec(shape, lambda b, i: (0,) * len(shape))
    return pl.pallas_call(
        _pre_kernel,
        out_shape=(jax.ShapeDtypeStruct((B, nH, QK_HEAD_DIM, S), BF16),
                   jax.ShapeDtypeStruct((B, nH, S, QK_HEAD_DIM), BF16),
                   jax.ShapeDtypeStruct((B, nH, V_ROWS, S), BF16),
                   jax.ShapeDtypeStruct((B, S, POOL_WIDTH), BF16)),
        grid=(B, S // T),
        in_specs=[pl.BlockSpec((None, T, D), lambda b, i: (b, i, 0)),
                  pl.BlockSpec((None, 1, T), lambda b, i: (b, 0, i)),
                  pl.BlockSpec((None, N_MOD, 1, D), lambda b, i: (b, 0, 0, 0)),
                  const((1, D)),
                  const(w_in_main.shape),
                  const(w_krT.shape),
                  const(inv_freq_col.shape),
                  const(wpool_bd.shape),
                  const((1, POOL_WIDTH)),
                  const((1, Q_LORA_RANK)),
                  const(w_uqT.shape),
                  const((1, KV_LORA_RANK)),
                  const(w_uk.shape),
                  const(w_uvT.shape)],
        out_specs=(pl.BlockSpec((None, nH, QK_HEAD_DIM, T), lambda b, i: (b, 0, 0, i)),
                   pl.BlockSpec((None, nH, T, QK_HEAD_DIM), lambda b, i: (b, 0, i, 0)),
                   pl.BlockSpec((None, nH, V_ROWS, T), lambda b, i: (b, 0, 0, i)),
                   pl.BlockSpec((None, T, POOL_WIDTH), lambda b, i: (b, i, 0))),
        scratch_shapes=[pltpu.VMEM((T + POOL_HALO, POOL_WIDTH), F32)] * 4,
        compiler_params=_cparams("arbitrary", "arbitrary"),
        name="pre",
    )(x, posr, mod4, norm_g, w_in_main, w_krT, inv_freq_col, wpool_bd, pool_scale, qg, w_uqT, kvg, w_uk, w_uvT)


def _attn_kernel(qT_ref, k_ref, vT_ref, o_ref):
    S = k_ref.shape[0]
    tq, tk = min(ATTN_TQ, S), ATTN_TK
    diag = lax.broadcasted_iota(I32, (tk, tk), 0) <= lax.broadcasted_iota(I32, (tk, tk), 1)
    for i in range(S // tq):
        qT = qT_ref[:, i * tq:(i + 1) * tq]
        m = jnp.full((1, tq), NEG_BIG, F32)
        acc = jnp.zeros((V_ROWS, tq), F32)
        for j in range((i + 1) * (tq // tk)):
            k = k_ref[j * tk:(j + 1) * tk, :]
            vT = vT_ref[:, j * tk:(j + 1) * tk]
            d = max(j * tk - i * tq, 0)
            sT = jnp.dot(k, qT[:, d:], preferred_element_type=F32)
            if j * tk >= i * tq:
                masked = jnp.where(diag, sT[:, :tk], NEG_BIG)
                sT = masked if sT.shape[1] == tk else jnp.concatenate([masked, sT[:, tk:]], axis=1)
            m_new = jnp.maximum(m[:, d:], jnp.max(sT, axis=0, keepdims=True))
            pT = jnp.exp2(sT - m_new).astype(BF16)
            acc_new = jnp.exp2(m[:, d:] - m_new) * acc[:, d:] + jnp.dot(vT, pT, preferred_element_type=F32)
            if d:
                m_new = jnp.concatenate([m[:, :d], m_new], axis=1)
                acc_new = jnp.concatenate([acc[:, :d], acc_new], axis=1)
            m, acc = m_new, acc_new
        out = acc[0:V_HEAD_DIM] * (1.0 / acc[V_HEAD_DIM:V_HEAD_DIM + 1])
        o_ref[i * tq:(i + 1) * tq, :] = out.T.astype(o_ref.dtype)


def _attn_call(qT, k, vT):
    B, nH, S, _ = k.shape
    return pl.pallas_call(
        _attn_kernel,
        out_shape=jax.ShapeDtypeStruct((B, S, nH * V_HEAD_DIM), BF16),
        grid=(B, nH),
        in_specs=[pl.BlockSpec((None, None, QK_HEAD_DIM, S), lambda b, h: (b, h, 0, 0)),
                  pl.BlockSpec((None, None, S, QK_HEAD_DIM), lambda b, h: (b, h, 0, 0)),
                  pl.BlockSpec((None, None, V_ROWS, S), lambda b, h: (b, h, 0, 0))],
        out_specs=pl.BlockSpec((None, S, V_HEAD_DIM), lambda b, h: (b, 0, h)),
        compiler_params=_cparams("arbitrary", "arbitrary"),
        name="attn",
    )(qT, k, vT)


def _post_kernel(x_ref, yp_ref, ya_ref, mod_ref, wo_p_ref, wo_a_ref, g_ref,
                 wgT_ref, bg_ref, wrT_ref, br_ref,
                 x1_ref, h2_ref, mi_ref, mw_ref, cnt_ref):
    T = x_ref.shape[0]

    gate_a = mod_ref[2]
    shift_f = mod_ref[3]
    scale_f = mod_ref[4]
    mix = (jnp.dot(yp_ref[...], wo_p_ref[...], preferred_element_type=F32)
           + jnp.dot(ya_ref[...], wo_a_ref[...], preferred_element_type=F32))
    x1 = x_ref[...] + gate_a * mix
    x1_ref[...] = x1
    hb = (_rms(x1, g_ref[...]) * (1.0 + scale_f) + shift_f).astype(BF16)
    h2_ref[...] = hb

    nt = (((1,), (1,)), ((), ()))
    gl = lax.dot_general(wgT_ref[...], hb, nt, preferred_element_type=F32)
    el = lax.dot_general(wrT_ref[...], hb, nt, preferred_element_type=F32)

    r8 = lax.broadcasted_iota(I32, (8, T), 0)
    gvalid = r8 < N_GROUPS
    gmax = jnp.max(jnp.where(gvalid, gl, NEG_BIG), axis=0, keepdims=True)
    gexp = jnp.where(gvalid, jnp.exp(gl - gmax), 0.0)
    g_prob = gexp / jnp.sum(gexp, axis=0, keepdims=True)
    gb = jnp.where(gvalid, gl + bg_ref[...], NEG_BIG)
    gbmax = jnp.max(gb, axis=0, keepdims=True)
    g_sel = jnp.min(jnp.where(gb == gbmax, r8, 8), axis=0, keepdims=True)
    gp = jnp.sum(jnp.where(r8 == g_sel, g_prob, 0.0), axis=0, keepdims=True)

    e_in = jnp.zeros((EXPERTS_PER_GROUP, T), F32)
    b_in = jnp.zeros((EXPERTS_PER_GROUP, T), F32)
    br = br_ref[...]
    for g in range(N_GROUPS):
        sel = g_sel == g
        e_in = jnp.where(sel, el[g * 8:(g + 1) * 8, :], e_in)
        b_in = jnp.where(sel, br[g * 8:(g + 1) * 8, :], b_in)
    eb = e_in + b_in
    m1 = jnp.max(eb, axis=0, keepdims=True)
    i1 = jnp.min(jnp.where(eb == m1, r8, 8), axis=0, keepdims=True)
    eb2 = jnp.where(r8 == i1, NEG_BIG, eb)
    m2 = jnp.max(eb2, axis=0, keepdims=True)
    i2 = jnp.min(jnp.where(eb2 == m2, r8, 8), axis=0, keepdims=True)
    emax = jnp.max(e_in, axis=0, keepdims=True)
    eexp = jnp.exp(e_in - emax)
    sp = eexp / jnp.sum(eexp, axis=0, keepdims=True)
    p1 = jnp.sum(jnp.where(r8 == i1, sp, 0.0), axis=0, keepdims=True)
    p2 = jnp.sum(jnp.where(r8 == i2, sp, 0.0), axis=0, keepdims=True)
    tot = p1 + p2
    w1 = gp * (p1 / tot)
    w2 = gp * (p2 / tot)
    e1 = g_sel * EXPERTS_PER_GROUP + i1
    e2 = g_sel * EXPERTS_PER_GROUP + i2

    r32 = lax.broadcasted_iota(I32, (N_EXPERTS, T), 0)
    oh1 = r32 == e1
    oh2 = r32 == e2
    oh = jnp.where(oh1 | oh2, 1.0, 0.0)
    upper = jnp.where(lax.broadcasted_iota(I32, (T, T), 0) < lax.broadcasted_iota(I32, (T, T), 1),
                      1.0, 0.0).astype(BF16)
    before = jnp.dot(oh.astype(BF16), upper, preferred_element_type=F32)
    cnt = jnp.sum(oh, axis=1, keepdims=True)
    run8 = jnp.floor((cnt + (SUBLANES - 1.0)) * (1.0 / SUBLANES))
    lower = jnp.where(lax.broadcasted_iota(I32, (N_EXPERTS, N_EXPERTS), 1)
                      < lax.broadcasted_iota(I32, (N_EXPERTS, N_EXPERTS), 0), 1.0, 0.0).astype(BF16)
    run_start = jnp.dot(lower, jnp.broadcast_to(run8, (N_EXPERTS, 128)).astype(BF16),
                        preferred_element_type=F32)[:, 0:1] * float(SUBLANES)
    pos = before + run_start
    row1 = jnp.sum(jnp.where(oh1, pos, 0.0), axis=0, keepdims=True)
    row2 = jnp.sum(jnp.where(oh2, pos, 0.0), axis=0, keepdims=True)
    cnt_ref[...] = cnt.astype(I32)

    mi_ref[0:1, :] = e1
    mi_ref[1:2, :] = e2
    mi_ref[2:3, :] = row1.astype(I32)
    mi_ref[3:4, :] = row2.astype(I32)
    mw_ref[0:1, :] = w1
    mw_ref[1:2, :] = w2


def _post_call(x2, yp2, ya2, mod4, wo_p, wo_a, g, wgT, bg, wrT, br, T, tiles_per_batch):
    N, D = x2.shape
    const = lambda shape: pl.BlockSpec(shape, lambda i: (0,) * len(shape))
    return pl.pallas_call(
        _post_kernel,
        out_shape=(jax.ShapeDtypeStruct((N, D), F32),
                   jax.ShapeDtypeStruct((N, D), BF16),
                   jax.ShapeDtypeStruct((4, N), I32),
                   jax.ShapeDtypeStruct((2, N), F32),
                   jax.ShapeDtypeStruct((N // T, N_EXPERTS, 1), I32)),
        grid=(N // T,),
        in_specs=[pl.BlockSpec((T, D), lambda i: (i, 0)),
                  pl.BlockSpec((T, POOL_WIDTH), lambda i: (i, 0)),
                  pl.BlockSpec((T, MLA_HEADS * V_HEAD_DIM), lambda i: (i, 0)),
                  pl.BlockSpec((None, N_MOD, 1, D), lambda i: (i // tiles_per_batch, 0, 0, 0)),
                  const(wo_p.shape), const(wo_a.shape), const((1, D)),
                  const(wgT.shape), const(bg.shape), const(wrT.shape), const(br.shape)],
        out_specs=(pl.BlockSpec((T, D), lambda i: (i, 0)),
                   pl.BlockSpec((T, D), lambda i: (i, 0)),
                   pl.BlockSpec((4, T), lambda i: (0, i)),
                   pl.BlockSpec((2, T), lambda i: (0, i)),
                   pl.BlockSpec((None, N_EXPERTS, 1), lambda i: (i, 0, 0))),
        compiler_params=_cparams("arbitrary"),
        name="post",
    )(x2, yp2, ya2, mod4, wo_p, wo_a, g, wgT, bg, wrT, br)


def _for_each_strip(tile, src_ref, dst_ref, n_ref, fn):
    first = tile * MAX_STRIPS
    for b in range(STRIP_BITS):
        last = first + n_ref[tile * STRIP_BITS + b]

        def body(r, c, n=SUBLANES << b):
            fn(pl.multiple_of(src_ref[r], SUBLANES), pl.multiple_of(dst_ref[r], SUBLANES), n)
            return c

        lax.fori_loop(first, last, body, 0)
        first = last


def _dispatch_kernel(src_ref, dst_ref, n_ref, pends_ref, padlen_ref, paddst_ref, mi_ref, h_ref, xs_ref,
                     lbuf, zbuf, sems, zsem):
    T = h_ref.shape[0]
    RB = zbuf.shape[0]
    half = h_ref.shape[1] // 2
    step = pl.program_id(0)
    n_steps = pl.num_programs(0)
    slot = step % DISPATCH_SLOTS

    def zero_fill(act):
        def per_expert(e, c):
            pad = padlen_ref[e]
            dst = paddst_ref[e]
            for b in reversed(range((RB // SUBLANES).bit_length() - 1)):
                n = SUBLANES << b
                done = pad & ~(2 * n - 1)

                @pl.when((pad & n) != 0)
                def _():
                    act(pltpu.make_async_copy(zbuf.at[pl.ds(0, n)],
                                              xs_ref.at[pl.ds(pl.multiple_of(dst + done, SUBLANES), n)], zsem))
            return c

        lax.fori_loop(0, N_EXPERTS, per_expert, 0)

        def per_block(b, c):
            act(pltpu.make_async_copy(zbuf, xs_ref.at[pl.ds(pl.multiple_of(b * RB, RB), RB)], zsem))
            return c

        lax.fori_loop(pends_ref[N_EXPERTS - 1] // RB, xs_ref.shape[0] // RB, per_block, 0)

    @pl.when(step == 0)
    def _():
        zbuf[...] = jnp.zeros(zbuf.shape, zbuf.dtype)
        zero_fill(lambda cp: cp.start())

    r = lax.broadcasted_iota(I32, (LOCAL_ROWS, T), 0)
    perm = jnp.where((r == mi_ref[2:3, :]) | (r == mi_ref[3:4, :]), 1.0, 0.0).astype(BF16)
    h = h_ref[...]
    lo = jnp.dot(perm, h[:, :half], preferred_element_type=F32)
    hi = jnp.dot(perm, h[:, half:], preferred_element_type=F32)
    lbuf[slot] = _pack_bf16_pair(lo, hi)

    def strip(tile_slot):
        def make(loc, dst, n):
            return pltpu.make_async_copy(lbuf.at[tile_slot, pl.ds(loc, n)], xs_ref.at[pl.ds(dst, n)],
                                         sems.at[tile_slot])
        return make

    def wait_tile(tile):
        _for_each_strip(tile, src_ref, dst_ref, n_ref, lambda *a: strip(tile % DISPATCH_SLOTS)(*a).wait())

    _for_each_strip(step, src_ref, dst_ref, n_ref, lambda *a: strip(slot)(*a).start())

    @pl.when(step >= DISPATCH_SLOTS - 1)
    def _():
        wait_tile(step - (DISPATCH_SLOTS - 1))

    @pl.when(step == n_steps - 1)
    def _():
        for back in reversed(range(DISPATCH_SLOTS - 1)):
            @pl.when(step >= back)
            def _():
                wait_tile(step - back)
        zero_fill(lambda cp: cp.wait())


def _dispatch_call(strip_src, strip_dst, strip_n, pends, pad_len, pad_dst, meta_i, h2, P_pad, T, RB):
    N, D = h2.shape
    return pl.pallas_call(
        _dispatch_kernel,
        out_shape=jax.ShapeDtypeStruct((P_pad, D // 2), U32),
        grid_spec=pltpu.PrefetchScalarGridSpec(
            num_scalar_prefetch=6,
            grid=(N // T,),
            in_specs=[pl.BlockSpec((4, T), lambda i, *_: (0, i)),
                      pl.BlockSpec((T, D), lambda i, *_: (i, 0))],
            out_specs=pl.BlockSpec(memory_space=pl.ANY),
            scratch_shapes=[pltpu.VMEM((DISPATCH_SLOTS, LOCAL_ROWS, D // 2), U32),
                            pltpu.VMEM((RB, D // 2), U32),
                            pltpu.SemaphoreType.DMA((DISPATCH_SLOTS,)),
                            pltpu.SemaphoreType.DMA]),
        compiler_params=_cparams("arbitrary"),
        name="dispatch",
    )(strip_src, strip_dst, strip_n, pends, pad_len, pad_dst, meta_i, h2)


def _expert_kernel(be_ref, nused_ref, xs_ref, wgu_ref, wd_ref, ys_ref, wgu_bf, wd_bf):
    i = pl.program_id(0)
    used = i < nused_ref[0]
    half = wgu_ref.shape[0] // 2

    @pl.when(used & ((i == 0) | (be_ref[i] != be_ref[jnp.maximum(i - 1, 0)])))
    def _():
        wgu_bf[...] = wgu_ref[...].astype(BF16)
        wd_bf[...] = wd_ref[...].astype(BF16)

    @pl.when(used)
    def _():
        lo, hi = _unpack_bf16_pair(xs_ref[...])
        gu = (jnp.dot(lo, wgu_bf[0:half, :], preferred_element_type=F32)
              + jnp.dot(hi, wgu_bf[half:, :], preferred_element_type=F32))
        gate = gu[:, :D_EXPERT]
        up = gu[:, D_EXPERT:]
        act = gate / (1.0 + jnp.exp(-gate)) * up
        y = jnp.dot(act.astype(BF16), wd_bf[...], preferred_element_type=F32)
        yb = y.astype(BF16).astype(F32)
        ys_ref[...] = _pack_bf16_pair(yb[:, :half], yb[:, half:])

    @pl.when(jnp.logical_not(used))
    def _():
        ys_ref[...] = jnp.zeros(ys_ref.shape, ys_ref.dtype)


def _expert_call(block_e, n_used, xs, wgu, wd, RB):
    P, Dh = xs.shape
    D = 2 * Dh
    row_map = lambda i, be, nu: (i, 0)
    in_map = lambda i, be, nu: (jnp.minimum(i, nu[0] - 1), 0)
    return pl.pallas_call(
        _expert_kernel,
        out_shape=jax.ShapeDtypeStruct((P, Dh), U32),
        grid_spec=pltpu.PrefetchScalarGridSpec(
            num_scalar_prefetch=2,
            grid=(P // RB,),
            in_specs=[pl.BlockSpec((RB, Dh), in_map),
                      pl.BlockSpec((None, D, 2 * D_EXPERT), lambda i, be, nu: (be[i], 0, 0)),
                      pl.BlockSpec((None, D_EXPERT, D), lambda i, be, nu: (be[i], 0, 0))],
            out_specs=pl.BlockSpec((RB, Dh), row_map),
            scratch_shapes=[pltpu.VMEM((D, 2 * D_EXPERT), BF16), pltpu.VMEM((D_EXPERT, D), BF16)]),
        compiler_params=_cparams("arbitrary"),
        name="experts",
    )(block_e, n_used, xs, wgu, wd)


def _combine_kernel(src_ref, dst_ref, n_ref, x1_ref, rows_ref, w_ref, mod_ref, g_ref, ys_ref, o_ref, ybuf, sems):
    T = x1_ref.shape[0]
    step = pl.program_id(0)
    n_steps = pl.num_programs(0)
    slot = step % 2

    def strip(tile_slot):
        def make(loc, dst, n):
            return pltpu.make_async_copy(ys_ref.at[pl.ds(dst, n)], ybuf.at[tile_slot, pl.ds(loc, n)],
                                         sems.at[tile_slot])
        return make

    @pl.when(step == 0)
    def _():
        ybuf[...] = jnp.zeros(ybuf.shape, ybuf.dtype)
        _for_each_strip(step, src_ref, dst_ref, n_ref, lambda *a: strip(slot)(*a).start())

    @pl.when(step + 1 < n_steps)
    def _():
        _for_each_strip(step + 1, src_ref, dst_ref, n_ref, lambda *a: strip(1 - slot)(*a).start())

    _for_each_strip(step, src_ref, dst_ref, n_ref, lambda *a: strip(slot)(*a).wait())

    c = lax.broadcasted_iota(I32, (T, LOCAL_ROWS), 1)
    rows = rows_ref[...]
    w = w_ref[...]
    permw = jnp.where(c == rows[:, 0:1], w[:, 0:1], jnp.where(c == rows[:, 1:2], w[:, 1:2], 0.0)).astype(BF16)
    lo, hi = _unpack_bf16_pair(ybuf[slot])
    moe = jnp.concatenate([jnp.dot(permw, lo, preferred_element_type=F32),
                           jnp.dot(permw, hi, preferred_element_type=F32)], axis=1)
    gate_f = mod_ref[5]
    o_ref[...] = _rms(x1_ref[...] + gate_f * moe, g_ref[...])


def _combine_call(strip_src, strip_dst, strip_n, x1, rows_tok, w_tok, mod4, final_g, ys, T, tiles_per_batch):
    N, D = x1.shape
    return pl.pallas_call(
        _combine_kernel,
        out_shape=jax.ShapeDtypeStruct((N, D), F32),
        grid_spec=pltpu.PrefetchScalarGridSpec(
            num_scalar_prefetch=3,
            grid=(N // T,),
            in_specs=[pl.BlockSpec((T, D), lambda i, *_: (i, 0)),
                      pl.BlockSpec((T, 2), lambda i, *_: (i, 0)),
                      pl.BlockSpec((T, 2), lambda i, *_: (i, 0)),
                      pl.BlockSpec((None, N_MOD, 1, D), lambda i, *_: (i // tiles_per_batch, 0, 0, 0)),
                      pl.BlockSpec((1, D), lambda i, *_: (0, 0)),
                      pl.BlockSpec(memory_space=pl.ANY)],
            out_specs=pl.BlockSpec((T, D), lambda i, *_: (i, 0)),
            scratch_shapes=[pltpu.VMEM((2, LOCAL_ROWS, D // 2), U32), pltpu.SemaphoreType.DMA((2,))]),
        compiler_params=_cparams("arbitrary"),
        name="combine",
    )(strip_src, strip_dst, strip_n, x1, rows_tok, w_tok, mod4, final_g, ys)


def _round_up(v, m):
    return (v + m - 1) // m * m


def kernel(x, c, positions, w_mod, b_mod, norm_mix_g, w_in, w_pool, pool_scale, q_norm_g, w_uq, kv_norm_g, w_ukv, w_o, norm_ffn_g, w_group, b_group, w_router, b_router, w_gate_up, w_down, final_g):
    B, S, D = x.shape
    N = B * S
    depth = w_mod.shape[0]
    T = ROW_TILE
    RB = EXPERT_ROWS
    assert depth == 1, "the final RMSNorm is fused into the layer's combine step"
    assert S % T == 0 and S % min(ATTN_TQ, S) == 0 and min(ATTN_TQ, S) % ATTN_TK == 0
    tiles_per_batch = S // T
    n_tiles = N // T
    nH = MLA_HEADS
    l = 0

    inv_freq = ROPE_THETA ** (-(jnp.arange(0, QK_ROPE_DIM, 2, dtype=F32) / QK_ROPE_DIM))
    posr = positions.reshape(B, 1, S)
    cut1 = POOL_WIDTH
    cut2 = cut1 + Q_LORA_RANK
    cut3 = cut2 + KV_LORA_RANK

    mod4 = _mod_call(c, w_mod[l], b_mod[l]).reshape(B, N_MOD, 1, D)

    wi = w_in[l]
    w_in_main = wi[:, :cut3].astype(BF16)
    w_krT = wi[:, cut3:].T.astype(BF16)
    wq = w_uq[l].reshape(Q_LORA_RANK, nH, QK_HEAD_DIM)
    wq_n = wq[:, :, :QK_NOPE_DIM].reshape(Q_LORA_RANK, nH * QK_NOPE_DIM)
    wq_r = wq[:, :, QK_NOPE_DIM:]
    w_uqT = jnp.concatenate([wq_n, wq_r.reshape(Q_LORA_RANK, nH * QK_ROPE_DIM)], axis=1).T.astype(BF16)
    wkv = w_ukv[l].reshape(KV_LORA_RANK, nH, QK_NOPE_DIM + V_HEAD_DIM)
    w_uk = wkv[:, :, :QK_NOPE_DIM].reshape(KV_LORA_RANK, nH * QK_NOPE_DIM).astype(BF16)
    w_uvT = wkv[:, :, QK_NOPE_DIM:].reshape(KV_LORA_RANK, nH * V_HEAD_DIM).T.astype(BF16)
    qg = (q_norm_g[l] * (QK_HEAD_DIM ** -0.5 * LOG2_E)).reshape(1, Q_LORA_RANK)
    wpool_bd = jnp.zeros((POOL_WIDTH, POOL_WIDTH), F32)
    for g in range(len(POOL_WINDOWS)):
        sl = slice(g * POOL_GROUP_DIM, (g + 1) * POOL_GROUP_DIM)
        wpool_bd = wpool_bd.at[sl, sl].set(w_pool[l, g])
    wpool_bd = wpool_bd.astype(BF16)

    qT, k, vT, yp = _pre_call(
        x, posr, mod4, norm_mix_g[l].reshape(1, D), w_in_main, w_krT, inv_freq.reshape(QK_ROPE_DIM // 2, 1),
        wpool_bd, pool_scale[l].reshape(1, POOL_WIDTH), qg, w_uqT, kv_norm_g[l].reshape(1, KV_LORA_RANK),
        w_uk, w_uvT, T)
    ya = _attn_call(qT, k, vT)

    wo = w_o[l].astype(BF16)
    wgT = jnp.zeros((8, D), F32).at[:N_GROUPS].set(w_group[l].T).astype(BF16)
    bg = jnp.zeros((8, 1), F32).at[:N_GROUPS, 0].set(b_group[l])
    x1, h2, meta_i, meta_w, tile_cnt = _post_call(
        x.reshape(N, D), yp.reshape(N, POOL_WIDTH), ya.reshape(N, nH * V_HEAD_DIM), mod4,
        wo[:POOL_WIDTH], wo[POOL_WIDTH:], norm_ffn_g[l].reshape(1, D),
        wgT, bg, w_router[l].T.astype(BF16), b_router[l].reshape(N_EXPERTS, 1), T, tiles_per_batch)

    run_len = _round_up(tile_cnt[:, :, 0], SUBLANES)
    run_loc = jnp.cumsum(run_len, axis=1) - run_len
    seg_rows = jnp.sum(run_len, axis=0)
    seg_len = _round_up(seg_rows, RB)
    pends = jnp.cumsum(seg_len).astype(I32)
    pstarts = pends - seg_len
    run_dst = pstarts[None, :] + jnp.cumsum(run_len, axis=0) - run_len
    pad_len = (seg_len - seg_rows).astype(I32)
    pad_dst = (pstarts + seg_rows).astype(I32)
    bits = jnp.arange(STRIP_BITS, dtype=I32)
    size = SUBLANES << bits
    valid = (run_len[:, None, :] & size[None, :, None]) != 0
    done = run_len[:, None, :] & ~(2 * size[None, :, None] - 1)
    flat = lambda a: a.reshape(n_tiles, STRIP_BITS * N_EXPERTS)
    valid_f = flat(valid)
    pos = jnp.cumsum(valid_f.astype(I32), axis=1) - valid_f.astype(I32)
    pick = valid_f[:, None, :] & (pos[:, None, :] == jnp.arange(MAX_STRIPS, dtype=I32)[None, :, None])
    gather = lambda a: jnp.sum(jnp.where(pick, flat(a)[:, None, :], 0), axis=-1).reshape(-1).astype(I32)
    strip_src = gather(run_loc[:, None, :] + done)
    strip_dst = gather(run_dst[:, None, :] + done)
    strip_n = jnp.sum(valid, axis=2).reshape(-1).astype(I32)
    P_pad = _round_up(2 * N + n_tiles * N_EXPERTS * (SUBLANES - 1), RB) + N_EXPERTS * RB
    n_rb = P_pad // RB
    n_used = (pends[-1:] // RB).astype(I32)
    block_start = jnp.minimum(jnp.arange(n_rb, dtype=I32), n_used - 1) * RB
    block_e = jnp.sum((pends[None, :] <= block_start[:, None]).astype(I32), axis=1)

    xs = _dispatch_call(strip_src, strip_dst, strip_n, pends, pad_len, pad_dst, meta_i, h2, P_pad, T, RB)
    ys = _expert_call(block_e, n_used, xs, w_gate_up[l], w_down[l], RB)
    out = _combine_call(strip_src, strip_dst, strip_n, x1, meta_i[2:4].T, meta_w.T, mod4, final_g.reshape(1, D), ys,
                        T, tiles_per_batch)
    return out.reshape(B, S, D)
```

```python
import jax
import jax.numpy as jnp
from jax import lax
from jax.experimental import pallas as pl
from jax.experimental.pallas import tpu as pltpu

F32 = jnp.float32
BF16 = jnp.bfloat16
U32 = jnp.uint32
I32 = jnp.int32

POOL_WINDOWS = (2, 4, 8, 16)
POOL_GROUP_DIM = 64
POOL_WIDTH = 256
MLA_HEADS = 6
QK_NOPE_DIM = 128
QK_ROPE_DIM = 64
QK_HEAD_DIM = QK_NOPE_DIM + QK_ROPE_DIM
V_HEAD_DIM = 128
V_ROWS = 144
Q_LORA_RANK = 512
KV_LORA_RANK = 256
ROPE_THETA = 10000.0
N_GROUPS = 4
EXPERTS_PER_GROUP = 8
N_EXPERTS = N_GROUPS * EXPERTS_PER_GROUP
D_EXPERT = 256
N_MOD = 6
EPS = 1e-6

SUBLANES = 8
POOL_HALO = 32
ROW_TILE = 512
PRE_TILE = 1024
ATTN_TQ = 2048
ATTN_TK = 512
EXPERT_ROWS = 512
LOCAL_ROWS = 2 * ROW_TILE + 256
assert LOCAL_ROWS >= 2 * ROW_TILE + N_EXPERTS * (SUBLANES - 1)
STRIP_BITS = (2 * ROW_TILE // SUBLANES).bit_length()
MAX_STRIPS = 128
assert MAX_STRIPS >= 2 * N_EXPERTS + (LOCAL_ROWS // SUBLANES - 3 * N_EXPERTS) // 4
DISPATCH_SLOTS = 3
VMEM_LIMIT = 56 * 1024 * 1024
NEG_BIG = -1e30
LOG2_E = 1.4426950408889634
HI_MASK = 0xFFFF0000


def _cparams(*sem):
    return pltpu.CompilerParams(dimension_semantics=sem, vmem_limit_bytes=VMEM_LIMIT)


def _rms(x, g):
    return x * lax.rsqrt(jnp.mean(x * x, axis=-1, keepdims=True) + EPS) * g


def _pack_bf16_pair(lo, hi):
    return lax.bitcast_convert_type(hi, U32) | (lax.bitcast_convert_type(lo, U32) >> 16)


def _unpack_bf16_pair(w):
    lo = lax.bitcast_convert_type(w << 16, F32).astype(BF16)
    hi = lax.bitcast_convert_type(w & jnp.uint32(HI_MASK), F32).astype(BF16)
    return lo, hi


def _mod_kernel(c_ref, w_ref, b_ref, o_ref):
    c = c_ref[...]
    ca = c / (1.0 + jnp.exp(-c))
    o_ref[...] = jnp.dot(ca, w_ref[...], precision=lax.Precision.HIGHEST,
                         preferred_element_type=F32) + b_ref[...]


def _mod_call(c, w_mod, b_mod):
    B, D = c.shape
    n_out = w_mod.shape[1]
    tn = 512
    return pl.pallas_call(
        _mod_kernel,
        out_shape=jax.ShapeDtypeStruct((B, n_out), F32),
        grid=(n_out // tn,),
        in_specs=[pl.BlockSpec((B, D), lambda j: (0, 0)),
                  pl.BlockSpec((D, tn), lambda j: (0, j)),
                  pl.BlockSpec((1, tn), lambda j: (0, j))],
        out_specs=pl.BlockSpec((B, tn), lambda j: (0, j)),
        compiler_params=_cparams("arbitrary"),
        name="mod",
    )(c, w_mod, b_mod.reshape(1, n_out))


def _pre_kernel(x_ref, posr_ref, mod_ref, g_ref, win_ref, wkrT_ref, invfc_ref, wpool_ref, pscale_ref,
                qg_ref, wuqT_ref, kvg_ref, wuk_ref, wuvT_ref,
                qT_ref, k_ref, vT_ref, yp_ref,
                pbuf, b2, b4, b8):
    T = x_ref.shape[0]
    H = POOL_HALO
    i = pl.program_id(1)
    nt = (((1,), (1,)), ((), ()))

    shift = mod_ref[0]
    scale = mod_ref[1]
    hb = (_rms(x_ref[...], g_ref[...]) * (1.0 + scale) + shift).astype(BF16)
    u = jnp.dot(hb, win_ref[...], preferred_element_type=F32)
    krT = lax.dot_general(wkrT_ref[...], hb, nt, preferred_element_type=F32)

    p = u[:, :POOL_WIDTH]

    @pl.when(i == 0)
    def _():
        pbuf[0:H, :] = jnp.zeros((H, POOL_WIDTH), F32)

    pbuf[H:H + T, :] = p
    b2[8:T + H, :] = pbuf[8:T + H, :] + pbuf[7:T + H - 1, :]
    b4[16:T + H, :] = b2[16:T + H, :] + b2[14:T + H - 2, :]
    b8[24:T + H, :] = b4[24:T + H, :] + b4[20:T + H - 4, :]
    s2 = b2[H:T + H, :]
    s4 = b4[H:T + H, :]
    s8 = b8[H:T + H, :]
    s16 = b8[H:T + H, :] + b8[H - 8:T + H - 8, :]
    pbuf[0:H, :] = pbuf[T:T + H, :]

    lane = lax.broadcasted_iota(I32, (T, POOL_WIDTH), 1)
    t1 = (lax.broadcasted_iota(I32, (T, 1), 0) + (i * T + 1)).astype(F32)
    inv2 = 1.0 / jnp.minimum(t1, 2.0)
    inv4 = 1.0 / jnp.minimum(t1, 4.0)
    inv8 = 1.0 / jnp.minimum(t1, 8.0)
    inv16 = 1.0 / jnp.minimum(t1, 16.0)
    mean = jnp.where(lane < 64, s2 * inv2,
                     jnp.where(lane < 128, s4 * inv4,
                               jnp.where(lane < 192, s8 * inv8, s16 * inv16)))
    pooled = mean - p
    yp = jnp.dot(pooled.astype(BF16), wpool_ref[...], preferred_element_type=F32) * pscale_ref[...]
    yp_ref[...] = yp.astype(yp_ref.dtype)

    ang = invfc_ref[...] * posr_ref[...].astype(F32)
    cos_h = jnp.cos(ang)
    sin_h = jnp.sin(ang)
    hr = QK_ROPE_DIM // 2

    def rope(xt):
        x1, x2 = xt[0:hr], xt[hr:]
        return x1 * cos_h - x2 * sin_h, x2 * cos_h + x1 * sin_h

    k_rope = jnp.concatenate(rope(krT), axis=0).T.astype(BF16)

    cq = u[:, POOL_WIDTH:POOL_WIDTH + Q_LORA_RANK]
    ckv = u[:, POOL_WIDTH + Q_LORA_RANK:POOL_WIDTH + Q_LORA_RANK + KV_LORA_RANK]
    cqn = _rms(cq, qg_ref[...]).astype(BF16)
    ckvn = _rms(ckv, kvg_ref[...]).astype(BF16)
    qaT = lax.dot_general(wuqT_ref[...], cqn, nt, preferred_element_type=F32)
    kn = jnp.dot(ckvn, wuk_ref[...], preferred_element_type=F32)
    vT = lax.dot_general(wuvT_ref[...], ckvn, nt, preferred_element_type=F32)
    nq = MLA_HEADS * QK_NOPE_DIM
    ones_rows = jnp.where(lax.broadcasted_iota(I32, (V_ROWS - V_HEAD_DIM, T), 0) == 0, 1.0, 0.0).astype(BF16)
    for hd in range(MLA_HEADS):
        q1, q2 = rope(qaT[nq + hd * QK_ROPE_DIM:nq + (hd + 1) * QK_ROPE_DIM, :])
        qT_ref[hd, 0:QK_NOPE_DIM, :] = qaT[hd * QK_NOPE_DIM:(hd + 1) * QK_NOPE_DIM, :].astype(BF16)
        qT_ref[hd, QK_NOPE_DIM:QK_NOPE_DIM + hr, :] = q1.astype(BF16)
        qT_ref[hd, QK_NOPE_DIM + hr:QK_HEAD_DIM, :] = q2.astype(BF16)
        k_ref[hd, :, 0:QK_NOPE_DIM] = kn[:, hd * QK_NOPE_DIM:(hd + 1) * QK_NOPE_DIM].astype(BF16)
        k_ref[hd, :, QK_NOPE_DIM:QK_HEAD_DIM] = k_rope
        vT_ref[hd, 0:V_HEAD_DIM, :] = vT[hd * V_HEAD_DIM:(hd + 1) * V_HEAD_DIM, :].astype(BF16)
        vT_ref[hd, V_HEAD_DIM:V_ROWS, :] = ones_rows


def _pre_call(x, posr, mod4, norm_g, w_in_main, w_krT, inv_freq_col, wpool_bd, pool_scale, qg, w_uqT, kvg,
              w_uk, w_uvT, T):
    B, S, D = x.shape
    nH = MLA_HEADS
    const = lambda shape: pl.BlockSpec(shape, lambda b, i: (0,) * len(shape))
    return pl.pallas_call(
        _pre_kernel,
        out_shape=(jax.ShapeDtypeStruct((B, nH, QK_HEAD_DIM, S), BF16),
                   jax.ShapeDtypeStruct((B, nH, S, QK_HEAD_DIM), BF16),
                   jax.ShapeDtypeStruct((B, nH, V_ROWS, S), BF16),
                   jax.ShapeDtypeStruct((B, S, POOL_WIDTH), BF16)),
        grid=(B, S // T),
        in_specs=[pl.BlockSpec((None, T, D), lambda b, i: (b, i, 0)),
                  pl.BlockSpec((None, 1, T), lambda b, i: (b, 0, i)),
                  pl.BlockSpec((None, N_MOD, 1, D), lambda b, i: (b, 0, 0, 0)),
                  const((1, D)),
                  const(w_in_main.shape),
                  const(w_krT.shape),
                  const(inv_freq_col.shape),
                  const(wpool_bd.shape),
                  const((1, POOL_WIDTH)),
                  const((1, Q_LORA_RANK)),
                  const(w_uqT.shape),
                  const((1, KV_LORA_RANK)),
                  const(w_uk.shape),
                  const(w_uvT.shape)],
        out_specs=(pl.BlockSpec((None, nH, QK_HEAD_DIM, T), lambda b, i: (b, 0, 0, i)),
                   pl.BlockSpec((None, nH, T, QK_HEAD_DIM), lambda b, i: (b, 0, i, 0)),
                   pl.BlockSpec((None, nH, V_ROWS, T), lambda b, i: (b, 0, 0, i)),
                   pl.BlockSpec((None, T, POOL_WIDTH), lambda b, i: (b, i, 0))),
        scratch_shapes=[pltpu.VMEM((T + POOL_HALO, POOL_WIDTH), F32)] * 4,
        compiler_params=_cparams("arbitrary", "arbitrary"),
        name="pre",
    )(x, posr, mod4, norm_g, w_in_main, w_krT, inv_freq_col, wpool_bd, pool_scale, qg, w_uqT, kvg, w_uk, w_uvT)


def _attn_kernel(qT_ref, k_ref, vT_ref, o_ref):
    S = k_ref.shape[0]
    tq, tk = min(ATTN_TQ, S), ATTN_TK
    diag = lax.broadcasted_iota(I32, (tk, tk), 0) <= lax.broadcasted_iota(I32, (tk, tk), 1)
    blocks = [(i, j) for i in range(S // tq) for j in range((i + 1) * (tq // tk))]

    def scores(i, j):
        d = max(j * tk - i * tq, 0)
        sT = jnp.dot(k_ref[j * tk:(j + 1) * tk, :], qT_ref[:, i * tq + d:(i + 1) * tq],
                     preferred_element_type=F32)
        if j * tk >= i * tq:
            masked = jnp.where(diag, sT[:, :tk], NEG_BIG)
            sT = masked if sT.shape[1] == tk else jnp.concatenate([masked, sT[:, tk:]], axis=1)
        return sT

    s_next = scores(*blocks[0])
    m = acc = None
    for n, (i, j) in enumerate(blocks):
        sT = s_next
        if n + 1 < len(blocks):
            s_next = scores(*blocks[n + 1])
        if j == 0:
            m = jnp.full((1, tq), NEG_BIG, F32)
            acc = jnp.zeros((V_ROWS, tq), F32)
        d = max(j * tk - i * tq, 0)
        vT = vT_ref[:, j * tk:(j + 1) * tk]
        m_new = jnp.maximum(m[:, d:], jnp.max(sT, axis=0, keepdims=True))
        pT = jnp.exp2(sT - m_new).astype(BF16)
        acc_new = jnp.exp2(m[:, d:] - m_new) * acc[:, d:] + jnp.dot(vT, pT, preferred_element_type=F32)
        if d:
            m_new = jnp.concatenate([m[:, :d], m_new], axis=1)
            acc_new = jnp.concatenate([acc[:, :d], acc_new], axis=1)
        m, acc = m_new, acc_new
        if j == (i + 1) * (tq // tk) - 1:
            out = acc[0:V_HEAD_DIM] * (1.0 / acc[V_HEAD_DIM:V_HEAD_DIM + 1])
            o_ref[i * tq:(i + 1) * tq, :] = out.T.astype(o_ref.dtype)


def _attn_call(qT, k, vT):
    B, nH, S, _ = k.shape
    return pl.pallas_call(
        _attn_kernel,
        out_shape=jax.ShapeDtypeStruct((B, S, nH * V_HEAD_DIM), BF16),
        grid=(B, nH),
        in_specs=[pl.BlockSpec((None, None, QK_HEAD_DIM, S), lambda b, h: (b, h, 0, 0)),
                  pl.BlockSpec((None, None, S, QK_HEAD_DIM), lambda b, h: (b, h, 0, 0)),
                  pl.BlockSpec((None, None, V_ROWS, S), lambda b, h: (b, h, 0, 0))],
        out_specs=pl.BlockSpec((None, S, V_HEAD_DIM), lambda b, h: (b, 0, h)),
        compiler_params=_cparams("arbitrary", "arbitrary"),
        name="attn",
    )(qT, k, vT)


def _post_kernel(x_ref, yp_ref, ya_ref, mod_ref, wo_p_ref, wo_a_ref, g_ref,
                 wgT_ref, bg_ref, wrT_ref, br_ref,
                 x1_ref, h2_ref, rows_ref, mf_ref, cnt_ref):
    T = x_ref.shape[0]

    gate_a = mod_ref[2]
    shift_f = mod_ref[3]
    scale_f = mod_ref[4]
    mix = (jnp.dot(yp_ref[...], wo_p_ref[...], preferred_element_type=F32)
           + jnp.dot(ya_ref[...], wo_a_ref[...], preferred_element_type=F32))
    x1 = x_ref[...] + gate_a * mix
    x1_ref[...] = x1
    hb = (_rms(x1, g_ref[...]) * (1.0 + scale_f) + shift_f).astype(BF16)
    h2_ref[...] = hb

    nt = (((1,), (1,)), ((), ()))
    gl = lax.dot_general(wgT_ref[...], hb, nt, preferred_element_type=F32)
    el = lax.dot_general(wrT_ref[...], hb, nt, preferred_element_type=F32)

    r8 = lax.broadcasted_iota(I32, (8, T), 0)
    gvalid = r8 < N_GROUPS
    gmax = jnp.max(jnp.where(gvalid, gl, NEG_BIG), axis=0, keepdims=True)
    gexp = jnp.where(gvalid, jnp.exp(gl - gmax), 0.0)
    g_prob = gexp / jnp.sum(gexp, axis=0, keepdims=True)
    gb = jnp.where(gvalid, gl + bg_ref[...], NEG_BIG)
    gbmax = jnp.max(gb, axis=0, keepdims=True)
    g_sel = jnp.min(jnp.where(gb == gbmax, r8, 8), axis=0, keepdims=True)
    gp = jnp.sum(jnp.where(r8 == g_sel, g_prob, 0.0), axis=0, keepdims=True)

    e_in = jnp.zeros((EXPERTS_PER_GROUP, T), F32)
    b_in = jnp.zeros((EXPERTS_PER_GROUP, T), F32)
    br = br_ref[...]
    for g in range(N_GROUPS):
        sel = g_sel == g
        e_in = jnp.where(sel, el[g * 8:(g + 1) * 8, :], e_in)
        b_in = jnp.where(sel, br[g * 8:(g + 1) * 8, :], b_in)
    eb = e_in + b_in
    m1 = jnp.max(eb, axis=0, keepdims=True)
    i1 = jnp.min(jnp.where(eb == m1, r8, 8), axis=0, keepdims=True)
    eb2 = jnp.where(r8 == i1, NEG_BIG, eb)
    m2 = jnp.max(eb2, axis=0, keepdims=True)
    i2 = jnp.min(jnp.where(eb2 == m2, r8, 8), axis=0, keepdims=True)
    emax = jnp.max(e_in, axis=0, keepdims=True)
    eexp = jnp.exp(e_in - emax)
    sp = eexp / jnp.sum(eexp, axis=0, keepdims=True)
    p1 = jnp.sum(jnp.where(r8 == i1, sp, 0.0), axis=0, keepdims=True)
    p2 = jnp.sum(jnp.where(r8 == i2, sp, 0.0), axis=0, keepdims=True)
    tot = p1 + p2
    w1 = gp * (p1 / tot)
    w2 = gp * (p2 / tot)
    e1 = g_sel * EXPERTS_PER_GROUP + i1
    e2 = g_sel * EXPERTS_PER_GROUP + i2

    r32 = lax.broadcasted_iota(I32, (N_EXPERTS, T), 0)
    oh1 = r32 == e1
    oh2 = r32 == e2
    oh = jnp.where(oh1 | oh2, 1.0, 0.0)
    upper = jnp.where(lax.broadcasted_iota(I32, (T, T), 0) < lax.broadcasted_iota(I32, (T, T), 1),
                      1.0, 0.0).astype(BF16)
    before = jnp.dot(oh.astype(BF16), upper, preferred_element_type=F32)
    cnt = jnp.sum(oh, axis=1, keepdims=True)
    run8 = jnp.floor((cnt + (SUBLANES - 1.0)) * (1.0 / SUBLANES))
    lower = jnp.where(lax.broadcasted_iota(I32, (N_EXPERTS, N_EXPERTS), 1)
                      < lax.broadcasted_iota(I32, (N_EXPERTS, N_EXPERTS), 0), 1.0, 0.0).astype(BF16)
    run_start = jnp.dot(lower, jnp.broadcast_to(run8, (N_EXPERTS, 128)).astype(BF16),
                        preferred_element_type=F32)[:, 0:1] * float(SUBLANES)
    pos = before + run_start
    row1 = jnp.sum(jnp.where(oh1, pos, 0.0), axis=0, keepdims=True)
    row2 = jnp.sum(jnp.where(oh2, pos, 0.0), axis=0, keepdims=True)
    cnt_ref[...] = cnt.astype(I32)

    rows_ref[0:1, :] = row1.astype(I32)
    rows_ref[1:2, :] = row2.astype(I32)
    mf_ref[0:1, :] = row1
    mf_ref[1:2, :] = row2
    mf_ref[2:3, :] = w1
    mf_ref[3:4, :] = w2


def _post_call(x2, yp2, ya2, mod4, wo_p, wo_a, g, wgT, bg, wrT, br, T, tiles_per_batch):
    N, D = x2.shape
    const = lambda shape: pl.BlockSpec(shape, lambda i: (0,) * len(shape))
    return pl.pallas_call(
        _post_kernel,
        out_shape=(jax.ShapeDtypeStruct((N, D), F32),
                   jax.ShapeDtypeStruct((N, D), BF16),
                   jax.ShapeDtypeStruct((2, N), I32),
                   jax.ShapeDtypeStruct((4, N), F32),
                   jax.ShapeDtypeStruct((N // T, N_EXPERTS, 1), I32)),
        grid=(N // T,),
        in_specs=[pl.BlockSpec((T, D), lambda i: (i, 0)),
                  pl.BlockSpec((T, POOL_WIDTH), lambda i: (i, 0)),
                  pl.BlockSpec((T, MLA_HEADS * V_HEAD_DIM), lambda i: (i, 0)),
                  pl.BlockSpec((None, N_MOD, 1, D), lambda i: (i // tiles_per_batch, 0, 0, 0)),
                  const(wo_p.shape), const(wo_a.shape), const((1, D)),
                  const(wgT.shape), const(bg.shape), const(wrT.shape), const(br.shape)],
        out_specs=(pl.BlockSpec((T, D), lambda i: (i, 0)),
                   pl.BlockSpec((T, D), lambda i: (i, 0)),
                   pl.BlockSpec((2, T), lambda i: (0, i)),
                   pl.BlockSpec((4, T), lambda i: (0, i)),
                   pl.BlockSpec((None, N_EXPERTS, 1), lambda i: (i, 0, 0))),
        compiler_params=_cparams("arbitrary"),
        name="post",
    )(x2, yp2, ya2, mod4, wo_p, wo_a, g, wgT, bg, wrT, br)


def _for_each_strip(tile, src_ref, dst_ref, n_ref, fn):
    first = tile * MAX_STRIPS
    for b in range(STRIP_BITS):
        last = first + n_ref[tile * STRIP_BITS + b]

        def body(r, c, n=SUBLANES << b):
            fn(pl.multiple_of(src_ref[r], SUBLANES), pl.multiple_of(dst_ref[r], SUBLANES), n)
            return c

        lax.fori_loop(first, last, body, 0)
        first = last


def _dispatch_kernel(src_ref, dst_ref, n_ref, pends_ref, padlen_ref, paddst_ref, rows_ref, h_ref, xs_ref,
                     lbuf, zbuf, sems, zsem):
    T = h_ref.shape[0]
    RB = zbuf.shape[0]
    half = h_ref.shape[1] // 2
    step = pl.program_id(0)
    n_steps = pl.num_programs(0)
    slot = step % DISPATCH_SLOTS

    def zero_fill(act):
        def per_expert(e, c):
            pad = padlen_ref[e]
            dst = paddst_ref[e]
            for b in reversed(range((RB // SUBLANES).bit_length() - 1)):
                n = SUBLANES << b
                done = pad & ~(2 * n - 1)

                @pl.when((pad & n) != 0)
                def _():
                    act(pltpu.make_async_copy(zbuf.at[pl.ds(0, n)],
                                              xs_ref.at[pl.ds(pl.multiple_of(dst + done, SUBLANES), n)], zsem))
            return c

        lax.fori_loop(0, N_EXPERTS, per_expert, 0)

        def per_block(b, c):
            act(pltpu.make_async_copy(zbuf, xs_ref.at[pl.ds(pl.multiple_of(b * RB, RB), RB)], zsem))
            return c

        lax.fori_loop(pends_ref[N_EXPERTS - 1] // RB, xs_ref.shape[0] // RB, per_block, 0)

    @pl.when(step == 0)
    def _():
        zbuf[...] = jnp.zeros(zbuf.shape, zbuf.dtype)
        zero_fill(lambda cp: cp.start())

    r = lax.broadcasted_iota(I32, (LOCAL_ROWS, T), 0)
    perm = jnp.where((r == rows_ref[0:1, :]) | (r == rows_ref[1:2, :]), 1.0, 0.0).astype(BF16)
    h = h_ref[...]
    lo = jnp.dot(perm, h[:, :half], preferred_element_type=F32)
    hi = jnp.dot(perm, h[:, half:], preferred_element_type=F32)
    lbuf[slot] = _pack_bf16_pair(lo, hi)

    def strip(tile_slot):
        def make(loc, dst, n):
            return pltpu.make_async_copy(lbuf.at[tile_slot, pl.ds(loc, n)], xs_ref.at[pl.ds(dst, n)],
                                         sems.at[tile_slot])
        return make

    def wait_tile(tile):
        _for_each_strip(tile, src_ref, dst_ref, n_ref, lambda *a: strip(tile % DISPATCH_SLOTS)(*a).wait())

    _for_each_strip(step, src_ref, dst_ref, n_ref, lambda *a: strip(slot)(*a).start())

    @pl.when(step >= DISPATCH_SLOTS - 1)
    def _():
        wait_tile(step - (DISPATCH_SLOTS - 1))

    @pl.when(step == n_steps - 1)
    def _():
        for back in reversed(range(DISPATCH_SLOTS - 1)):
            @pl.when(step >= back)
            def _():
                wait_tile(step - back)
        zero_fill(lambda cp: cp.wait())


def _dispatch_call(strip_src, strip_dst, strip_n, pends, pad_len, pad_dst, rows, h2, P_pad, T, RB):
    N, D = h2.shape
    return pl.pallas_call(
        _dispatch_kernel,
        out_shape=jax.ShapeDtypeStruct((P_pad, D // 2), U32),
        grid_spec=pltpu.PrefetchScalarGridSpec(
            num_scalar_prefetch=6,
            grid=(N // T,),
            in_specs=[pl.BlockSpec((2, T), lambda i, *_: (0, i)),
                      pl.BlockSpec((T, D), lambda i, *_: (i, 0))],
            out_specs=pl.BlockSpec(memory_space=pl.ANY),
            scratch_shapes=[pltpu.VMEM((DISPATCH_SLOTS, LOCAL_ROWS, D // 2), U32),
                            pltpu.VMEM((RB, D // 2), U32),
                            pltpu.SemaphoreType.DMA((DISPATCH_SLOTS,)),
                            pltpu.SemaphoreType.DMA]),
        compiler_params=_cparams("arbitrary"),
        name="dispatch",
    )(strip_src, strip_dst, strip_n, pends, pad_len, pad_dst, rows, h2)


def _expert_kernel(be_ref, nused_ref, xs_ref, wgu_ref, wd_ref, ys_ref, wgu_bf, wd_bf):
    i = pl.program_id(0)
    used = i < nused_ref[0]
    half = wgu_ref.shape[0] // 2

    @pl.when(used & ((i == 0) | (be_ref[i] != be_ref[jnp.maximum(i - 1, 0)])))
    def _():
        wgu_bf[...] = wgu_ref[...].astype(BF16)
        wd_bf[...] = wd_ref[...].astype(BF16)

    @pl.when(used)
    def _():
        lo, hi = _unpack_bf16_pair(xs_ref[...])
        gu = (jnp.dot(lo, wgu_bf[0:half, :], preferred_element_type=F32)
              + jnp.dot(hi, wgu_bf[half:, :], preferred_element_type=F32))
        gate = gu[:, :D_EXPERT]
        up = gu[:, D_EXPERT:]
        act = gate / (1.0 + jnp.exp(-gate)) * up
        y = jnp.dot(act.astype(BF16), wd_bf[...], preferred_element_type=F32)
        yb = y.astype(BF16).astype(F32)
        ys_ref[...] = _pack_bf16_pair(yb[:, :half], yb[:, half:])

    @pl.when(jnp.logical_not(used))
    def _():
        ys_ref[...] = jnp.zeros(ys_ref.shape, ys_ref.dtype)


def _expert_call(block_e, n_used, xs, wgu, wd, RB):
    P, Dh = xs.shape
    D = 2 * Dh
    row_map = lambda i, be, nu: (i, 0)
    in_map = lambda i, be, nu: (jnp.minimum(i, nu[0] - 1), 0)
    return pl.pallas_call(
        _expert_kernel,
        out_shape=jax.ShapeDtypeStruct((P, Dh), U32),
        grid_spec=pltpu.PrefetchScalarGridSpec(
            num_scalar_prefetch=2,
            grid=(P // RB,),
            in_specs=[pl.BlockSpec((RB, Dh), in_map),
                      pl.BlockSpec((None, D, 2 * D_EXPERT), lambda i, be, nu: (be[i], 0, 0)),
                      pl.BlockSpec((None, D_EXPERT, D), lambda i, be, nu: (be[i], 0, 0))],
            out_specs=pl.BlockSpec((RB, Dh), row_map),
            scratch_shapes=[pltpu.VMEM((D, 2 * D_EXPERT), BF16), pltpu.VMEM((D_EXPERT, D), BF16)]),
        compiler_params=_cparams("arbitrary"),
        name="experts",
    )(block_e, n_used, xs, wgu, wd)


def _combine_kernel(src_ref, dst_ref, n_ref, x1_ref, tok_ref, mod_ref, g_ref, ys_ref, o_ref, ybuf, sems):
    T = x1_ref.shape[0]
    step = pl.program_id(0)
    n_steps = pl.num_programs(0)
    slot = step % 2

    def strip(tile_slot):
        def make(loc, dst, n):
            return pltpu.make_async_copy(ys_ref.at[pl.ds(dst, n)], ybuf.at[tile_slot, pl.ds(loc, n)],
                                         sems.at[tile_slot])
        return make

    @pl.when(step == 0)
    def _():
        ybuf[...] = jnp.zeros(ybuf.shape, ybuf.dtype)
        _for_each_strip(step, src_ref, dst_ref, n_ref, lambda *a: strip(slot)(*a).start())

    @pl.when(step + 1 < n_steps)
    def _():
        _for_each_strip(step + 1, src_ref, dst_ref, n_ref, lambda *a: strip(1 - slot)(*a).start())

    _for_each_strip(step, src_ref, dst_ref, n_ref, lambda *a: strip(slot)(*a).wait())

    c = lax.broadcasted_iota(I32, (T, LOCAL_ROWS), 1)
    tok = tok_ref[...]
    rows = tok[:, 0:2].astype(I32)
    permw = jnp.where(c == rows[:, 0:1], tok[:, 2:3], jnp.where(c == rows[:, 1:2], tok[:, 3:4], 0.0)).astype(BF16)
    lo, hi = _unpack_bf16_pair(ybuf[slot])
    moe = jnp.concatenate([jnp.dot(permw, lo, preferred_element_type=F32),
                           jnp.dot(permw, hi, preferred_element_type=F32)], axis=1)
    gate_f = mod_ref[5]
    o_ref[...] = _rms(x1_ref[...] + gate_f * moe, g_ref[...])


def _combine_call(strip_src, strip_dst, strip_n, x1, tok, mod4, final_g, ys, T, tiles_per_batch):
    N, D = x1.shape
    return pl.pallas_call(
        _combine_kernel,
        out_shape=jax.ShapeDtypeStruct((N, D), F32),
        grid_spec=pltpu.PrefetchScalarGridSpec(
            num_scalar_prefetch=3,
            grid=(N // T,),
            in_specs=[pl.BlockSpec((T, D), lambda i, *_: (i, 0)),
                      pl.BlockSpec((T, 4), lambda i, *_: (i, 0)),
                      pl.BlockSpec((None, N_MOD, 1, D), lambda i, *_: (i // tiles_per_batch, 0, 0, 0)),
                      pl.BlockSpec((1, D), lambda i, *_: (0, 0)),
                      pl.BlockSpec(memory_space=pl.ANY)],
            out_specs=pl.BlockSpec((T, D), lambda i, *_: (i, 0)),
            scratch_shapes=[pltpu.VMEM((2, LOCAL_ROWS, D // 2), U32), pltpu.SemaphoreType.DMA((2,))]),
        compiler_params=_cparams("arbitrary"),
        name="combine",
    )(strip_src, strip_dst, strip_n, x1, tok, mod4, final_g, ys)


def _round_up(v, m):
    return (v + m - 1) // m * m


def kernel(x, c, positions, w_mod, b_mod, norm_mix_g, w_in, w_pool, pool_scale, q_norm_g, w_uq, kv_norm_g, w_ukv, w_o, norm_ffn_g, w_group, b_group, w_router, b_router, w_gate_up, w_down, final_g):
    B, S, D = x.shape
    N = B * S
    depth = w_mod.shape[0]
    T = ROW_TILE
    RB = EXPERT_ROWS
    assert depth == 1, "the final RMSNorm is fused into the layer's combine step"
    assert S % T == 0 and S % min(PRE_TILE, S) == 0 and S % min(ATTN_TQ, S) == 0 and min(ATTN_TQ, S) % ATTN_TK == 0
    tiles_per_batch = S // T
    n_tiles = N // T
    nH = MLA_HEADS
    l = 0

    inv_freq = ROPE_THETA ** (-(jnp.arange(0, QK_ROPE_DIM, 2, dtype=F32) / QK_ROPE_DIM))
    posr = positions.reshape(B, 1, S)
    cut1 = POOL_WIDTH
    cut2 = cut1 + Q_LORA_RANK
    cut3 = cut2 + KV_LORA_RANK

    mod4 = _mod_call(c, w_mod[l], b_mod[l]).reshape(B, N_MOD, 1, D)

    wi = w_in[l]
    w_in_main = wi[:, :cut3].astype(BF16)
    w_krT = wi[:, cut3:].T.astype(BF16)
    wq = w_uq[l].reshape(Q_LORA_RANK, nH, QK_HEAD_DIM)
    wq_n = wq[:, :, :QK_NOPE_DIM].reshape(Q_LORA_RANK, nH * QK_NOPE_DIM)
    wq_r = wq[:, :, QK_NOPE_DIM:]
    w_uqT = jnp.concatenate([wq_n, wq_r.reshape(Q_LORA_RANK, nH * QK_ROPE_DIM)], axis=1).T.astype(BF16)
    wkv = w_ukv[l].reshape(KV_LORA_RANK, nH, QK_NOPE_DIM + V_HEAD_DIM)
    w_uk = wkv[:, :, :QK_NOPE_DIM].reshape(KV_LORA_RANK, nH * QK_NOPE_DIM).astype(BF16)
    w_uvT = wkv[:, :, QK_NOPE_DIM:].reshape(KV_LORA_RANK, nH * V_HEAD_DIM).T.astype(BF16)
    qg = (q_norm_g[l] * (QK_HEAD_DIM ** -0.5 * LOG2_E)).reshape(1, Q_LORA_RANK)
    wpool_bd = jnp.zeros((POOL_WIDTH, POOL_WIDTH), F32)
    for g in range(len(POOL_WINDOWS)):
        sl = slice(g * POOL_GROUP_DIM, (g + 1) * POOL_GROUP_DIM)
        wpool_bd = wpool_bd.at[sl, sl].set(w_pool[l, g])
    wpool_bd = wpool_bd.astype(BF16)

    qT, k, vT, yp = _pre_call(
        x, posr, mod4, norm_mix_g[l].reshape(1, D), w_in_main, w_krT, inv_freq.reshape(QK_ROPE_DIM // 2, 1),
        wpool_bd, pool_scale[l].reshape(1, POOL_WIDTH), qg, w_uqT, kv_norm_g[l].reshape(1, KV_LORA_RANK),
        w_uk, w_uvT, min(PRE_TILE, S))
    ya = _attn_call(qT, k, vT)

    wo = w_o[l].astype(BF16)
    wgT = jnp.zeros((8, D), F32).at[:N_GROUPS].set(w_group[l].T).astype(BF16)
    bg = jnp.zeros((8, 1), F32).at[:N_GROUPS, 0].set(b_group[l])
    x1, h2, rows, meta_f, tile_cnt = _post_call(
        x.reshape(N, D), yp.reshape(N, POOL_WIDTH), ya.reshape(N, nH * V_HEAD_DIM), mod4,
        wo[:POOL_WIDTH], wo[POOL_WIDTH:], norm_ffn_g[l].reshape(1, D),
        wgT, bg, w_router[l].T.astype(BF16), b_router[l].reshape(N_EXPERTS, 1), T, tiles_per_batch)

    run_len = _round_up(tile_cnt[:, :, 0], SUBLANES)
    run_loc = jnp.cumsum(run_len, axis=1) - run_len
    seg_rows = jnp.sum(run_len, axis=0)
    seg_len = _round_up(seg_rows, RB)
    pends = jnp.cumsum(seg_len).astype(I32)
    pstarts = pends - seg_len
    run_dst = pstarts[None, :] + jnp.cumsum(run_len, axis=0) - run_len
    pad_len = (seg_len - seg_rows).astype(I32)
    pad_dst = (pstarts + seg_rows).astype(I32)
    bits = jnp.arange(STRIP_BITS, dtype=I32)
    size = SUBLANES << bits
    valid = (run_len[:, None, :] & size[None, :, None]) != 0
    done = run_len[:, None, :] & ~(2 * size[None, :, None] - 1)
    flat = lambda a: a.reshape(n_tiles, STRIP_BITS * N_EXPERTS)
    valid_f = flat(valid)
    pos = jnp.cumsum(valid_f.astype(I32), axis=1) - valid_f.astype(I32)
    pick = valid_f[:, None, :] & (pos[:, None, :] == jnp.arange(MAX_STRIPS, dtype=I32)[None, :, None])
    gather = lambda a: jnp.sum(jnp.where(pick, flat(a)[:, None, :], 0), axis=-1).reshape(-1).astype(I32)
    strip_src = gather(run_loc[:, None, :] + done)
    strip_dst = gather(run_dst[:, None, :] + done)
    strip_n = jnp.sum(valid, axis=2).reshape(-1).astype(I32)
    P_pad = _round_up(2 * N + n_tiles * N_EXPERTS * (SUBLANES - 1), RB) + N_EXPERTS * RB
    n_rb = P_pad // RB
    n_used = (pends[-1:] // RB).astype(I32)
    block_start = jnp.minimum(jnp.arange(n_rb, dtype=I32), n_used - 1) * RB
    block_e = jnp.sum((pends[None, :] <= block_start[:, None]).astype(I32), axis=1)

    xs = _dispatch_call(strip_src, strip_dst, strip_n, pends, pad_len, pad_dst, rows, h2, P_pad, T, RB)
    ys = _expert_call(block_e, n_used, xs, w_gate_up[l], w_down[l], RB)
    out = _combine_call(strip_src, strip_dst, strip_n, x1, meta_f.T, mod4, final_g.reshape(1, D), ys,
                        T, tiles_per_batch)
    return out.reshape(B, S, D)
```

```python
import jax
import jax.numpy as jnp
from jax import lax
from jax.experimental import pallas as pl
from jax.experimental.pallas import tpu as pltpu

F32 = jnp.float32
BF16 = jnp.bfloat16
U32 = jnp.uint32
I32 = jnp.int32

POOL_WINDOWS = (2, 4, 8, 16)
POOL_GROUP_DIM = 64
POOL_WIDTH = 256
MLA_HEADS = 6
QK_NOPE_DIM = 128
QK_ROPE_DIM = 64
QK_HEAD_DIM = QK_NOPE_DIM + QK_ROPE_DIM
V_HEAD_DIM = 128
V_ROWS = 144
Q_LORA_RANK = 512
KV_LORA_RANK = 256
ROPE_THETA = 10000.0
N_GROUPS = 4
EXPERTS_PER_GROUP = 8
N_EXPERTS = N_GROUPS * EXPERTS_PER_GROUP
D_EXPERT = 256
N_MOD = 6
EPS = 1e-6

SUBLANES = 8
POOL_HALO = 32
ROW_TILE = 512
PRE_TILE = 1024
ATTN_TQ = 2048
ATTN_TK = 512
EXPERT_ROWS = 1024
LOCAL_ROWS = 2 * ROW_TILE + 256
assert LOCAL_ROWS >= 2 * ROW_TILE + N_EXPERTS * (SUBLANES - 1)
STRIP_BITS = (2 * ROW_TILE // SUBLANES).bit_length()
MAX_STRIPS = 128
assert MAX_STRIPS >= 2 * N_EXPERTS + (LOCAL_ROWS // SUBLANES - 3 * N_EXPERTS) // 4
DISPATCH_SLOTS = 3
VMEM_LIMIT = 56 * 1024 * 1024
NEG_BIG = -1e30
LOG2_E = 1.4426950408889634
HI_MASK = 0xFFFF0000


def _cparams(*sem):
    return pltpu.CompilerParams(dimension_semantics=sem, vmem_limit_bytes=VMEM_LIMIT)


def _rms(x, g):
    return x * lax.rsqrt(jnp.mean(x * x, axis=-1, keepdims=True) + EPS) * g


def _pack_bf16_pair(lo, hi):
    return lax.bitcast_convert_type(hi, U32) | (lax.bitcast_convert_type(lo, U32) >> 16)


def _unpack_bf16_pair(w):
    lo = lax.bitcast_convert_type(w << 16, F32).astype(BF16)
    hi = lax.bitcast_convert_type(w & jnp.uint32(HI_MASK), F32).astype(BF16)
    return lo, hi


def _mod_kernel(c_ref, w_ref, b_ref, o_ref):
    c = c_ref[...]
    ca = c / (1.0 + jnp.exp(-c))
    o_ref[...] = jnp.dot(ca.astype(BF16), w_ref[...].astype(BF16), preferred_element_type=F32) + b_ref[...]


def _mod_call(c, w_mod, b_mod):
    B, D = c.shape
    n_out = w_mod.shape[1]
    tn = 512
    return pl.pallas_call(
        _mod_kernel,
        out_shape=jax.ShapeDtypeStruct((B, n_out), F32),
        grid=(n_out // tn,),
        in_specs=[pl.BlockSpec((B, D), lambda j: (0, 0)),
                  pl.BlockSpec((D, tn), lambda j: (0, j)),
                  pl.BlockSpec((1, tn), lambda j: (0, j))],
        out_specs=pl.BlockSpec((B, tn), lambda j: (0, j)),
        compiler_params=_cparams("arbitrary"),
        name="mod",
    )(c, w_mod, b_mod.reshape(1, n_out))


def _pre_kernel(x_ref, posr_ref, mod_ref, g_ref, win_ref, wkrT_ref, invfc_ref, wpool_ref, pscale_ref,
                qg_ref, wuqT_ref, kvg_ref, wuk_ref, wuvT_ref,
                qT_ref, k_ref, vT_ref, yp_ref,
                pbuf, b2, b4, b8):
    T = x_ref.shape[0]
    H = POOL_HALO
    i = pl.program_id(1)
    nt = (((1,), (1,)), ((), ()))

    shift = mod_ref[0]
    scale = mod_ref[1]
    hb = (_rms(x_ref[...], g_ref[...]) * (1.0 + scale) + shift).astype(BF16)
    u = jnp.dot(hb, win_ref[...], preferred_element_type=F32)
    krT = lax.dot_general(wkrT_ref[...], hb, nt, preferred_element_type=F32)

    p = u[:, :POOL_WIDTH]

    @pl.when(i == 0)
    def _():
        pbuf[0:H, :] = jnp.zeros((H, POOL_WIDTH), F32)

    pbuf[H:H + T, :] = p
    b2[8:T + H, :] = pbuf[8:T + H, :] + pbuf[7:T + H - 1, :]
    b4[16:T + H, :] = b2[16:T + H, :] + b2[14:T + H - 2, :]
    b8[24:T + H, :] = b4[24:T + H, :] + b4[20:T + H - 4, :]
    s2 = b2[H:T + H, :]
    s4 = b4[H:T + H, :]
    s8 = b8[H:T + H, :]
    s16 = b8[H:T + H, :] + b8[H - 8:T + H - 8, :]
    pbuf[0:H, :] = pbuf[T:T + H, :]

    lane = lax.broadcasted_iota(I32, (T, POOL_WIDTH), 1)
    t1 = (lax.broadcasted_iota(I32, (T, 1), 0) + (i * T + 1)).astype(F32)
    inv2 = 1.0 / jnp.minimum(t1, 2.0)
    inv4 = 1.0 / jnp.minimum(t1, 4.0)
    inv8 = 1.0 / jnp.minimum(t1, 8.0)
    inv16 = 1.0 / jnp.minimum(t1, 16.0)
    mean = jnp.where(lane < 64, s2 * inv2,
                     jnp.where(lane < 128, s4 * inv4,
                               jnp.where(lane < 192, s8 * inv8, s16 * inv16)))
    pooled = mean - p
    yp = jnp.dot(pooled.astype(BF16), wpool_ref[...], preferred_element_type=F32) * pscale_ref[...]
    yp_ref[...] = yp.astype(yp_ref.dtype)

    ang = invfc_ref[...] * posr_ref[...].astype(F32)
    cos_h = jnp.cos(ang)
    sin_h = jnp.sin(ang)
    hr = QK_ROPE_DIM // 2

    def rope(xt):
        x1, x2 = xt[0:hr], xt[hr:]
        return x1 * cos_h - x2 * sin_h, x2 * cos_h + x1 * sin_h

    k_rope = jnp.concatenate(rope(krT), axis=0).T.astype(BF16)

    cq = u[:, POOL_WIDTH:POOL_WIDTH + Q_LORA_RANK]
    ckv = u[:, POOL_WIDTH + Q_LORA_RANK:POOL_WIDTH + Q_LORA_RANK + KV_LORA_RANK]
    cqn = _rms(cq, qg_ref[...]).astype(BF16)
    ckvn = _rms(ckv, kvg_ref[...]).astype(BF16)
    qaT = lax.dot_general(wuqT_ref[...], cqn, nt, preferred_element_type=F32)
    kn = jnp.dot(ckvn, wuk_ref[...], preferred_element_type=F32)
    vT = lax.dot_general(wuvT_ref[...], ckvn, nt, preferred_element_type=F32)
    nq = MLA_HEADS * QK_NOPE_DIM
    ones_rows = jnp.where(lax.broadcasted_iota(I32, (V_ROWS - V_HEAD_DIM, T), 0) == 0, 1.0, 0.0).astype(BF16)
    for hd in range(MLA_HEADS):
        q1, q2 = rope(qaT[nq + hd * QK_ROPE_DIM:nq + (hd + 1) * QK_ROPE_DIM, :])
        qT_ref[hd, 0:QK_NOPE_DIM, :] = qaT[hd * QK_NOPE_DIM:(hd + 1) * QK_NOPE_DIM, :].astype(BF16)
        qT_ref[hd, QK_NOPE_DIM:QK_NOPE_DIM + hr, :] = q1.astype(BF16)
        qT_ref[hd, QK_NOPE_DIM + hr:QK_HEAD_DIM, :] = q2.astype(BF16)
        k_ref[hd, :, 0:QK_NOPE_DIM] = kn[:, hd * QK_NOPE_DIM:(hd + 1) * QK_NOPE_DIM].astype(BF16)
        k_ref[hd, :, QK_NOPE_DIM:QK_HEAD_DIM] = k_rope
        vT_ref[hd, 0:V_HEAD_DIM, :] = vT[hd * V_HEAD_DIM:(hd + 1) * V_HEAD_DIM, :].astype(BF16)
        vT_ref[hd, V_HEAD_DIM:V_ROWS, :] = ones_rows


def _pre_call(x, posr, mod4, norm_g, w_in_main, w_krT, inv_freq_col, wpool_bd, pool_scale, qg, w_uqT, kvg,
              w_uk, w_uvT, T):
    B, S, D = x.shape
    nH = MLA_HEADS
    const = lambda shape: pl.BlockSpec(shape, lambda b, i: (0,) * len(shape))
    return pl.pallas_call(
        _pre_kernel,
        out_shape=(jax.ShapeDtypeStruct((B, nH, QK_HEAD_DIM, S), BF16),
                   jax.ShapeDtypeStruct((B, nH, S, QK_HEAD_DIM), BF16),
                   jax.ShapeDtypeStruct((B, nH, V_ROWS, S), BF16),
                   jax.ShapeDtypeStruct((B, S, POOL_WIDTH), BF16)),
        grid=(B, S // T),
        in_specs=[pl.BlockSpec((None, T, D), lambda b, i: (b, i, 0)),
                  pl.BlockSpec((None, 1, T), lambda b, i: (b, 0, i)),
                  pl.BlockSpec((None, N_MOD, 1, D), lambda b, i: (b, 0, 0, 0)),
                  const((1, D)),
                  const(w_in_main.shape),
                  const(w_krT.shape),
                  const(inv_freq_col.shape),
                  const(wpool_bd.shape),
                  const((1, POOL_WIDTH)),
                  const((1, Q_LORA_RANK)),
                  const(w_uqT.shape),
                  const((1, KV_LORA_RANK)),
                  const(w_uk.shape),
                  const(w_uvT.shape)],
        out_specs=(pl.BlockSpec((None, nH, QK_HEAD_DIM, T), lambda b, i: (b, 0, 0, i)),
                   pl.BlockSpec((None, nH, T, QK_HEAD_DIM), lambda b, i: (b, 0, i, 0)),
                   pl.BlockSpec((None, nH, V_ROWS, T), lambda b, i: (b, 0, 0, i)),
                   pl.BlockSpec((None, T, POOL_WIDTH), lambda b, i: (b, i, 0))),
        scratch_shapes=[pltpu.VMEM((T + POOL_HALO, POOL_WIDTH), F32)] * 4,
        compiler_params=_cparams("arbitrary", "arbitrary"),
        name="pre",
    )(x, posr, mod4, norm_g, w_in_main, w_krT, inv_freq_col, wpool_bd, pool_scale, qg, w_uqT, kvg, w_uk, w_uvT)


def _attn_kernel(qT_ref, k_ref, vT_ref, o_ref):
    S = k_ref.shape[0]
    tq, tk = min(ATTN_TQ, S), ATTN_TK
    diag = lax.broadcasted_iota(I32, (tk, tk), 0) <= lax.broadcasted_iota(I32, (tk, tk), 1)
    blocks = [(i, j) for i in range(S // tq) for j in range((i + 1) * (tq // tk))]

    def scores(i, j):
        d = max(j * tk - i * tq, 0)
        sT = jnp.dot(k_ref[j * tk:(j + 1) * tk, :], qT_ref[:, i * tq + d:(i + 1) * tq],
                     preferred_element_type=F32)
        if j * tk >= i * tq:
            masked = jnp.where(diag, sT[:, :tk], NEG_BIG)
            sT = masked if sT.shape[1] == tk else jnp.concatenate([masked, sT[:, tk:]], axis=1)
        return sT

    s_next = scores(*blocks[0])
    m = acc = None
    for n, (i, j) in enumerate(blocks):
        sT = s_next
        if n + 1 < len(blocks):
            s_next = scores(*blocks[n + 1])
        if j == 0:
            m = jnp.full((1, tq), NEG_BIG, F32)
            acc = jnp.zeros((V_ROWS, tq), F32)
        d = max(j * tk - i * tq, 0)
        vT = vT_ref[:, j * tk:(j + 1) * tk]
        m_new = jnp.maximum(m[:, d:], jnp.max(sT, axis=0, keepdims=True))
        pT = jnp.exp2(sT - m_new).astype(BF16)
        acc_new = jnp.exp2(m[:, d:] - m_new) * acc[:, d:] + jnp.dot(vT, pT, preferred_element_type=F32)
        if d:
            m_new = jnp.concatenate([m[:, :d], m_new], axis=1)
            acc_new = jnp.concatenate([acc[:, :d], acc_new], axis=1)
        m, acc = m_new, acc_new
        if j == (i + 1) * (tq // tk) - 1:
            out = acc[0:V_HEAD_DIM] * (1.0 / acc[V_HEAD_DIM:V_HEAD_DIM + 1])
            o_ref[i * tq:(i + 1) * tq, :] = out.T.astype(o_ref.dtype)


def _attn_call(qT, k, vT):
    B, nH, S, _ = k.shape
    return pl.pallas_call(
        _attn_kernel,
        out_shape=jax.ShapeDtypeStruct((B, S, nH * V_HEAD_DIM), BF16),
        grid=(B, nH),
        in_specs=[pl.BlockSpec((None, None, QK_HEAD_DIM, S), lambda b, h: (b, h, 0, 0)),
                  pl.BlockSpec((None, None, S, QK_HEAD_DIM), lambda b, h: (b, h, 0, 0)),
                  pl.BlockSpec((None, None, V_ROWS, S), lambda b, h: (b, h, 0, 0))],
        out_specs=pl.BlockSpec((None, S, V_HEAD_DIM), lambda b, h: (b, 0, h)),
        compiler_params=_cparams("arbitrary", "arbitrary"),
        name="attn",
    )(qT, k, vT)


def _post_kernel(x_ref, yp_ref, ya_ref, mod_ref, wo_p_ref, wo_a_ref, g_ref,
                 wgT_ref, bg_ref, wrT_ref, br_ref,
                 x1_ref, h2_ref, rows_ref, mf_ref, cnt_ref):
    T = x_ref.shape[0]

    gate_a = mod_ref[2]
    shift_f = mod_ref[3]
    scale_f = mod_ref[4]
    mix = (jnp.dot(yp_ref[...], wo_p_ref[...], preferred_element_type=F32)
           + jnp.dot(ya_ref[...], wo_a_ref[...], preferred_element_type=F32))
    x1 = x_ref[...] + gate_a * mix
    x1_ref[...] = x1
    hb = (_rms(x1, g_ref[...]) * (1.0 + scale_f) + shift_f).astype(BF16)
    h2_ref[...] = hb

    nt = (((1,), (1,)), ((), ()))
    gl = lax.dot_general(wgT_ref[...], hb, nt, preferred_element_type=F32)
    el = lax.dot_general(wrT_ref[...], hb, nt, preferred_element_type=F32)

    r8 = lax.broadcasted_iota(I32, (8, T), 0)
    gvalid = r8 < N_GROUPS
    gmax = jnp.max(jnp.where(gvalid, gl, NEG_BIG), axis=0, keepdims=True)
    gexp = jnp.where(gvalid, jnp.exp(gl - gmax), 0.0)
    g_prob = gexp / jnp.sum(gexp, axis=0, keepdims=True)
    gb = jnp.where(gvalid, gl + bg_ref[...], NEG_BIG)
    gbmax = jnp.max(gb, axis=0, keepdims=True)
    g_sel = jnp.min(jnp.where(gb == gbmax, r8, 8), axis=0, keepdims=True)
    gp = jnp.sum(jnp.where(r8 == g_sel, g_prob, 0.0), axis=0, keepdims=True)

    e_in = jnp.zeros((EXPERTS_PER_GROUP, T), F32)
    b_in = jnp.zeros((EXPERTS_PER_GROUP, T), F32)
    br = br_ref[...]
    for g in range(N_GROUPS):
        sel = g_sel == g
        e_in = jnp.where(sel, el[g * 8:(g + 1) * 8, :], e_in)
        b_in = jnp.where(sel, br[g * 8:(g + 1) * 8, :], b_in)
    eb = e_in + b_in
    m1 = jnp.max(eb, axis=0, keepdims=True)
    i1 = jnp.min(jnp.where(eb == m1, r8, 8), axis=0, keepdims=True)
    eb2 = jnp.where(r8 == i1, NEG_BIG, eb)
    m2 = jnp.max(eb2, axis=0, keepdims=True)
    i2 = jnp.min(jnp.where(eb2 == m2, r8, 8), axis=0, keepdims=True)
    emax = jnp.max(e_in, axis=0, keepdims=True)
    eexp = jnp.exp(e_in - emax)
    sp = eexp / jnp.sum(eexp, axis=0, keepdims=True)
    p1 = jnp.sum(jnp.where(r8 == i1, sp, 0.0), axis=0, keepdims=True)
    p2 = jnp.sum(jnp.where(r8 == i2, sp, 0.0), axis=0, keepdims=True)
    tot = p1 + p2
    w1 = gp * (p1 / tot)
    w2 = gp * (p2 / tot)
    e1 = g_sel * EXPERTS_PER_GROUP + i1
    e2 = g_sel * EXPERTS_PER_GROUP + i2

    r32 = lax.broadcasted_iota(I32, (N_EXPERTS, T), 0)
    oh1 = r32 == e1
    oh2 = r32 == e2
    oh = jnp.where(oh1 | oh2, 1.0, 0.0)
    upper = jnp.where(lax.broadcasted_iota(I32, (T, T), 0) < lax.broadcasted_iota(I32, (T, T), 1),
                      1.0, 0.0).astype(BF16)
    before = jnp.dot(oh.astype(BF16), upper, preferred_element_type=F32)
    cnt = jnp.sum(oh, axis=1, keepdims=True)
    run8 = jnp.floor((cnt + (SUBLANES - 1.0)) * (1.0 / SUBLANES))
    lower = jnp.where(lax.broadcasted_iota(I32, (N_EXPERTS, N_EXPERTS), 1)
                      < lax.broadcasted_iota(I32, (N_EXPERTS, N_EXPERTS), 0), 1.0, 0.0).astype(BF16)
    run_start = jnp.dot(lower, jnp.broadcast_to(run8, (N_EXPERTS, 128)).astype(BF16),
                        preferred_element_type=F32)[:, 0:1] * float(SUBLANES)
    pos = before + run_start
    row1 = jnp.sum(jnp.where(oh1, pos, 0.0), axis=0, keepdims=True)
    row2 = jnp.sum(jnp.where(oh2, pos, 0.0), axis=0, keepdims=True)
    cnt_ref[...] = cnt.astype(I32)

    rows_ref[0:1, :] = row1.astype(I32)
    rows_ref[1:2, :] = row2.astype(I32)
    mf_ref[0:1, :] = row1
    mf_ref[1:2, :] = row2
    mf_ref[2:3, :] = w1
    mf_ref[3:4, :] = w2


def _post_call(x2, yp2, ya2, mod4, wo_p, wo_a, g, wgT, bg, wrT, br, T, tiles_per_batch):
    N, D = x2.shape
    const = lambda shape: pl.BlockSpec(shape, lambda i: (0,) * len(shape))
    return pl.pallas_call(
        _post_kernel,
        out_shape=(jax.ShapeDtypeStruct((N, D), F32),
                   jax.ShapeDtypeStruct((N, D), BF16),
                   jax.ShapeDtypeStruct((2, N), I32),
                   jax.ShapeDtypeStruct((4, N), F32),
                   jax.ShapeDtypeStruct((N // T, N_EXPERTS, 1), I32)),
        grid=(N // T,),
        in_specs=[pl.BlockSpec((T, D), lambda i: (i, 0)),
                  pl.BlockSpec((T, POOL_WIDTH), lambda i: (i, 0)),
                  pl.BlockSpec((T, MLA_HEADS * V_HEAD_DIM), lambda i: (i, 0)),
                  pl.BlockSpec((None, N_MOD, 1, D), lambda i: (i // tiles_per_batch, 0, 0, 0)),
                  const(wo_p.shape), const(wo_a.shape), const((1, D)),
                  const(wgT.shape), const(bg.shape), const(wrT.shape), const(br.shape)],
        out_specs=(pl.BlockSpec((T, D), lambda i: (i, 0)),
                   pl.BlockSpec((T, D), lambda i: (i, 0)),
                   pl.BlockSpec((2, T), lambda i: (0, i)),
                   pl.BlockSpec((4, T), lambda i: (0, i)),
                   pl.BlockSpec((None, N_EXPERTS, 1), lambda i: (i, 0, 0))),
        compiler_params=_cparams("arbitrary"),
        name="post",
    )(x2, yp2, ya2, mod4, wo_p, wo_a, g, wgT, bg, wrT, br)


def _for_each_strip(tile, src_ref, dst_ref, n_ref, fn):
    first = tile * MAX_STRIPS
    for b in range(STRIP_BITS):
        last = first + n_ref[tile * STRIP_BITS + b]

        def body(r, c, n=SUBLANES << b):
            fn(pl.multiple_of(src_ref[r], SUBLANES), pl.multiple_of(dst_ref[r], SUBLANES), n)
            return c

        lax.fori_loop(first, last, body, 0)
        first = last


def _dispatch_kernel(src_ref, dst_ref, n_ref, pends_ref, padlen_ref, paddst_ref, rows_ref, h_ref, xs_ref,
                     lbuf, zbuf, sems, zsem):
    T = h_ref.shape[0]
    RB = zbuf.shape[0]
    half = h_ref.shape[1] // 2
    step = pl.program_id(0)
    n_steps = pl.num_programs(0)
    slot = step % DISPATCH_SLOTS

    def zero_fill(act):
        def per_expert(e, c):
            pad = padlen_ref[e]
            dst = paddst_ref[e]
            for b in reversed(range((RB // SUBLANES).bit_length() - 1)):
                n = SUBLANES << b
                done = pad & ~(2 * n - 1)

                @pl.when((pad & n) != 0)
                def _():
                    act(pltpu.make_async_copy(zbuf.at[pl.ds(0, n)],
                                              xs_ref.at[pl.ds(pl.multiple_of(dst + done, SUBLANES), n)], zsem))
            return c

        lax.fori_loop(0, N_EXPERTS, per_expert, 0)

        def per_block(b, c):
            act(pltpu.make_async_copy(zbuf, xs_ref.at[pl.ds(pl.multiple_of(b * RB, RB), RB)], zsem))
            return c

        lax.fori_loop(pends_ref[N_EXPERTS - 1] // RB, xs_ref.shape[0] // RB, per_block, 0)

    @pl.when(step == 0)
    def _():
        zbuf[...] = jnp.zeros(zbuf.shape, zbuf.dtype)
        zero_fill(lambda cp: cp.start())

    r = lax.broadcasted_iota(I32, (LOCAL_ROWS, T), 0)
    perm = jnp.where((r == rows_ref[0:1, :]) | (r == rows_ref[1:2, :]), 1.0, 0.0).astype(BF16)
    h = h_ref[...]
    lo = jnp.dot(perm, h[:, :half], preferred_element_type=F32)
    hi = jnp.dot(perm, h[:, half:], preferred_element_type=F32)
    lbuf[slot] = _pack_bf16_pair(lo, hi)

    def strip(tile_slot):
        def make(loc, dst, n):
            return pltpu.make_async_copy(lbuf.at[tile_slot, pl.ds(loc, n)], xs_ref.at[pl.ds(dst, n)],
                                         sems.at[tile_slot])
        return make

    def wait_tile(tile):
        _for_each_strip(tile, src_ref, dst_ref, n_ref, lambda *a: strip(tile % DISPATCH_SLOTS)(*a).wait())

    _for_each_strip(step, src_ref, dst_ref, n_ref, lambda *a: strip(slot)(*a).start())

    @pl.when(step >= DISPATCH_SLOTS - 1)
    def _():
        wait_tile(step - (DISPATCH_SLOTS - 1))

    @pl.when(step == n_steps - 1)
    def _():
        for back in reversed(range(DISPATCH_SLOTS - 1)):
            @pl.when(step >= back)
            def _():
                wait_tile(step - back)
        zero_fill(lambda cp: cp.wait())


def _dispatch_call(strip_src, strip_dst, strip_n, pends, pad_len, pad_dst, rows, h2, P_pad, T, RB):
    N, D = h2.shape
    return pl.pallas_call(
        _dispatch_kernel,
        out_shape=jax.ShapeDtypeStruct((P_pad, D // 2), U32),
        grid_spec=pltpu.PrefetchScalarGridSpec(
            num_scalar_prefetch=6,
            grid=(N // T,),
            in_specs=[pl.BlockSpec((2, T), lambda i, *_: (0, i)),
                      pl.BlockSpec((T, D), lambda i, *_: (i, 0))],
            out_specs=pl.BlockSpec(memory_space=pl.ANY),
            scratch_shapes=[pltpu.VMEM((DISPATCH_SLOTS, LOCAL_ROWS, D // 2), U32),
                            pltpu.VMEM((RB, D // 2), U32),
                            pltpu.SemaphoreType.DMA((DISPATCH_SLOTS,)),
                            pltpu.SemaphoreType.DMA]),
        compiler_params=_cparams("arbitrary"),
        name="dispatch",
    )(strip_src, strip_dst, strip_n, pends, pad_len, pad_dst, rows, h2)


def _expert_kernel(be_ref, nused_ref, xs_ref, wgu_ref, wd_ref, ys_ref, wgu_bf, wd_bf):
    i = pl.program_id(0)
    used = i < nused_ref[0]
    half = wgu_ref.shape[0] // 2

    @pl.when(used & ((i == 0) | (be_ref[i] != be_ref[jnp.maximum(i - 1, 0)])))
    def _():
        wgu_bf[...] = wgu_ref[...].astype(BF16)
        wd_bf[...] = wd_ref[...].astype(BF16)

    @pl.when(used)
    def _():
        lo, hi = _unpack_bf16_pair(xs_ref[...])
        gu = (jnp.dot(lo, wgu_bf[0:half, :], preferred_element_type=F32)
              + jnp.dot(hi, wgu_bf[half:, :], preferred_element_type=F32))
        gate = gu[:, :D_EXPERT]
        up = gu[:, D_EXPERT:]
        act = gate / (1.0 + jnp.exp(-gate)) * up
        y = jnp.dot(act.astype(BF16), wd_bf[...], preferred_element_type=F32)
        yb = y.astype(BF16).astype(F32)
        ys_ref[...] = _pack_bf16_pair(yb[:, :half], yb[:, half:])

    @pl.when(jnp.logical_not(used))
    def _():
        ys_ref[...] = jnp.zeros(ys_ref.shape, ys_ref.dtype)


def _expert_call(block_e, n_used, xs, wgu, wd, RB):
    P, Dh = xs.shape
    D = 2 * Dh
    row_map = lambda i, be, nu: (i, 0)
    in_map = lambda i, be, nu: (jnp.minimum(i, nu[0] - 1), 0)
    return pl.pallas_call(
        _expert_kernel,
        out_shape=jax.ShapeDtypeStruct((P, Dh), U32),
        grid_spec=pltpu.PrefetchScalarGridSpec(
            num_scalar_prefetch=2,
            grid=(P // RB,),
            in_specs=[pl.BlockSpec((RB, Dh), in_map),
                      pl.BlockSpec((None, D, 2 * D_EXPERT), lambda i, be, nu: (be[i], 0, 0)),
                      pl.BlockSpec((None, D_EXPERT, D), lambda i, be, nu: (be[i], 0, 0))],
            out_specs=pl.BlockSpec((RB, Dh), row_map),
            scratch_shapes=[pltpu.VMEM((D, 2 * D_EXPERT), BF16), pltpu.VMEM((D_EXPERT, D), BF16)]),
        compiler_params=_cparams("arbitrary"),
        name="experts",
    )(block_e, n_used, xs, wgu, wd)


def _combine_kernel(src_ref, dst_ref, n_ref, x1_ref, tok_ref, mod_ref, g_ref, ys_ref, o_ref, ybuf, sems):
    T = x1_ref.shape[0]
    step = pl.program_id(0)
    n_steps = pl.num_programs(0)
    slot = step % 2

    def strip(tile_slot):
        def make(loc, dst, n):
            return pltpu.make_async_copy(ys_ref.at[pl.ds(dst, n)], ybuf.at[tile_slot, pl.ds(loc, n)],
                                         sems.at[tile_slot])
        return make

    @pl.when(step == 0)
    def _():
        ybuf[...] = jnp.zeros(ybuf.shape, ybuf.dtype)
        _for_each_strip(step, src_ref, dst_ref, n_ref, lambda *a: strip(slot)(*a).start())

    @pl.when(step + 1 < n_steps)
    def _():
        _for_each_strip(step + 1, src_ref, dst_ref, n_ref, lambda *a: strip(1 - slot)(*a).start())

    _for_each_strip(step, src_ref, dst_ref, n_ref, lambda *a: strip(slot)(*a).wait())

    tok = tok_ref[...]
    lo, hi = _unpack_bf16_pair(ybuf[slot])
    n_chunks = 4
    tc = T // n_chunks

    def permw_chunk(ci):
        t = tok[ci * tc:(ci + 1) * tc]
        rows = t[:, 0:2].astype(I32)
        c = lax.broadcasted_iota(I32, (tc, LOCAL_ROWS), 1)
        return jnp.where(c == rows[:, 0:1], t[:, 2:3], jnp.where(c == rows[:, 1:2], t[:, 3:4], 0.0)).astype(BF16)

    p_next = permw_chunk(0)
    parts = []
    for ci in range(n_chunks):
        permw = p_next
        if ci + 1 < n_chunks:
            p_next = permw_chunk(ci + 1)
        parts.append(jnp.concatenate([jnp.dot(permw, lo, preferred_element_type=F32),
                                      jnp.dot(permw, hi, preferred_element_type=F32)], axis=1))
    moe = jnp.concatenate(parts, axis=0)
    gate_f = mod_ref[5]
    o_ref[...] = _rms(x1_ref[...] + gate_f * moe, g_ref[...])


def _combine_call(strip_src, strip_dst, strip_n, x1, tok, mod4, final_g, ys, T, tiles_per_batch):
    N, D = x1.shape
    return pl.pallas_call(
        _combine_kernel,
        out_shape=jax.ShapeDtypeStruct((N, D), F32),
        grid_spec=pltpu.PrefetchScalarGridSpec(
            num_scalar_prefetch=3,
            grid=(N // T,),
            in_specs=[pl.BlockSpec((T, D), lambda i, *_: (i, 0)),
                      pl.BlockSpec((T, 4), lambda i, *_: (i, 0)),
                      pl.BlockSpec((None, N_MOD, 1, D), lambda i, *_: (i // tiles_per_batch, 0, 0, 0)),
                      pl.BlockSpec((1, D), lambda i, *_: (0, 0)),
                      pl.BlockSpec(memory_space=pl.ANY)],
            out_specs=pl.BlockSpec((T, D), lambda i, *_: (i, 0)),
            scratch_shapes=[pltpu.VMEM((2, LOCAL_ROWS, D // 2), U32), pltpu.SemaphoreType.DMA((2,))]),
        compiler_params=_cparams("arbitrary"),
        name="combine",
    )(strip_src, strip_dst, strip_n, x1, tok, mod4, final_g, ys)


def _round_up(v, m):
    return (v + m - 1) // m * m


def kernel(x, c, positions, w_mod, b_mod, norm_mix_g, w_in, w_pool, pool_scale, q_norm_g, w_uq, kv_norm_g, w_ukv, w_o, norm_ffn_g, w_group, b_group, w_router, b_router, w_gate_up, w_down, final_g):
    B, S, D = x.shape
    N = B * S
    depth = w_mod.shape[0]
    T = ROW_TILE
    RB = EXPERT_ROWS
    assert depth == 1, "the final RMSNorm is fused into the layer's combine step"
    assert S % T == 0 and S % min(PRE_TILE, S) == 0 and S % min(ATTN_TQ, S) == 0 and min(ATTN_TQ, S) % ATTN_TK == 0
    tiles_per_batch = S // T
    n_tiles = N // T
    nH = MLA_HEADS
    l = 0

    inv_freq = ROPE_THETA ** (-(jnp.arange(0, QK_ROPE_DIM, 2, dtype=F32) / QK_ROPE_DIM))
    posr = positions.reshape(B, 1, S)
    cut1 = POOL_WIDTH
    cut2 = cut1 + Q_LORA_RANK
    cut3 = cut2 + KV_LORA_RANK

    mod4 = _mod_call(c, w_mod[l], b_mod[l]).reshape(B, N_MOD, 1, D)

    wi = w_in[l]
    w_in_main = wi[:, :cut3].astype(BF16)
    w_krT = wi[:, cut3:].T.astype(BF16)
    wq = w_uq[l].reshape(Q_LORA_RANK, nH, QK_HEAD_DIM)
    wq_n = wq[:, :, :QK_NOPE_DIM].reshape(Q_LORA_RANK, nH * QK_NOPE_DIM)
    wq_r = wq[:, :, QK_NOPE_DIM:]
    w_uqT = jnp.concatenate([wq_n, wq_r.reshape(Q_LORA_RANK, nH * QK_ROPE_DIM)], axis=1).T.astype(BF16)
    wkv = w_ukv[l].reshape(KV_LORA_RANK, nH, QK_NOPE_DIM + V_HEAD_DIM)
    w_uk = wkv[:, :, :QK_NOPE_DIM].reshape(KV_LORA_RANK, nH * QK_NOPE_DIM).astype(BF16)
    w_uvT = wkv[:, :, QK_NOPE_DIM:].reshape(KV_LORA_RANK, nH * V_HEAD_DIM).T.astype(BF16)
    qg = (q_norm_g[l] * (QK_HEAD_DIM ** -0.5 * LOG2_E)).reshape(1, Q_LORA_RANK)
    wpool_bd = jnp.zeros((POOL_WIDTH, POOL_WIDTH), F32)
    for g in range(len(POOL_WINDOWS)):
        sl = slice(g * POOL_GROUP_DIM, (g + 1) * POOL_GROUP_DIM)
        wpool_bd = wpool_bd.at[sl, sl].set(w_pool[l, g])
    wpool_bd = wpool_bd.astype(BF16)

    qT, k, vT, yp = _pre_call(
        x, posr, mod4, norm_mix_g[l].reshape(1, D), w_in_main, w_krT, inv_freq.reshape(QK_ROPE_DIM // 2, 1),
        wpool_bd, pool_scale[l].reshape(1, POOL_WIDTH), qg, w_uqT, kv_norm_g[l].reshape(1, KV_LORA_RANK),
        w_uk, w_uvT, min(PRE_TILE, S))
    ya = _attn_call(qT, k, vT)

    wo = w_o[l].astype(BF16)
    wgT = jnp.zeros((8, D), F32).at[:N_GROUPS].set(w_group[l].T).astype(BF16)
    bg = jnp.zeros((8, 1), F32).at[:N_GROUPS, 0].set(b_group[l])
    x1, h2, rows, meta_f, tile_cnt = _post_call(
        x.reshape(N, D), yp.reshape(N, POOL_WIDTH), ya.reshape(N, nH * V_HEAD_DIM), mod4,
        wo[:POOL_WIDTH], wo[POOL_WIDTH:], norm_ffn_g[l].reshape(1, D),
        wgT, bg, w_router[l].T.astype(BF16), b_router[l].reshape(N_EXPERTS, 1), T, tiles_per_batch)

    run_len = _round_up(tile_cnt[:, :, 0], SUBLANES)
    run_loc = jnp.cumsum(run_len, axis=1) - run_len
    seg_rows = jnp.sum(run_len, axis=0)
    seg_len = _round_up(seg_rows, RB)
    pends = jnp.cumsum(seg_len).astype(I32)
    pstarts = pends - seg_len
    run_dst = pstarts[None, :] + jnp.cumsum(run_len, axis=0) - run_len
    pad_len = (seg_len - seg_rows).astype(I32)
    pad_dst = (pstarts + seg_rows).astype(I32)
    bits = jnp.arange(STRIP_BITS, dtype=I32)
    size = SUBLANES << bits
    valid = (run_len[:, None, :] & size[None, :, None]) != 0
    done = run_len[:, None, :] & ~(2 * size[None, :, None] - 1)
    flat = lambda a: a.reshape(n_tiles, STRIP_BITS * N_EXPERTS)
    valid_f = flat(valid)
    pos = jnp.cumsum(valid_f.astype(I32), axis=1) - valid_f.astype(I32)
    pick = valid_f[:, None, :] & (pos[:, None, :] == jnp.arange(MAX_STRIPS, dtype=I32)[None, :, None])
    gather = lambda a: jnp.sum(jnp.where(pick, flat(a)[:, None, :], 0), axis=-1).reshape(-1).astype(I32)
    strip_src = gather(run_loc[:, None, :] + done)
    strip_dst = gather(run_dst[:, None, :] + done)
    strip_n = jnp.sum(valid, axis=2).reshape(-1).astype(I32)
    P_pad = _round_up(2 * N + n_tiles * N_EXPERTS * (SUBLANES - 1), RB) + N_EXPERTS * RB
    n_rb = P_pad // RB
    n_used = (pends[-1:] // RB).astype(I32)
    block_start = jnp.minimum(jnp.arange(n_rb, dtype=I32), n_used - 1) * RB
    block_e = jnp.sum((pends[None, :] <= block_start[:, None]).astype(I32), axis=1)

    xs = _dispatch_call(strip_src, strip_dst, strip_n, pends, pad_len, pad_dst, rows, h2, P_pad, T, RB)
    ys = _expert_call(block_e, n_used, xs, w_gate_up[l], w_down[l], RB)
    out = _combine_call(strip_src, strip_dst, strip_n, x1, meta_f.T, mod4, final_g.reshape(1, D), ys,
                        T, tiles_per_batch)
    return out.reshape(B, S, D)
```

```python
import jax
import jax.numpy as jnp
from jax import lax
from jax.experimental import pallas as pl
from jax.experimental.pallas import tpu as pltpu

F32 = jnp.float32
BF16 = jnp.bfloat16
U32 = jnp.uint32
I32 = jnp.int32

POOL_WINDOWS = (2, 4, 8, 16)
POOL_GROUP_DIM = 64
POOL_WIDTH = 256
MLA_HEADS = 6
QK_NOPE_DIM = 128
QK_ROPE_DIM = 64
QK_HEAD_DIM = QK_NOPE_DIM + QK_ROPE_DIM
V_HEAD_DIM = 128
V_ROWS = 144
Q_LORA_RANK = 512
KV_LORA_RANK = 256
ROPE_THETA = 10000.0
N_GROUPS = 4
EXPERTS_PER_GROUP = 8
N_EXPERTS = N_GROUPS * EXPERTS_PER_GROUP
D_EXPERT = 256
N_MOD = 6
EPS = 1e-6

SUBLANES = 8
POOL_HALO = 32
ROW_TILE = 512
PRE_TILE = 1024
POST_TILE = 1024
ATTN_TQ = 2048
ATTN_TK = 512
EXPERT_ROWS = 1024
LOCAL_ROWS = 2 * ROW_TILE + 256
assert LOCAL_ROWS >= 2 * ROW_TILE + N_EXPERTS * (SUBLANES - 1)
STRIP_BITS = (2 * ROW_TILE // SUBLANES).bit_length()
MAX_STRIPS = 128
assert MAX_STRIPS >= 2 * N_EXPERTS + (LOCAL_ROWS // SUBLANES - 3 * N_EXPERTS) // 4
DISPATCH_SLOTS = 2
VMEM_LIMIT = 56 * 1024 * 1024
NEG_BIG = -1e30
LOG2_E = 1.4426950408889634
HI_MASK = 0xFFFF0000


def _cparams(*sem):
    return pltpu.CompilerParams(dimension_semantics=sem, vmem_limit_bytes=VMEM_LIMIT)


def _rms(x, g):
    return x * lax.rsqrt(jnp.mean(x * x, axis=-1, keepdims=True) + EPS) * g


def _pack_bf16_pair(lo, hi):
    return lax.bitcast_convert_type(hi, U32) | (lax.bitcast_convert_type(lo, U32) >> 16)


def _unpack_bf16_pair(w):
    lo = lax.bitcast_convert_type(w << 16, F32).astype(BF16)
    hi = lax.bitcast_convert_type(w & jnp.uint32(HI_MASK), F32).astype(BF16)
    return lo, hi


def _mod_kernel(c_ref, w_ref, b_ref, o_ref):
    c = c_ref[...]
    ca = c / (1.0 + jnp.exp(-c))
    o_ref[...] = jnp.dot(ca.astype(BF16), w_ref[...].astype(BF16), preferred_element_type=F32) + b_ref[...]


def _mod_call(c, w_mod, b_mod):
    B, D = c.shape
    n_out = w_mod.shape[1]
    tn = 512
    return pl.pallas_call(
        _mod_kernel,
        out_shape=jax.ShapeDtypeStruct((B, n_out), F32),
        grid=(n_out // tn,),
        in_specs=[pl.BlockSpec((B, D), lambda j: (0, 0)),
                  pl.BlockSpec((D, tn), lambda j: (0, j)),
                  pl.BlockSpec((1, tn), lambda j: (0, j))],
        out_specs=pl.BlockSpec((B, tn), lambda j: (0, j)),
        compiler_params=_cparams("arbitrary"),
        name="mod",
    )(c, w_mod, b_mod.reshape(1, n_out))


def _pre_kernel(x_ref, posr_ref, mod_ref, g_ref, win_ref, wkrT_ref, invfc_ref, wpool_ref, pscale_ref,
                qg_ref, wuqT_ref, kvg_ref, wuk_ref, wuvT_ref,
                qT_ref, k_ref, vT_ref, yp_ref,
                pbuf, b2, b4, b8):
    T = x_ref.shape[0]
    H = POOL_HALO
    i = pl.program_id(1)
    nt = (((1,), (1,)), ((), ()))

    shift = mod_ref[0]
    scale = mod_ref[1]
    hb = (_rms(x_ref[...], g_ref[...]) * (1.0 + scale) + shift).astype(BF16)
    u = jnp.dot(hb, win_ref[...], preferred_element_type=F32)
    krT = lax.dot_general(wkrT_ref[...], hb, nt, preferred_element_type=F32)

    p = u[:, :POOL_WIDTH]

    @pl.when(i == 0)
    def _():
        pbuf[0:H, :] = jnp.zeros((H, POOL_WIDTH), F32)

    pbuf[H:H + T, :] = p
    b2[8:T + H, :] = pbuf[8:T + H, :] + pbuf[7:T + H - 1, :]
    b4[16:T + H, :] = b2[16:T + H, :] + b2[14:T + H - 2, :]
    b8[24:T + H, :] = b4[24:T + H, :] + b4[20:T + H - 4, :]
    s2 = b2[H:T + H, :]
    s4 = b4[H:T + H, :]
    s8 = b8[H:T + H, :]
    s16 = b8[H:T + H, :] + b8[H - 8:T + H - 8, :]
    pbuf[0:H, :] = pbuf[T:T + H, :]

    lane = lax.broadcasted_iota(I32, (T, POOL_WIDTH), 1)
    t1 = (lax.broadcasted_iota(I32, (T, 1), 0) + (i * T + 1)).astype(F32)
    inv2 = 1.0 / jnp.minimum(t1, 2.0)
    inv4 = 1.0 / jnp.minimum(t1, 4.0)
    inv8 = 1.0 / jnp.minimum(t1, 8.0)
    inv16 = 1.0 / jnp.minimum(t1, 16.0)
    mean = jnp.where(lane < 64, s2 * inv2,
                     jnp.where(lane < 128, s4 * inv4,
                               jnp.where(lane < 192, s8 * inv8, s16 * inv16)))
    pooled = mean - p
    yp = jnp.dot(pooled.astype(BF16), wpool_ref[...], preferred_element_type=F32) * pscale_ref[...]
    yp_ref[...] = yp.astype(yp_ref.dtype)

    ang = invfc_ref[...] * posr_ref[...].astype(F32)
    cos_h = jnp.cos(ang)
    sin_h = jnp.sin(ang)
    hr = QK_ROPE_DIM // 2

    def rope(xt):
        x1, x2 = xt[0:hr], xt[hr:]
        return x1 * cos_h - x2 * sin_h, x2 * cos_h + x1 * sin_h

    k_rope = jnp.concatenate(rope(krT), axis=0).T.astype(BF16)

    cq = u[:, POOL_WIDTH:POOL_WIDTH + Q_LORA_RANK]
    ckv = u[:, POOL_WIDTH + Q_LORA_RANK:POOL_WIDTH + Q_LORA_RANK + KV_LORA_RANK]
    cqn = _rms(cq, qg_ref[...]).astype(BF16)
    ckvn = _rms(ckv, kvg_ref[...]).astype(BF16)
    qaT = lax.dot_general(wuqT_ref[...], cqn, nt, preferred_element_type=F32)
    kn = jnp.dot(ckvn, wuk_ref[...], preferred_element_type=F32)
    vT = lax.dot_general(wuvT_ref[...], ckvn, nt, preferred_element_type=F32)
    nq = MLA_HEADS * QK_NOPE_DIM
    ones_rows = jnp.where(lax.broadcasted_iota(I32, (V_ROWS - V_HEAD_DIM, T), 0) == 0, 1.0, 0.0).astype(BF16)
    for hd in range(MLA_HEADS):
        q1, q2 = rope(qaT[nq + hd * QK_ROPE_DIM:nq + (hd + 1) * QK_ROPE_DIM, :])
        qT_ref[hd, 0:QK_NOPE_DIM, :] = qaT[hd * QK_NOPE_DIM:(hd + 1) * QK_NOPE_DIM, :].astype(BF16)
        qT_ref[hd, QK_NOPE_DIM:QK_NOPE_DIM + hr, :] = q1.astype(BF16)
        qT_ref[hd, QK_NOPE_DIM + hr:QK_HEAD_DIM, :] = q2.astype(BF16)
        k_ref[hd, :, 0:QK_NOPE_DIM] = kn[:, hd * QK_NOPE_DIM:(hd + 1) * QK_NOPE_DIM].astype(BF16)
        k_ref[hd, :, QK_NOPE_DIM:QK_HEAD_DIM] = k_rope
        vT_ref[hd, 0:V_HEAD_DIM, :] = vT[hd * V_HEAD_DIM:(hd + 1) * V_HEAD_DIM, :].astype(BF16)
        vT_ref[hd, V_HEAD_DIM:V_ROWS, :] = ones_rows


def _pre_call(x, posr, mod4, norm_g, w_in_main, w_krT, inv_freq_col, wpool_bd, pool_scale, qg, w_uqT, kvg,
              w_uk, w_uvT, T):
    B, S, D = x.shape
    nH = MLA_HEADS
    const = lambda shape: pl.BlockSpec(shape, lambda b, i: (0,) * len(shape))
    return pl.pallas_call(
        _pre_kernel,
        out_shape=(jax.ShapeDtypeStruct((B, nH, QK_HEAD_DIM, S), BF16),
                   jax.ShapeDtypeStruct((B, nH, S, QK_HEAD_DIM), BF16),
                   jax.ShapeDtypeStruct((B, nH, V_ROWS, S), BF16),
                   jax.ShapeDtypeStruct((B, S, POOL_WIDTH), BF16)),
        grid=(B, S // T),
        in_specs=[pl.BlockSpec((None, T, D), lambda b, i: (b, i, 0)),
                  pl.BlockSpec((None, 1, T), lambda b, i: (b, 0, i)),
                  pl.BlockSpec((None, N_MOD, 1, D), lambda b, i: (b, 0, 0, 0)),
                  const((1, D)),
                  const(w_in_main.shape),
                  const(w_krT.shape),
                  const(inv_freq_col.shape),
                  const(wpool_bd.shape),
                  const((1, POOL_WIDTH)),
                  const((1, Q_LORA_RANK)),
                  const(w_uqT.shape),
                  const((1, KV_LORA_RANK)),
                  const(w_uk.shape),
                  const(w_uvT.shape)],
        out_specs=(pl.BlockSpec((None, nH, QK_HEAD_DIM, T), lambda b, i: (b, 0, 0, i)),
                   pl.BlockSpec((None, nH, T, QK_HEAD_DIM), lambda b, i: (b, 0, i, 0)),
                   pl.BlockSpec((None, nH, V_ROWS, T), lambda b, i: (b, 0, 0, i)),
                   pl.BlockSpec((None, T, POOL_WIDTH), lambda b, i: (b, i, 0))),
        scratch_shapes=[pltpu.VMEM((T + POOL_HALO, POOL_WIDTH), F32)] * 4,
        compiler_params=_cparams("arbitrary", "arbitrary"),
        name="pre",
    )(x, posr, mod4, norm_g, w_in_main, w_krT, inv_freq_col, wpool_bd, pool_scale, qg, w_uqT, kvg, w_uk, w_uvT)


def _attn_kernel(qT_ref, k_ref, vT_ref, o_ref):
    S = k_ref.shape[0]
    tq, tk = min(ATTN_TQ, S), ATTN_TK
    diag = lax.broadcasted_iota(I32, (tk, tk), 0) <= lax.broadcasted_iota(I32, (tk, tk), 1)
    blocks = [(i, j) for i in range(S // tq) for j in range((i + 1) * (tq // tk))]

    def scores(i, j):
        d = max(j * tk - i * tq, 0)
        sT = jnp.dot(k_ref[j * tk:(j + 1) * tk, :], qT_ref[:, i * tq + d:(i + 1) * tq],
                     preferred_element_type=F32)
        if j * tk >= i * tq:
            masked = jnp.where(diag, sT[:, :tk], NEG_BIG)
            sT = masked if sT.shape[1] == tk else jnp.concatenate([masked, sT[:, tk:]], axis=1)
        return sT

    s_next = scores(*blocks[0])
    m = acc = None
    for n, (i, j) in enumerate(blocks):
        sT = s_next
        if n + 1 < len(blocks):
            s_next = scores(*blocks[n + 1])
        if j == 0:
            m = jnp.full((1, tq), NEG_BIG, F32)
            acc = jnp.zeros((V_ROWS, tq), F32)
        d = max(j * tk - i * tq, 0)
        vT = vT_ref[:, j * tk:(j + 1) * tk]
        m_new = jnp.maximum(m[:, d:], jnp.max(sT, axis=0, keepdims=True))
        pT = jnp.exp2(sT - m_new).astype(BF16)
        acc_new = jnp.exp2(m[:, d:] - m_new) * acc[:, d:] + jnp.dot(vT, pT, preferred_element_type=F32)
        if d:
            m_new = jnp.concatenate([m[:, :d], m_new], axis=1)
            acc_new = jnp.concatenate([acc[:, :d], acc_new], axis=1)
        m, acc = m_new, acc_new
        if j == (i + 1) * (tq // tk) - 1:
            out = acc[0:V_HEAD_DIM] * (1.0 / acc[V_HEAD_DIM:V_HEAD_DIM + 1])
            o_ref[i * tq:(i + 1) * tq, :] = out.T.astype(o_ref.dtype)


def _attn_call(qT, k, vT):
    B, nH, S, _ = k.shape
    return pl.pallas_call(
        _attn_kernel,
        out_shape=jax.ShapeDtypeStruct((B, S, nH * V_HEAD_DIM), BF16),
        grid=(B, nH),
        in_specs=[pl.BlockSpec((None, None, QK_HEAD_DIM, S), lambda b, h: (b, h, 0, 0)),
                  pl.BlockSpec((None, None, S, QK_HEAD_DIM), lambda b, h: (b, h, 0, 0)),
                  pl.BlockSpec((None, None, V_ROWS, S), lambda b, h: (b, h, 0, 0))],
        out_specs=pl.BlockSpec((None, S, V_HEAD_DIM), lambda b, h: (b, 0, h)),
        compiler_params=_cparams("arbitrary", "arbitrary"),
        name="attn",
    )(qT, k, vT)


def _route_tile(hb, wgT_ref, bg_ref, wrT_ref, br_ref):
    T = hb.shape[0]
    nt = (((1,), (1,)), ((), ()))
    gl = lax.dot_general(wgT_ref[...], hb, nt, preferred_element_type=F32)
    el = lax.dot_general(wrT_ref[...], hb, nt, preferred_element_type=F32)

    r8 = lax.broadcasted_iota(I32, (8, T), 0)
    gvalid = r8 < N_GROUPS
    gmax = jnp.max(jnp.where(gvalid, gl, NEG_BIG), axis=0, keepdims=True)
    gexp = jnp.where(gvalid, jnp.exp(gl - gmax), 0.0)
    g_prob = gexp / jnp.sum(gexp, axis=0, keepdims=True)
    gb = jnp.where(gvalid, gl + bg_ref[...], NEG_BIG)
    gbmax = jnp.max(gb, axis=0, keepdims=True)
    g_sel = jnp.min(jnp.where(gb == gbmax, r8, 8), axis=0, keepdims=True)
    gp = jnp.sum(jnp.where(r8 == g_sel, g_prob, 0.0), axis=0, keepdims=True)

    e_in = jnp.zeros((EXPERTS_PER_GROUP, T), F32)
    b_in = jnp.zeros((EXPERTS_PER_GROUP, T), F32)
    br = br_ref[...]
    for g in range(N_GROUPS):
        sel = g_sel == g
        e_in = jnp.where(sel, el[g * 8:(g + 1) * 8, :], e_in)
        b_in = jnp.where(sel, br[g * 8:(g + 1) * 8, :], b_in)
    eb = e_in + b_in
    m1 = jnp.max(eb, axis=0, keepdims=True)
    i1 = jnp.min(jnp.where(eb == m1, r8, 8), axis=0, keepdims=True)
    eb2 = jnp.where(r8 == i1, NEG_BIG, eb)
    m2 = jnp.max(eb2, axis=0, keepdims=True)
    i2 = jnp.min(jnp.where(eb2 == m2, r8, 8), axis=0, keepdims=True)
    emax = jnp.max(e_in, axis=0, keepdims=True)
    eexp = jnp.exp(e_in - emax)
    sp = eexp / jnp.sum(eexp, axis=0, keepdims=True)
    p1 = jnp.sum(jnp.where(r8 == i1, sp, 0.0), axis=0, keepdims=True)
    p2 = jnp.sum(jnp.where(r8 == i2, sp, 0.0), axis=0, keepdims=True)
    tot = p1 + p2
    w1 = gp * (p1 / tot)
    w2 = gp * (p2 / tot)
    e1 = g_sel * EXPERTS_PER_GROUP + i1
    e2 = g_sel * EXPERTS_PER_GROUP + i2

    r32 = lax.broadcasted_iota(I32, (N_EXPERTS, T), 0)
    oh1 = r32 == e1
    oh2 = r32 == e2
    oh = jnp.where(oh1 | oh2, 1.0, 0.0)
    upper = jnp.where(lax.broadcasted_iota(I32, (T, T), 0) < lax.broadcasted_iota(I32, (T, T), 1),
                      1.0, 0.0).astype(BF16)
    before = jnp.dot(oh.astype(BF16), upper, preferred_element_type=F32)
    cnt = jnp.sum(oh, axis=1, keepdims=True)
    run8 = jnp.floor((cnt + (SUBLANES - 1.0)) * (1.0 / SUBLANES))
    lower = jnp.where(lax.broadcasted_iota(I32, (N_EXPERTS, N_EXPERTS), 1)
                      < lax.broadcasted_iota(I32, (N_EXPERTS, N_EXPERTS), 0), 1.0, 0.0).astype(BF16)
    run_start = jnp.dot(lower, jnp.broadcast_to(run8, (N_EXPERTS, 128)).astype(BF16),
                        preferred_element_type=F32)[:, 0:1] * float(SUBLANES)
    pos = before + run_start
    row1 = jnp.sum(jnp.where(oh1, pos, 0.0), axis=0, keepdims=True)
    row2 = jnp.sum(jnp.where(oh2, pos, 0.0), axis=0, keepdims=True)
    return row1, row2, w1, w2, cnt


def _post_kernel(x_ref, yp_ref, ya_ref, mod_ref, wo_p_ref, wo_a_ref, g_ref,
                 wgT_ref, bg_ref, wrT_ref, br_ref,
                 x1_ref, h2_ref, rows_ref, mf_ref, cnt_ref):
    T = ROW_TILE
    gate_a = mod_ref[2]
    shift_f = mod_ref[3]
    scale_f = mod_ref[4]
    mix = (jnp.dot(yp_ref[...], wo_p_ref[...], preferred_element_type=F32)
           + jnp.dot(ya_ref[...], wo_a_ref[...], preferred_element_type=F32))
    x1 = x_ref[...] + gate_a * mix
    x1_ref[...] = x1
    hb = (_rms(x1, g_ref[...]) * (1.0 + scale_f) + shift_f).astype(BF16)
    h2_ref[...] = hb

    for s in range(x_ref.shape[0] // T):
        cols = slice(s * T, (s + 1) * T)
        row1, row2, w1, w2, cnt = _route_tile(hb[cols], wgT_ref, bg_ref, wrT_ref, br_ref)
        cnt_ref[s] = cnt.astype(I32)
        rows_ref[0:1, cols] = row1.astype(I32)
        rows_ref[1:2, cols] = row2.astype(I32)
        mf_ref[0:1, cols] = row1
        mf_ref[1:2, cols] = row2
        mf_ref[2:3, cols] = w1
        mf_ref[3:4, cols] = w2


def _post_call(x2, yp2, ya2, mod4, wo_p, wo_a, g, wgT, bg, wrT, br, T, steps_per_batch):
    N, D = x2.shape
    const = lambda shape: pl.BlockSpec(shape, lambda i: (0,) * len(shape))
    return pl.pallas_call(
        _post_kernel,
        out_shape=(jax.ShapeDtypeStruct((N, D), F32),
                   jax.ShapeDtypeStruct((N, D), BF16),
                   jax.ShapeDtypeStruct((2, N), I32),
                   jax.ShapeDtypeStruct((4, N), F32),
                   jax.ShapeDtypeStruct((N // ROW_TILE, N_EXPERTS, 1), I32)),
        grid=(N // T,),
        in_specs=[pl.BlockSpec((T, D), lambda i: (i, 0)),
                  pl.BlockSpec((T, POOL_WIDTH), lambda i: (i, 0)),
                  pl.BlockSpec((T, MLA_HEADS * V_HEAD_DIM), lambda i: (i, 0)),
                  pl.BlockSpec((None, N_MOD, 1, D), lambda i: (i // steps_per_batch, 0, 0, 0)),
                  const(wo_p.shape), const(wo_a.shape), const((1, D)),
                  const(wgT.shape), const(bg.shape), const(wrT.shape), const(br.shape)],
        out_specs=(pl.BlockSpec((T, D), lambda i: (i, 0)),
                   pl.BlockSpec((T, D), lambda i: (i, 0)),
                   pl.BlockSpec((2, T), lambda i: (0, i)),
                   pl.BlockSpec((4, T), lambda i: (0, i)),
                   pl.BlockSpec((T // ROW_TILE, N_EXPERTS, 1), lambda i: (i, 0, 0))),
        compiler_params=_cparams("arbitrary"),
        name="post",
    )(x2, yp2, ya2, mod4, wo_p, wo_a, g, wgT, bg, wrT, br)


def _for_each_strip(tile, src_ref, dst_ref, n_ref, fn):
    first = tile * MAX_STRIPS
    for b in range(STRIP_BITS):
        last = first + n_ref[tile * STRIP_BITS + b]

        def body(r, c, n=SUBLANES << b):
            fn(pl.multiple_of(src_ref[r], SUBLANES), pl.multiple_of(dst_ref[r], SUBLANES), n)
            return c

        lax.fori_loop(first, last, body, 0)
        first = last


def _wait_strip_rows(n_rows, make):
    for b in range((LOCAL_ROWS // SUBLANES).bit_length()):
        n = SUBLANES << b

        @pl.when((n_rows & n) != 0)
        def _():
            make(n).wait()


def _dispatch_kernel(src_ref, dst_ref, n_ref, trows_ref, pends_ref, padlen_ref, paddst_ref, rows_ref, h_ref, xs_ref,
                     lbuf, zbuf, sems, zsem):
    T = h_ref.shape[0]
    RB = zbuf.shape[0]
    half = h_ref.shape[1] // 2
    step = pl.program_id(0)
    n_steps = pl.num_programs(0)
    slot = step % DISPATCH_SLOTS

    def zero_fill(act):
        def per_expert(e, c):
            pad = padlen_ref[e]
            dst = paddst_ref[e]
            for b in reversed(range((RB // SUBLANES).bit_length() - 1)):
                n = SUBLANES << b
                done = pad & ~(2 * n - 1)

                @pl.when((pad & n) != 0)
                def _():
                    act(pltpu.make_async_copy(zbuf.at[pl.ds(0, n)],
                                              xs_ref.at[pl.ds(pl.multiple_of(dst + done, SUBLANES), n)], zsem))
            return c

        lax.fori_loop(0, N_EXPERTS, per_expert, 0)

        def per_block(b, c):
            act(pltpu.make_async_copy(zbuf, xs_ref.at[pl.ds(pl.multiple_of(b * RB, RB), RB)], zsem))
            return c

        lax.fori_loop(pends_ref[N_EXPERTS - 1] // RB, xs_ref.shape[0] // RB, per_block, 0)

    @pl.when(step == 0)
    def _():
        zbuf[...] = jnp.zeros(zbuf.shape, zbuf.dtype)
        zero_fill(lambda cp: cp.start())

    r = lax.broadcasted_iota(I32, (LOCAL_ROWS, T), 0)
    perm = jnp.where((r == rows_ref[0:1, :]) | (r == rows_ref[1:2, :]), 1.0, 0.0).astype(BF16)
    h = h_ref[...]
    lo = jnp.dot(perm, h[:, :half], preferred_element_type=F32)
    hi = jnp.dot(perm, h[:, half:], preferred_element_type=F32)
    lbuf[slot] = _pack_bf16_pair(lo, hi)

    def strip(tile_slot):
        def make(loc, dst, n):
            return pltpu.make_async_copy(lbuf.at[tile_slot, pl.ds(loc, n)], xs_ref.at[pl.ds(dst, n)],
                                         sems.at[tile_slot])
        return make

    def wait_tile(tile):
        _wait_strip_rows(trows_ref[tile], lambda n: strip(tile % DISPATCH_SLOTS)(0, 0, n))

    _for_each_strip(step, src_ref, dst_ref, n_ref, lambda *a: strip(slot)(*a).start())

    @pl.when(step >= DISPATCH_SLOTS - 1)
    def _():
        wait_tile(step - (DISPATCH_SLOTS - 1))

    @pl.when(step == n_steps - 1)
    def _():
        for back in reversed(range(DISPATCH_SLOTS - 1)):
            @pl.when(step >= back)
            def _():
                wait_tile(step - back)
        zero_fill(lambda cp: cp.wait())


def _dispatch_call(strip_src, strip_dst, strip_n, tile_rows, pends, pad_len, pad_dst, rows, h2, P_pad, T, RB):
    N, D = h2.shape
    return pl.pallas_call(
        _dispatch_kernel,
        out_shape=jax.ShapeDtypeStruct((P_pad, D // 2), U32),
        grid_spec=pltpu.PrefetchScalarGridSpec(
            num_scalar_prefetch=7,
            grid=(N // T,),
            in_specs=[pl.BlockSpec((2, T), lambda i, *_: (0, i)),
                      pl.BlockSpec((T, D), lambda i, *_: (i, 0))],
            out_specs=pl.BlockSpec(memory_space=pl.ANY),
            scratch_shapes=[pltpu.VMEM((DISPATCH_SLOTS, LOCAL_ROWS, D // 2), U32),
                            pltpu.VMEM((RB, D // 2), U32),
                            pltpu.SemaphoreType.DMA((DISPATCH_SLOTS,)),
                            pltpu.SemaphoreType.DMA]),
        compiler_params=_cparams("arbitrary"),
        name="dispatch",
    )(strip_src, strip_dst, strip_n, tile_rows, pends, pad_len, pad_dst, rows, h2)


def _expert_kernel(be_ref, nused_ref, xs_ref, wgu_ref, wd_ref, ys_ref, wgu_bf, wd_bf):
    i = pl.program_id(0)
    used = i < nused_ref[0]
    half = wgu_ref.shape[0] // 2

    @pl.when(used & ((i == 0) | (be_ref[i] != be_ref[jnp.maximum(i - 1, 0)])))
    def _():
        wgu_bf[...] = wgu_ref[...].astype(BF16)
        wd_bf[...] = wd_ref[...].astype(BF16)

    @pl.when(used)
    def _():
        lo, hi = _unpack_bf16_pair(xs_ref[...])
        gu = (jnp.dot(lo, wgu_bf[0:half, :], preferred_element_type=F32)
              + jnp.dot(hi, wgu_bf[half:, :], preferred_element_type=F32))
        gate = gu[:, :D_EXPERT]
        up = gu[:, D_EXPERT:]
        act = gate / (1.0 + jnp.exp(-gate)) * up
        y = jnp.dot(act.astype(BF16), wd_bf[...], preferred_element_type=F32)
        yb = y.astype(BF16).astype(F32)
        ys_ref[...] = _pack_bf16_pair(yb[:, :half], yb[:, half:])

    @pl.when(jnp.logical_not(used))
    def _():
        ys_ref[...] = jnp.zeros(ys_ref.shape, ys_ref.dtype)


def _expert_call(block_e, n_used, xs, wgu, wd, RB):
    P, Dh = xs.shape
    D = 2 * Dh
    row_map = lambda i, be, nu: (i, 0)
    in_map = lambda i, be, nu: (jnp.minimum(i, nu[0] - 1), 0)
    return pl.pallas_call(
        _expert_kernel,
        out_shape=jax.ShapeDtypeStruct((P, Dh), U32),
        grid_spec=pltpu.PrefetchScalarGridSpec(
            num_scalar_prefetch=2,
            grid=(P // RB,),
            in_specs=[pl.BlockSpec((RB, Dh), in_map),
                      pl.BlockSpec((None, D, 2 * D_EXPERT), lambda i, be, nu: (be[i], 0, 0)),
                      pl.BlockSpec((None, D_EXPERT, D), lambda i, be, nu: (be[i], 0, 0))],
            out_specs=pl.BlockSpec((RB, Dh), row_map),
            scratch_shapes=[pltpu.VMEM((D, 2 * D_EXPERT), BF16), pltpu.VMEM((D_EXPERT, D), BF16)]),
        compiler_params=_cparams("arbitrary"),
        name="experts",
    )(block_e, n_used, xs, wgu, wd)


def _combine_kernel(src_ref, dst_ref, n_ref, trows_ref, x1_ref, tok_ref, mod_ref, g_ref, ys_ref, o_ref, ybuf, sems):
    T = x1_ref.shape[0]
    step = pl.program_id(0)
    n_steps = pl.num_programs(0)
    slot = step % 2

    def strip(tile_slot):
        def make(loc, dst, n):
            return pltpu.make_async_copy(ys_ref.at[pl.ds(dst, n)], ybuf.at[tile_slot, pl.ds(loc, n)],
                                         sems.at[tile_slot])
        return make

    @pl.when(step == 0)
    def _():
        ybuf[...] = jnp.zeros(ybuf.shape, ybuf.dtype)
        _for_each_strip(step, src_ref, dst_ref, n_ref, lambda *a: strip(slot)(*a).start())

    @pl.when(step + 1 < n_steps)
    def _():
        _for_each_strip(step + 1, src_ref, dst_ref, n_ref, lambda *a: strip(1 - slot)(*a).start())

    _wait_strip_rows(trows_ref[step], lambda n: strip(slot)(0, 0, n))

    tok = tok_ref[...]
    lo, hi = _unpack_bf16_pair(ybuf[slot])
    n_chunks = 4
    tc = T // n_chunks

    def permw_chunk(ci):
        t = tok[ci * tc:(ci + 1) * tc]
        rows = t[:, 0:2].astype(I32)
        c = lax.broadcasted_iota(I32, (tc, LOCAL_ROWS), 1)
        return jnp.where(c == rows[:, 0:1], t[:, 2:3], jnp.where(c == rows[:, 1:2], t[:, 3:4], 0.0)).astype(BF16)

    p_next = permw_chunk(0)
    parts = []
    for ci in range(n_chunks):
        permw = p_next
        if ci + 1 < n_chunks:
            p_next = permw_chunk(ci + 1)
        parts.append(jnp.concatenate([jnp.dot(permw, lo, preferred_element_type=F32),
                                      jnp.dot(permw, hi, preferred_element_type=F32)], axis=1))
    moe = jnp.concatenate(parts, axis=0)
    gate_f = mod_ref[5]
    o_ref[...] = _rms(x1_ref[...] + gate_f * moe, g_ref[...])


def _combine_call(strip_src, strip_dst, strip_n, tile_rows, x1, tok, mod4, final_g, ys, T, tiles_per_batch):
    N, D = x1.shape
    return pl.pallas_call(
        _combine_kernel,
        out_shape=jax.ShapeDtypeStruct((N, D), F32),
        grid_spec=pltpu.PrefetchScalarGridSpec(
            num_scalar_prefetch=4,
            grid=(N // T,),
            in_specs=[pl.BlockSpec((T, D), lambda i, *_: (i, 0)),
                      pl.BlockSpec((T, 4), lambda i, *_: (i, 0)),
                      pl.BlockSpec((None, N_MOD, 1, D), lambda i, *_: (i // tiles_per_batch, 0, 0, 0)),
                      pl.BlockSpec((1, D), lambda i, *_: (0, 0)),
                      pl.BlockSpec(memory_space=pl.ANY)],
            out_specs=pl.BlockSpec((T, D), lambda i, *_: (i, 0)),
            scratch_shapes=[pltpu.VMEM((2, LOCAL_ROWS, D // 2), U32), pltpu.SemaphoreType.DMA((2,))]),
        compiler_params=_cparams("arbitrary"),
        name="combine",
    )(strip_src, strip_dst, strip_n, tile_rows, x1, tok, mod4, final_g, ys)


def _round_up(v, m):
    return (v + m - 1) // m * m


def kernel(x, c, positions, w_mod, b_mod, norm_mix_g, w_in, w_pool, pool_scale, q_norm_g, w_uq, kv_norm_g, w_ukv, w_o, norm_ffn_g, w_group, b_group, w_router, b_router, w_gate_up, w_down, final_g):
    B, S, D = x.shape
    N = B * S
    depth = w_mod.shape[0]
    T = ROW_TILE
    RB = EXPERT_ROWS
    assert depth == 1, "the final RMSNorm is fused into the layer's combine step"
    assert S % T == 0 and S % min(POST_TILE, S) == 0 and min(POST_TILE, S) % T == 0 and S % min(PRE_TILE, S) == 0 and S % min(ATTN_TQ, S) == 0 and min(ATTN_TQ, S) % ATTN_TK == 0
    tiles_per_batch = S // T
    post_tile = min(POST_TILE, S)
    n_tiles = N // T
    nH = MLA_HEADS
    l = 0

    inv_freq = ROPE_THETA ** (-(jnp.arange(0, QK_ROPE_DIM, 2, dtype=F32) / QK_ROPE_DIM))
    posr = positions.reshape(B, 1, S)
    cut1 = POOL_WIDTH
    cut2 = cut1 + Q_LORA_RANK
    cut3 = cut2 + KV_LORA_RANK

    mod4 = _mod_call(c, w_mod[l], b_mod[l]).reshape(B, N_MOD, 1, D)

    wi = w_in[l]
    w_in_main = wi[:, :cut3].astype(BF16)
    w_krT = wi[:, cut3:].T.astype(BF16)
    wq = w_uq[l].reshape(Q_LORA_RANK, nH, QK_HEAD_DIM)
    wq_n = wq[:, :, :QK_NOPE_DIM].reshape(Q_LORA_RANK, nH * QK_NOPE_DIM)
    wq_r = wq[:, :, QK_NOPE_DIM:]
    w_uqT = jnp.concatenate([wq_n, wq_r.reshape(Q_LORA_RANK, nH * QK_ROPE_DIM)], axis=1).T.astype(BF16)
    wkv = w_ukv[l].reshape(KV_LORA_RANK, nH, QK_NOPE_DIM + V_HEAD_DIM)
    w_uk = wkv[:, :, :QK_NOPE_DIM].reshape(KV_LORA_RANK, nH * QK_NOPE_DIM).astype(BF16)
    w_uvT = wkv[:, :, QK_NOPE_DIM:].reshape(KV_LORA_RANK, nH * V_HEAD_DIM).T.astype(BF16)
    qg = (q_norm_g[l] * (QK_HEAD_DIM ** -0.5 * LOG2_E)).reshape(1, Q_LORA_RANK)
    wpool_bd = jnp.zeros((POOL_WIDTH, POOL_WIDTH), F32)
    for g in range(len(POOL_WINDOWS)):
        sl = slice(g * POOL_GROUP_DIM, (g + 1) * POOL_GROUP_DIM)
        wpool_bd = wpool_bd.at[sl, sl].set(w_pool[l, g])
    wpool_bd = wpool_bd.astype(BF16)

    qT, k, vT, yp = _pre_call(
        x, posr, mod4, norm_mix_g[l].reshape(1, D), w_in_main, w_krT, inv_freq.reshape(QK_ROPE_DIM // 2, 1),
        wpool_bd, pool_scale[l].reshape(1, POOL_WIDTH), qg, w_uqT, kv_norm_g[l].reshape(1, KV_LORA_RANK),
        w_uk, w_uvT, min(PRE_TILE, S))
    ya = _attn_call(qT, k, vT)

    wo = w_o[l].astype(BF16)
    wgT = jnp.zeros((8, D), F32).at[:N_GROUPS].set(w_group[l].T).astype(BF16)
    bg = jnp.zeros((8, 1), F32).at[:N_GROUPS, 0].set(b_group[l])
    x1, h2, rows, meta_f, tile_cnt = _post_call(
        x.reshape(N, D), yp.reshape(N, POOL_WIDTH), ya.reshape(N, nH * V_HEAD_DIM), mod4,
        wo[:POOL_WIDTH], wo[POOL_WIDTH:], norm_ffn_g[l].reshape(1, D),
        wgT, bg, w_router[l].T.astype(BF16), b_router[l].reshape(N_EXPERTS, 1), post_tile, S // post_tile)

    run_len = _round_up(tile_cnt[:, :, 0], SUBLANES)
    run_loc = jnp.cumsum(run_len, axis=1) - run_len
    seg_rows = jnp.sum(run_len, axis=0)
    seg_len = _round_up(seg_rows, RB)
    pends = jnp.cumsum(seg_len).astype(I32)
    pstarts = pends - seg_len
    run_dst = pstarts[None, :] + jnp.cumsum(run_len, axis=0) - run_len
    pad_len = (seg_len - seg_rows).astype(I32)
    pad_dst = (pstarts + seg_rows).astype(I32)
    bits = jnp.arange(STRIP_BITS, dtype=I32)
    size = SUBLANES << bits
    valid = (run_len[:, None, :] & size[None, :, None]) != 0
    done = run_len[:, None, :] & ~(2 * size[None, :, None] - 1)
    flat = lambda a: a.reshape(n_tiles, STRIP_BITS * N_EXPERTS)
    valid_f = flat(valid)
    pos = jnp.cumsum(valid_f.astype(I32), axis=1) - valid_f.astype(I32)
    pick = valid_f[:, None, :] & (pos[:, None, :] == jnp.arange(MAX_STRIPS, dtype=I32)[None, :, None])
    gather = lambda a: jnp.sum(jnp.where(pick, flat(a)[:, None, :], 0), axis=-1).reshape(-1).astype(I32)
    strip_src = gather(run_loc[:, None, :] + done)
    strip_dst = gather(run_dst[:, None, :] + done)
    strip_n = jnp.sum(valid, axis=2).reshape(-1).astype(I32)
    tile_rows = jnp.sum(run_len, axis=1).astype(I32)
    P_pad = _round_up(2 * N + n_tiles * N_EXPERTS * (SUBLANES - 1), RB) + N_EXPERTS * RB
    n_rb = P_pad // RB
    n_used = (pends[-1:] // RB).astype(I32)
    block_start = jnp.minimum(jnp.arange(n_rb, dtype=I32), n_used - 1) * RB
    block_e = jnp.sum((pends[None, :] <= block_start[:, None]).astype(I32), axis=1)

    xs = _dispatch_call(strip_src, strip_dst, strip_n, tile_rows, pends, pad_len, pad_dst, rows, h2, P_pad, T, RB)
    ys = _expert_call(block_e, n_used, xs, w_gate_up[l], w_down[l], RB)
    out = _combine_call(strip_src, strip_dst, strip_n, tile_rows, x1, meta_f.T, mod4, final_g.reshape(1, D), ys,
                        T, tiles_per_batch)
    return out.reshape(B, S, D)
```

```python
import jax
import jax.numpy as jnp
from jax import lax
from jax.experimental import pallas as pl
from jax.experimental.pallas import tpu as pltpu

F32 = jnp.float32
BF16 = jnp.bfloat16
U32 = jnp.uint32
I32 = jnp.int32

POOL_WINDOWS = (2, 4, 8, 16)
POOL_GROUP_DIM = 64
POOL_WIDTH = 256
MLA_HEADS = 6
QK_NOPE_DIM = 128
QK_ROPE_DIM = 64
QK_HEAD_DIM = QK_NOPE_DIM + QK_ROPE_DIM
V_HEAD_DIM = 128
V_ROWS = 144
Q_LORA_RANK = 512
KV_LORA_RANK = 256
ROPE_THETA = 10000.0
N_GROUPS = 4
EXPERTS_PER_GROUP = 8
N_EXPERTS = N_GROUPS * EXPERTS_PER_GROUP
D_EXPERT = 256
N_MOD = 6
EPS = 1e-6

SUBLANES = 8
POOL_HALO = 32
ROW_TILE = 512
PRE_TILE = 1024
POST_TILE = 1024
ATTN_TQ = 2048
ATTN_TK = 512
EXPERT_ROWS = 1024
LOCAL_ROWS = 2 * ROW_TILE + 256
assert LOCAL_ROWS >= 2 * ROW_TILE + N_EXPERTS * (SUBLANES - 1)
STRIP_BITS = (2 * ROW_TILE // SUBLANES).bit_length()
MAX_STRIPS = 128
assert MAX_STRIPS >= 2 * N_EXPERTS + (LOCAL_ROWS // SUBLANES - 3 * N_EXPERTS) // 4
DISPATCH_SLOTS = 2
VMEM_LIMIT = 56 * 1024 * 1024
NEG_BIG = -1e30
LOG2_E = 1.4426950408889634
HI_MASK = 0xFFFF0000


def _cparams(*sem):
    return pltpu.CompilerParams(dimension_semantics=sem, vmem_limit_bytes=VMEM_LIMIT)


def _rms(x, g):
    return x * lax.rsqrt(jnp.mean(x * x, axis=-1, keepdims=True) + EPS) * g


def _pack_bf16_pair(lo, hi):
    return lax.bitcast_convert_type(hi, U32) | (lax.bitcast_convert_type(lo, U32) >> 16)


def _unpack_bf16_pair(w):
    lo = lax.bitcast_convert_type(w << 16, F32).astype(BF16)
    hi = lax.bitcast_convert_type(w & jnp.uint32(HI_MASK), F32).astype(BF16)
    return lo, hi


def _mod_kernel(c_ref, w_ref, b_ref, o_ref):
    c = c_ref[...]
    ca = c / (1.0 + jnp.exp(-c))
    o_ref[...] = jnp.dot(ca.astype(BF16), w_ref[...].astype(BF16), preferred_element_type=F32) + b_ref[...]


def _mod_call(c, w_mod, b_mod):
    B, D = c.shape
    n_out = w_mod.shape[1]
    tn = 512
    return pl.pallas_call(
        _mod_kernel,
        out_shape=jax.ShapeDtypeStruct((B, n_out), F32),
        grid=(n_out // tn,),
        in_specs=[pl.BlockSpec((B, D), lambda j: (0, 0)),
                  pl.BlockSpec((D, tn), lambda j: (0, j)),
                  pl.BlockSpec((1, tn), lambda j: (0, j))],
        out_specs=pl.BlockSpec((B, tn), lambda j: (0, j)),
        compiler_params=_cparams("arbitrary"),
        name="mod",
    )(c, w_mod, b_mod.reshape(1, n_out))


def _pre_kernel(x_ref, posr_ref, mod_ref, g_ref, win_ref, wkrT_ref, invfc_ref, wpool_ref, pscale_ref,
                qg_ref, wuqT_ref, kvg_ref, wuk_ref, wuvT_ref,
                qT_ref, k_ref, vT_ref, yp_ref,
                pbuf, b2, b4, b8):
    T = x_ref.shape[0]
    H = POOL_HALO
    i = pl.program_id(1)
    nt = (((1,), (1,)), ((), ()))

    shift = mod_ref[0]
    scale = mod_ref[1]
    hb = (_rms(x_ref[...], g_ref[...]) * (1.0 + scale) + shift).astype(BF16)
    u = jnp.dot(hb, win_ref[...], preferred_element_type=F32)
    krT = lax.dot_general(wkrT_ref[...], hb, nt, preferred_element_type=F32)

    p = u[:, :POOL_WIDTH]

    @pl.when(i == 0)
    def _():
        pbuf[0:H, :] = jnp.zeros((H, POOL_WIDTH), F32)

    pbuf[H:H + T, :] = p
    b2[8:T + H, :] = pbuf[8:T + H, :] + pbuf[7:T + H - 1, :]
    b4[16:T + H, :] = b2[16:T + H, :] + b2[14:T + H - 2, :]
    b8[24:T + H, :] = b4[24:T + H, :] + b4[20:T + H - 4, :]
    s2 = b2[H:T + H, :]
    s4 = b4[H:T + H, :]
    s8 = b8[H:T + H, :]
    s16 = b8[H:T + H, :] + b8[H - 8:T + H - 8, :]
    pbuf[0:H, :] = pbuf[T:T + H, :]

    lane = lax.broadcasted_iota(I32, (T, POOL_WIDTH), 1)
    t1 = (lax.broadcasted_iota(I32, (T, 1), 0) + (i * T + 1)).astype(F32)
    inv2 = 1.0 / jnp.minimum(t1, 2.0)
    inv4 = 1.0 / jnp.minimum(t1, 4.0)
    inv8 = 1.0 / jnp.minimum(t1, 8.0)
    inv16 = 1.0 / jnp.minimum(t1, 16.0)
    mean = jnp.where(lane < 64, s2 * inv2,
                     jnp.where(lane < 128, s4 * inv4,
                               jnp.where(lane < 192, s8 * inv8, s16 * inv16)))
    pooled = mean - p
    yp = jnp.dot(pooled.astype(BF16), wpool_ref[...], preferred_element_type=F32) * pscale_ref[...]
    yp_ref[...] = yp.astype(yp_ref.dtype)

    ang = invfc_ref[...] * posr_ref[...].astype(F32)
    cos_h = jnp.cos(ang)
    sin_h = jnp.sin(ang)
    hr = QK_ROPE_DIM // 2

    def rope(xt):
        x1, x2 = xt[0:hr], xt[hr:]
        return x1 * cos_h - x2 * sin_h, x2 * cos_h + x1 * sin_h

    k_rope = jnp.concatenate(rope(krT), axis=0).T.astype(BF16)

    cq = u[:, POOL_WIDTH:POOL_WIDTH + Q_LORA_RANK]
    ckv = u[:, POOL_WIDTH + Q_LORA_RANK:POOL_WIDTH + Q_LORA_RANK + KV_LORA_RANK]
    cqn = _rms(cq, qg_ref[...]).astype(BF16)
    ckvn = _rms(ckv, kvg_ref[...]).astype(BF16)
    qaT = lax.dot_general(wuqT_ref[...], cqn, nt, preferred_element_type=F32)
    kn = jnp.dot(ckvn, wuk_ref[...], preferred_element_type=F32)
    vT = lax.dot_general(wuvT_ref[...], ckvn, nt, preferred_element_type=F32)
    nq = MLA_HEADS * QK_NOPE_DIM
    ones_rows = jnp.where(lax.broadcasted_iota(I32, (V_ROWS - V_HEAD_DIM, T), 0) == 0, 1.0, 0.0).astype(BF16)
    for hd in range(MLA_HEADS):
        q1, q2 = rope(qaT[nq + hd * QK_ROPE_DIM:nq + (hd + 1) * QK_ROPE_DIM, :])
        qT_ref[hd, 0:QK_NOPE_DIM, :] = qaT[hd * QK_NOPE_DIM:(hd + 1) * QK_NOPE_DIM, :].astype(BF16)
        qT_ref[hd, QK_NOPE_DIM:QK_NOPE_DIM + hr, :] = q1.astype(BF16)
        qT_ref[hd, QK_NOPE_DIM + hr:QK_HEAD_DIM, :] = q2.astype(BF16)
        k_ref[hd, :, 0:QK_NOPE_DIM] = kn[:, hd * QK_NOPE_DIM:(hd + 1) * QK_NOPE_DIM].astype(BF16)
        k_ref[hd, :, QK_NOPE_DIM:QK_HEAD_DIM] = k_rope
        vT_ref[hd, 0:V_HEAD_DIM, :] = vT[hd * V_HEAD_DIM:(hd + 1) * V_HEAD_DIM, :].astype(BF16)
        vT_ref[hd, V_HEAD_DIM:V_ROWS, :] = ones_rows


def _pre_call(x, posr, mod4, norm_g, w_in_main, w_krT, inv_freq_col, wpool_bd, pool_scale, qg, w_uqT, kvg,
              w_uk, w_uvT, T):
    B, S, D = x.shape
    nH = MLA_HEADS
    const = lambda shape: pl.BlockSpec(shape, lambda b, i: (0,) * len(shape))
    return pl.pallas_call(
        _pre_kernel,
        out_shape=(jax.ShapeDtypeStruct((B, nH, QK_HEAD_DIM, S), BF16),
                   jax.ShapeDtypeStruct((B, nH, S, QK_HEAD_DIM), BF16),
                   jax.ShapeDtypeStruct((B, nH, V_ROWS, S), BF16),
                   jax.ShapeDtypeStruct((B, S, POOL_WIDTH), BF16)),
        grid=(B, S // T),
        in_specs=[pl.BlockSpec((None, T, D), lambda b, i: (b, i, 0)),
                  pl.BlockSpec((None, 1, T), lambda b, i: (b, 0, i)),
                  pl.BlockSpec((None, N_MOD, 1, D), lambda b, i: (b, 0, 0, 0)),
                  const((1, D)),
                  const(w_in_main.shape),
                  const(w_krT.shape),
                  const(inv_freq_col.shape),
                  const(wpool_bd.shape),
                  const((1, POOL_WIDTH)),
                  const((1, Q_LORA_RANK)),
                  const(w_uqT.shape),
                  const((1, KV_LORA_RANK)),
                  const(w_uk.shape),
                  const(w_uvT.shape)],
        out_specs=(pl.BlockSpec((None, nH, QK_HEAD_DIM, T), lambda b, i: (b, 0, 0, i)),
                   pl.BlockSpec((None, nH, T, QK_HEAD_DIM), lambda b, i: (b, 0, i, 0)),
                   pl.BlockSpec((None, nH, V_ROWS, T), lambda b, i: (b, 0, 0, i)),
                   pl.BlockSpec((None, T, POOL_WIDTH), lambda b, i: (b, i, 0))),
        scratch_shapes=[pltpu.VMEM((T + POOL_HALO, POOL_WIDTH), F32)] * 4,
        compiler_params=_cparams("arbitrary", "arbitrary"),
        name="pre",
    )(x, posr, mod4, norm_g, w_in_main, w_krT, inv_freq_col, wpool_bd, pool_scale, qg, w_uqT, kvg, w_uk, w_uvT)


def _attn_kernel(qT_ref, k_ref, vT_ref, o_ref):
    S = k_ref.shape[0]
    tq, tk = min(ATTN_TQ, S), ATTN_TK
    diag = lax.broadcasted_iota(I32, (tk, tk), 0) <= lax.broadcasted_iota(I32, (tk, tk), 1)
    blocks = [(i, j) for i in range(S // tq) for j in range((i + 1) * (tq // tk))]

    def scores(i, j):
        d = max(j * tk - i * tq, 0)
        sT = jnp.dot(k_ref[j * tk:(j + 1) * tk, :], qT_ref[:, i * tq + d:(i + 1) * tq],
                     preferred_element_type=F32)
        if j * tk >= i * tq:
            masked = jnp.where(diag, sT[:, :tk], NEG_BIG)
            sT = masked if sT.shape[1] == tk else jnp.concatenate([masked, sT[:, tk:]], axis=1)
        return sT

    s_next = scores(*blocks[0])
    m = acc = None
    for n, (i, j) in enumerate(blocks):
        sT = s_next
        if n + 1 < len(blocks):
            s_next = scores(*blocks[n + 1])
        if j == 0:
            m = jnp.full((1, tq), NEG_BIG, F32)
            acc = jnp.zeros((V_ROWS, tq), F32)
        d = max(j * tk - i * tq, 0)
        vT = vT_ref[:, j * tk:(j + 1) * tk]
        m_new = jnp.maximum(m[:, d:], jnp.max(sT, axis=0, keepdims=True))
        pT = jnp.exp2(sT - m_new).astype(BF16)
        acc_new = jnp.exp2(m[:, d:] - m_new) * acc[:, d:] + jnp.dot(vT, pT, preferred_element_type=F32)
        if d:
            m_new = jnp.concatenate([m[:, :d], m_new], axis=1)
            acc_new = jnp.concatenate([acc[:, :d], acc_new], axis=1)
        m, acc = m_new, acc_new
        if j == (i + 1) * (tq // tk) - 1:
            out = acc[0:V_HEAD_DIM] * (1.0 / acc[V_HEAD_DIM:V_HEAD_DIM + 1])
            o_ref[i * tq:(i + 1) * tq, :] = out.T.astype(o_ref.dtype)


def _attn_call(qT, k, vT):
    B, nH, S, _ = k.shape
    return pl.pallas_call(
        _attn_kernel,
        out_shape=jax.ShapeDtypeStruct((B, S, nH * V_HEAD_DIM), BF16),
        grid=(B, nH),
        in_specs=[pl.BlockSpec((None, None, QK_HEAD_DIM, S), lambda b, h: (b, h, 0, 0)),
                  pl.BlockSpec((None, None, S, QK_HEAD_DIM), lambda b, h: (b, h, 0, 0)),
                  pl.BlockSpec((None, None, V_ROWS, S), lambda b, h: (b, h, 0, 0))],
        out_specs=pl.BlockSpec((None, S, V_HEAD_DIM), lambda b, h: (b, 0, h)),
        compiler_params=_cparams("arbitrary", "arbitrary"),
        name="attn",
    )(qT, k, vT)


def _route_tile(hb, wgT_ref, bg_ref, wrT_ref, br_ref):
    T = hb.shape[0]
    nt = (((1,), (1,)), ((), ()))
    gl = lax.dot_general(wgT_ref[...], hb, nt, preferred_element_type=F32)
    el = lax.dot_general(wrT_ref[...], hb, nt, preferred_element_type=F32)

    r8 = lax.broadcasted_iota(I32, (8, T), 0)
    gvalid = r8 < N_GROUPS
    gmax = jnp.max(jnp.where(gvalid, gl, NEG_BIG), axis=0, keepdims=True)
    gexp = jnp.where(gvalid, jnp.exp(gl - gmax), 0.0)
    g_prob = gexp / jnp.sum(gexp, axis=0, keepdims=True)
    gb = jnp.where(gvalid, gl + bg_ref[...], NEG_BIG)
    gbmax = jnp.max(gb, axis=0, keepdims=True)
    g_sel = jnp.min(jnp.where(gb == gbmax, r8, 8), axis=0, keepdims=True)
    gp = jnp.sum(jnp.where(r8 == g_sel, g_prob, 0.0), axis=0, keepdims=True)

    e_in = jnp.zeros((EXPERTS_PER_GROUP, T), F32)
    b_in = jnp.zeros((EXPERTS_PER_GROUP, T), F32)
    br = br_ref[...]
    for g in range(N_GROUPS):
        sel = g_sel == g
        e_in = jnp.where(sel, el[g * 8:(g + 1) * 8, :], e_in)
        b_in = jnp.where(sel, br[g * 8:(g + 1) * 8, :], b_in)
    eb = e_in + b_in
    m1 = jnp.max(eb, axis=0, keepdims=True)
    i1 = jnp.min(jnp.where(eb == m1, r8, 8), axis=0, keepdims=True)
    eb2 = jnp.where(r8 == i1, NEG_BIG, eb)
    m2 = jnp.max(eb2, axis=0, keepdims=True)
    i2 = jnp.min(jnp.where(eb2 == m2, r8, 8), axis=0, keepdims=True)
    emax = jnp.max(e_in, axis=0, keepdims=True)
    eexp = jnp.exp(e_in - emax)
    sp = eexp / jnp.sum(eexp, axis=0, keepdims=True)
    p1 = jnp.sum(jnp.where(r8 == i1, sp, 0.0), axis=0, keepdims=True)
    p2 = jnp.sum(jnp.where(r8 == i2, sp, 0.0), axis=0, keepdims=True)
    tot = p1 + p2
    w1 = gp * (p1 / tot)
    w2 = gp * (p2 / tot)
    e1 = g_sel * EXPERTS_PER_GROUP + i1
    e2 = g_sel * EXPERTS_PER_GROUP + i2

    r32 = lax.broadcasted_iota(I32, (N_EXPERTS, T), 0)
    oh1 = r32 == e1
    oh2 = r32 == e2
    oh = jnp.where(oh1 | oh2, 1.0, 0.0)
    upper = jnp.where(lax.broadcasted_iota(I32, (T, T), 0) < lax.broadcasted_iota(I32, (T, T), 1),
                      1.0, 0.0).astype(BF16)
    before = jnp.dot(oh.astype(BF16), upper, preferred_element_type=F32)
    cnt = jnp.sum(oh, axis=1, keepdims=True)
    run8 = jnp.floor((cnt + (SUBLANES - 1.0)) * (1.0 / SUBLANES))
    lower = jnp.where(lax.broadcasted_iota(I32, (N_EXPERTS, N_EXPERTS), 1)
                      < lax.broadcasted_iota(I32, (N_EXPERTS, N_EXPERTS), 0), 1.0, 0.0).astype(BF16)
    run_start = jnp.dot(lower, jnp.broadcast_to(run8, (N_EXPERTS, 128)).astype(BF16),
                        preferred_element_type=F32)[:, 0:1] * float(SUBLANES)
    pos = before + run_start
    row1 = jnp.sum(jnp.where(oh1, pos, 0.0), axis=0, keepdims=True)
    row2 = jnp.sum(jnp.where(oh2, pos, 0.0), axis=0, keepdims=True)
    return row1, row2, w1, w2, cnt


def _post_kernel(x_ref, yp_ref, ya_ref, mod_ref, wo_p_ref, wo_a_ref, g_ref,
                 wgT_ref, bg_ref, wrT_ref, br_ref,
                 x1_ref, h2_ref, rows_ref, mf_ref, cnt_ref):
    T = ROW_TILE
    gate_a = mod_ref[2]
    shift_f = mod_ref[3]
    scale_f = mod_ref[4]
    mix = (jnp.dot(yp_ref[...], wo_p_ref[...], preferred_element_type=F32)
           + jnp.dot(ya_ref[...], wo_a_ref[...], preferred_element_type=F32))
    x1 = x_ref[...] + gate_a * mix
    x1_ref[...] = x1
    hb = (_rms(x1, g_ref[...]) * (1.0 + scale_f) + shift_f).astype(BF16)
    h2_ref[...] = hb

    for s in range(x_ref.shape[0] // T):
        cols = slice(s * T, (s + 1) * T)
        row1, row2, w1, w2, cnt = _route_tile(hb[cols], wgT_ref, bg_ref, wrT_ref, br_ref)
        cnt_ref[s] = cnt.astype(I32)
        rows_ref[0:1, cols] = row1.astype(I32)
        rows_ref[1:2, cols] = row2.astype(I32)
        mf_ref[0:1, cols] = row1
        mf_ref[1:2, cols] = row2
        mf_ref[2:3, cols] = w1
        mf_ref[3:4, cols] = w2


def _post_call(x2, yp2, ya2, mod4, wo_p, wo_a, g, wgT, bg, wrT, br, T, steps_per_batch):
    N, D = x2.shape
    const = lambda shape: pl.BlockSpec(shape, lambda i: (0,) * len(shape))
    return pl.pallas_call(
        _post_kernel,
        out_shape=(jax.ShapeDtypeStruct((N, D), F32),
                   jax.ShapeDtypeStruct((N, D), BF16),
                   jax.ShapeDtypeStruct((2, N), I32),
                   jax.ShapeDtypeStruct((4, N), F32),
                   jax.ShapeDtypeStruct((N // ROW_TILE, N_EXPERTS, 1), I32)),
        grid=(N // T,),
        in_specs=[pl.BlockSpec((T, D), lambda i: (i, 0)),
                  pl.BlockSpec((T, POOL_WIDTH), lambda i: (i, 0)),
                  pl.BlockSpec((T, MLA_HEADS * V_HEAD_DIM), lambda i: (i, 0)),
                  pl.BlockSpec((None, N_MOD, 1, D), lambda i: (i // steps_per_batch, 0, 0, 0)),
                  const(wo_p.shape), const(wo_a.shape), const((1, D)),
                  const(wgT.shape), const(bg.shape), const(wrT.shape), const(br.shape)],
        out_specs=(pl.BlockSpec((T, D), lambda i: (i, 0)),
                   pl.BlockSpec((T, D), lambda i: (i, 0)),
                   pl.BlockSpec((2, T), lambda i: (0, i)),
                   pl.BlockSpec((4, T), lambda i: (0, i)),
                   pl.BlockSpec((T // ROW_TILE, N_EXPERTS, 1), lambda i: (i, 0, 0))),
        compiler_params=_cparams("arbitrary"),
        name="post",
    )(x2, yp2, ya2, mod4, wo_p, wo_a, g, wgT, bg, wrT, br)


def _for_each_strip(tile, src_ref, dst_ref, n_ref, fn):
    first = tile * MAX_STRIPS
    for b in range(STRIP_BITS):
        last = first + n_ref[tile * STRIP_BITS + b]

        def body(r, c, n=SUBLANES << b):
            fn(pl.multiple_of(src_ref[r], SUBLANES), pl.multiple_of(dst_ref[r], SUBLANES), n)
            return c

        lax.fori_loop(first, last, body, 0)
        first = last


def _wait_strip_rows(n_rows, make):
    for b in range((LOCAL_ROWS // SUBLANES).bit_length()):
        n = SUBLANES << b

        @pl.when((n_rows & n) != 0)
        def _():
            make(n).wait()


def _dispatch_kernel(src_ref, dst_ref, n_ref, trows_ref, pends_ref, padlen_ref, paddst_ref, rows_ref, h_ref, xs_ref,
                     lbuf, zbuf, sems, zsem):
    T = h_ref.shape[0]
    RB = zbuf.shape[0]
    half = h_ref.shape[1] // 2
    step = pl.program_id(0)
    n_steps = pl.num_programs(0)
    slot = step % DISPATCH_SLOTS

    def zero_fill(act):
        def per_expert(e, c):
            pad = padlen_ref[e]
            dst = paddst_ref[e]
            for b in reversed(range((RB // SUBLANES).bit_length() - 1)):
                n = SUBLANES << b
                done = pad & ~(2 * n - 1)

                @pl.when((pad & n) != 0)
                def _():
                    act(pltpu.make_async_copy(zbuf.at[pl.ds(0, n)],
                                              xs_ref.at[pl.ds(pl.multiple_of(dst + done, SUBLANES), n)], zsem))
            return c

        lax.fori_loop(0, N_EXPERTS, per_expert, 0)

        def per_block(b, c):
            act(pltpu.make_async_copy(zbuf, xs_ref.at[pl.ds(pl.multiple_of(b * RB, RB), RB)], zsem))
            return c

        lax.fori_loop(pends_ref[N_EXPERTS - 1] // RB, xs_ref.shape[0] // RB, per_block, 0)

    @pl.when(step == 0)
    def _():
        zbuf[...] = jnp.zeros(zbuf.shape, zbuf.dtype)
        zero_fill(lambda cp: cp.start())

    r = lax.broadcasted_iota(I32, (LOCAL_ROWS, T), 0)
    perm = jnp.where((r == rows_ref[0:1, :]) | (r == rows_ref[1:2, :]), 1.0, 0.0).astype(BF16)
    h = h_ref[...]
    lo = jnp.dot(perm, h[:, :half], preferred_element_type=F32)
    hi = jnp.dot(perm, h[:, half:], preferred_element_type=F32)
    lbuf[slot] = _pack_bf16_pair(lo, hi)

    def strip(tile_slot):
        def make(loc, dst, n):
            return pltpu.make_async_copy(lbuf.at[tile_slot, pl.ds(loc, n)], xs_ref.at[pl.ds(dst, n)],
                                         sems.at[tile_slot])
        return make

    def wait_tile(tile):
        _wait_strip_rows(trows_ref[tile], lambda n: strip(tile % DISPATCH_SLOTS)(0, 0, n))

    _for_each_strip(step, src_ref, dst_ref, n_ref, lambda *a: strip(slot)(*a).start())

    @pl.when(step >= DISPATCH_SLOTS - 1)
    def _():
        wait_tile(step - (DISPATCH_SLOTS - 1))

    @pl.when(step == n_steps - 1)
    def _():
        for back in reversed(range(DISPATCH_SLOTS - 1)):
            @pl.when(step >= back)
            def _():
                wait_tile(step - back)
        zero_fill(lambda cp: cp.wait())


def _dispatch_call(strip_src, strip_dst, strip_n, tile_rows, pends, pad_len, pad_dst, rows, h2, P_pad, T, RB):
    N, D = h2.shape
    return pl.pallas_call(
        _dispatch_kernel,
        out_shape=jax.ShapeDtypeStruct((P_pad, D // 2), U32),
        grid_spec=pltpu.PrefetchScalarGridSpec(
            num_scalar_prefetch=7,
            grid=(N // T,),
            in_specs=[pl.BlockSpec((2, T), lambda i, *_: (0, i)),
                      pl.BlockSpec((T, D), lambda i, *_: (i, 0))],
            out_specs=pl.BlockSpec(memory_space=pl.ANY),
            scratch_shapes=[pltpu.VMEM((DISPATCH_SLOTS, LOCAL_ROWS, D // 2), U32),
                            pltpu.VMEM((RB, D // 2), U32),
                            pltpu.SemaphoreType.DMA((DISPATCH_SLOTS,)),
                            pltpu.SemaphoreType.DMA]),
        compiler_params=_cparams("arbitrary"),
        name="dispatch",
    )(strip_src, strip_dst, strip_n, tile_rows, pends, pad_len, pad_dst, rows, h2)


def _expert_kernel(seg_ref, sege_ref, nseg_ref, nused_ref, xs_ref, wgu_hbm, wd_hbm, ys_ref,
                   wgu_f32, wd_f32, wgu_bf, wd_bf, sems):
    i = pl.program_id(0)
    used = i < nused_ref[0]
    half = wgu_bf.shape[0] // 2
    k = seg_ref[i]

    def weight_copies(kk):
        slot = kk % 2
        e = sege_ref[kk]
        return (pltpu.make_async_copy(wgu_hbm.at[e], wgu_f32.at[slot], sems.at[0, slot]),
                pltpu.make_async_copy(wd_hbm.at[e], wd_f32.at[slot], sems.at[1, slot]))

    @pl.when(i == 0)
    def _():
        for cp in weight_copies(0):
            cp.start()

    @pl.when(used & ((i == 0) | (k != seg_ref[jnp.maximum(i - 1, 0)])))
    def _():
        for cp in weight_copies(k):
            cp.wait()

        @pl.when(k + 1 < nseg_ref[0])
        def _():
            for cp in weight_copies(k + 1):
                cp.start()

        wgu_bf[...] = wgu_f32[k % 2].astype(BF16)
        wd_bf[...] = wd_f32[k % 2].astype(BF16)

    @pl.when(used)
    def _():
        lo, hi = _unpack_bf16_pair(xs_ref[...])
        gu = (jnp.dot(lo, wgu_bf[0:half, :], preferred_element_type=F32)
              + jnp.dot(hi, wgu_bf[half:, :], preferred_element_type=F32))
        gate = gu[:, :D_EXPERT]
        up = gu[:, D_EXPERT:]
        act = gate / (1.0 + jnp.exp(-gate)) * up
        y = jnp.dot(act.astype(BF16), wd_bf[...], preferred_element_type=F32)
        yb = y.astype(BF16).astype(F32)
        ys_ref[...] = _pack_bf16_pair(yb[:, :half], yb[:, half:])

    @pl.when(jnp.logical_not(used))
    def _():
        ys_ref[...] = jnp.zeros(ys_ref.shape, ys_ref.dtype)


def _expert_call(block_seg, seg_expert, n_seg, n_used, xs, wgu, wd, RB):
    P, Dh = xs.shape
    D = 2 * Dh
    row_map = lambda i, *_: (i, 0)
    in_map = lambda i, seg, sege, nseg, nu: (jnp.minimum(i, nu[0] - 1), 0)
    return pl.pallas_call(
        _expert_kernel,
        out_shape=jax.ShapeDtypeStruct((P, Dh), U32),
        grid_spec=pltpu.PrefetchScalarGridSpec(
            num_scalar_prefetch=4,
            grid=(P // RB,),
            in_specs=[pl.BlockSpec((RB, Dh), in_map),
                      pl.BlockSpec(memory_space=pl.ANY),
                      pl.BlockSpec(memory_space=pl.ANY)],
            out_specs=pl.BlockSpec((RB, Dh), row_map),
            scratch_shapes=[pltpu.VMEM((2, D, 2 * D_EXPERT), F32), pltpu.VMEM((2, D_EXPERT, D), F32),
                            pltpu.VMEM((D, 2 * D_EXPERT), BF16), pltpu.VMEM((D_EXPERT, D), BF16),
                            pltpu.SemaphoreType.DMA((2, 2))]),
        compiler_params=_cparams("arbitrary"),
        name="experts",
    )(block_seg, seg_expert, n_seg, n_used, xs, wgu, wd)


def _combine_kernel(src_ref, dst_ref, n_ref, trows_ref, x1_ref, tok_ref, mod_ref, g_ref, ys_ref, o_ref, ybuf, sems):
    T = x1_ref.shape[0]
    step = pl.program_id(0)
    n_steps = pl.num_programs(0)
    slot = step % 2

    def strip(tile_slot):
        def make(loc, dst, n):
            return pltpu.make_async_copy(ys_ref.at[pl.ds(dst, n)], ybuf.at[tile_slot, pl.ds(loc, n)],
                                         sems.at[tile_slot])
        return make

    @pl.when(step == 0)
    def _():
        ybuf[...] = jnp.zeros(ybuf.shape, ybuf.dtype)
        _for_each_strip(step, src_ref, dst_ref, n_ref, lambda *a: strip(slot)(*a).start())

    @pl.when(step + 1 < n_steps)
    def _():
        _for_each_strip(step + 1, src_ref, dst_ref, n_ref, lambda *a: strip(1 - slot)(*a).start())

    _wait_strip_rows(trows_ref[step], lambda n: strip(slot)(0, 0, n))

    tok = tok_ref[...]
    lo, hi = _unpack_bf16_pair(ybuf[slot])
    n_chunks = 4
    tc = T // n_chunks

    def permw_chunk(ci):
        t = tok[ci * tc:(ci + 1) * tc]
        rows = t[:, 0:2].astype(I32)
        c = lax.broadcasted_iota(I32, (tc, LOCAL_ROWS), 1)
        return jnp.where(c == rows[:, 0:1], t[:, 2:3], jnp.where(c == rows[:, 1:2], t[:, 3:4], 0.0)).astype(BF16)

    p_next = permw_chunk(0)
    parts = []
    for ci in range(n_chunks):
        permw = p_next
        if ci + 1 < n_chunks:
            p_next = permw_chunk(ci + 1)
        parts.append(jnp.concatenate([jnp.dot(permw, lo, preferred_element_type=F32),
                                      jnp.dot(permw, hi, preferred_element_type=F32)], axis=1))
    moe = jnp.concatenate(parts, axis=0)
    gate_f = mod_ref[5]
    o_ref[...] = _rms(x1_ref[...] + gate_f * moe, g_ref[...])


def _combine_call(strip_src, strip_dst, strip_n, tile_rows, x1, tok, mod4, final_g, ys, T, tiles_per_batch):
    N, D = x1.shape
    return pl.pallas_call(
        _combine_kernel,
        out_shape=jax.ShapeDtypeStruct((N, D), F32),
        grid_spec=pltpu.PrefetchScalarGridSpec(
            num_scalar_prefetch=4,
            grid=(N // T,),
            in_specs=[pl.BlockSpec((T, D), lambda i, *_: (i, 0)),
                      pl.BlockSpec((T, 4), lambda i, *_: (i, 0)),
                      pl.BlockSpec((None, N_MOD, 1, D), lambda i, *_: (i // tiles_per_batch, 0, 0, 0)),
                      pl.BlockSpec((1, D), lambda i, *_: (0, 0)),
                      pl.BlockSpec(memory_space=pl.ANY)],
            out_specs=pl.BlockSpec((T, D), lambda i, *_: (i, 0)),
            scratch_shapes=[pltpu.VMEM((2, LOCAL_ROWS, D // 2), U32), pltpu.SemaphoreType.DMA((2,))]),
        compiler_params=_cparams("arbitrary"),
        name="combine",
    )(strip_src, strip_dst, strip_n, tile_rows, x1, tok, mod4, final_g, ys)


def _round_up(v, m):
    return (v + m - 1) // m * m


def kernel(x, c, positions, w_mod, b_mod, norm_mix_g, w_in, w_pool, pool_scale, q_norm_g, w_uq, kv_norm_g, w_ukv, w_o, norm_ffn_g, w_group, b_group, w_router, b_router, w_gate_up, w_down, final_g):
    B, S, D = x.shape
    N = B * S
    depth = w_mod.shape[0]
    T = ROW_TILE
    RB = EXPERT_ROWS
    assert depth == 1, "the final RMSNorm is fused into the layer's combine step"
    assert S % T == 0 and S % min(POST_TILE, S) == 0 and min(POST_TILE, S) % T == 0 and S % min(PRE_TILE, S) == 0 and S % min(ATTN_TQ, S) == 0 and min(ATTN_TQ, S) % ATTN_TK == 0
    tiles_per_batch = S // T
    post_tile = min(POST_TILE, S)
    n_tiles = N // T
    nH = MLA_HEADS
    l = 0

    inv_freq = ROPE_THETA ** (-(jnp.arange(0, QK_ROPE_DIM, 2, dtype=F32) / QK_ROPE_DIM))
    posr = positions.reshape(B, 1, S)
    cut1 = POOL_WIDTH
    cut2 = cut1 + Q_LORA_RANK
    cut3 = cut2 + KV_LORA_RANK

    mod4 = _mod_call(c, w_mod[l], b_mod[l]).reshape(B, N_MOD, 1, D)

    wi = w_in[l]
    w_in_main = wi[:, :cut3].astype(BF16)
    w_krT = wi[:, cut3:].T.astype(BF16)
    wq = w_uq[l].reshape(Q_LORA_RANK, nH, QK_HEAD_DIM)
    wq_n = wq[:, :, :QK_NOPE_DIM].reshape(Q_LORA_RANK, nH * QK_NOPE_DIM)
    wq_r = wq[:, :, QK_NOPE_DIM:]
    w_uqT = jnp.concatenate([wq_n, wq_r.reshape(Q_LORA_RANK, nH * QK_ROPE_DIM)], axis=1).T.astype(BF16)
    wkv = w_ukv[l].reshape(KV_LORA_RANK, nH, QK_NOPE_DIM + V_HEAD_DIM)
    w_uk = wkv[:, :, :QK_NOPE_DIM].reshape(KV_LORA_RANK, nH * QK_NOPE_DIM).astype(BF16)
    w_uvT = wkv[:, :, QK_NOPE_DIM:].reshape(KV_LORA_RANK, nH * V_HEAD_DIM).T.astype(BF16)
    qg = (q_norm_g[l] * (QK_HEAD_DIM ** -0.5 * LOG2_E)).reshape(1, Q_LORA_RANK)
    wpool_bd = jnp.zeros((POOL_WIDTH, POOL_WIDTH), F32)
    for g in range(len(POOL_WINDOWS)):
        sl = slice(g * POOL_GROUP_DIM, (g + 1) * POOL_GROUP_DIM)
        wpool_bd = wpool_bd.at[sl, sl].set(w_pool[l, g])
    wpool_bd = wpool_bd.astype(BF16)

    qT, k, vT, yp = _pre_call(
        x, posr, mod4, norm_mix_g[l].reshape(1, D), w_in_main, w_krT, inv_freq.reshape(QK_ROPE_DIM // 2, 1),
        wpool_bd, pool_scale[l].reshape(1, POOL_WIDTH), qg, w_uqT, kv_norm_g[l].reshape(1, KV_LORA_RANK),
        w_uk, w_uvT, min(PRE_TILE, S))
    ya = _attn_call(qT, k, vT)

    wo = w_o[l].astype(BF16)
    wgT = jnp.zeros((8, D), F32).at[:N_GROUPS].set(w_group[l].T).astype(BF16)
    bg = jnp.zeros((8, 1), F32).at[:N_GROUPS, 0].set(b_group[l])
    x1, h2, rows, meta_f, tile_cnt = _post_call(
        x.reshape(N, D), yp.reshape(N, POOL_WIDTH), ya.reshape(N, nH * V_HEAD_DIM), mod4,
        wo[:POOL_WIDTH], wo[POOL_WIDTH:], norm_ffn_g[l].reshape(1, D),
        wgT, bg, w_router[l].T.astype(BF16), b_router[l].reshape(N_EXPERTS, 1), post_tile, S // post_tile)

    run_len = _round_up(tile_cnt[:, :, 0], SUBLANES)
    run_loc = jnp.cumsum(run_len, axis=1) - run_len
    seg_rows = jnp.sum(run_len, axis=0)
    seg_len = _round_up(seg_rows, RB)
    pends = jnp.cumsum(seg_len).astype(I32)
    pstarts = pends - seg_len
    run_dst = pstarts[None, :] + jnp.cumsum(run_len, axis=0) - run_len
    pad_len = (seg_len - seg_rows).astype(I32)
    pad_dst = (pstarts + seg_rows).astype(I32)
    bits = jnp.arange(STRIP_BITS, dtype=I32)
    size = SUBLANES << bits
    valid = (run_len[:, None, :] & size[None, :, None]) != 0
    done = run_len[:, None, :] & ~(2 * size[None, :, None] - 1)
    flat = lambda a: a.reshape(n_tiles, STRIP_BITS * N_EXPERTS)
    valid_f = flat(valid)
    pos = jnp.cumsum(valid_f.astype(I32), axis=1) - valid_f.astype(I32)
    pick = valid_f[:, None, :] & (pos[:, None, :] == jnp.arange(MAX_STRIPS, dtype=I32)[None, :, None])
    gather = lambda a: jnp.sum(jnp.where(pick, flat(a)[:, None, :], 0), axis=-1).reshape(-1).astype(I32)
    strip_src = gather(run_loc[:, None, :] + done)
    strip_dst = gather(run_dst[:, None, :] + done)
    strip_n = jnp.sum(valid, axis=2).reshape(-1).astype(I32)
    tile_rows = jnp.sum(run_len, axis=1).astype(I32)
    P_pad = _round_up(2 * N + n_tiles * N_EXPERTS * (SUBLANES - 1), RB) + N_EXPERTS * RB
    n_rb = P_pad // RB
    n_used = (pends[-1:] // RB).astype(I32)
    block_start = jnp.minimum(jnp.arange(n_rb, dtype=I32), n_used - 1) * RB
    block_e = jnp.sum((pends[None, :] <= block_start[:, None]).astype(I32), axis=1)
    nonempty = seg_len > 0
    seg_index = jnp.cumsum(nonempty.astype(I32)) - 1
    n_seg = jnp.sum(nonempty.astype(I32)).reshape(1)
    experts = jnp.arange(N_EXPERTS, dtype=I32)
    block_seg = jnp.sum(jnp.where(block_e[:, None] == experts[None, :], seg_index[None, :], 0), axis=1).astype(I32)
    seg_expert = jnp.sum(jnp.where(nonempty[None, :] & (seg_index[None, :] == experts[:, None]), experts[None, :], 0),
                         axis=1).astype(I32)

    xs = _dispatch_call(strip_src, strip_dst, strip_n, tile_rows, pends, pad_len, pad_dst, rows, h2, P_pad, T, RB)
    ys = _expert_call(block_seg, seg_expert, n_seg, n_used, xs, w_gate_up[l], w_down[l], RB)
    out = _combine_call(strip_src, strip_dst, strip_n, tile_rows, x1, meta_f.T, mod4, final_g.reshape(1, D), ys,
                        T, tiles_per_batch)
    return out.reshape(B, S, D)
```

```python
import jax
import jax.numpy as jnp
from jax import lax
from jax.experimental import pallas as pl
from jax.experimental.pallas import tpu as pltpu

F32 = jnp.float32
BF16 = jnp.bfloat16
U32 = jnp.uint32
I32 = jnp.int32

POOL_WINDOWS = (2, 4, 8, 16)
POOL_GROUP_DIM = 64
POOL_WIDTH = 256
MLA_HEADS = 6
QK_NOPE_DIM = 128
QK_ROPE_DIM = 64
QK_HEAD_DIM = QK_NOPE_DIM + QK_ROPE_DIM
V_HEAD_DIM = 128
V_ROWS = 144
Q_LORA_RANK = 512
KV_LORA_RANK = 256
ROPE_THETA = 10000.0
N_GROUPS = 4
EXPERTS_PER_GROUP = 8
N_EXPERTS = N_GROUPS * EXPERTS_PER_GROUP
D_EXPERT = 256
N_MOD = 6
EPS = 1e-6

SUBLANES = 8
LANES = 128
assert EXPERTS_PER_GROUP == SUBLANES and N_GROUPS <= SUBLANES
assert POOL_WINDOWS == (2, 4, 8, 16) and POOL_GROUP_DIM * len(POOL_WINDOWS) == POOL_WIDTH
MOD_COLS = 512
POOL_HALO = 32
ROW_TILE = 512
PRE_TILE = 1024
POST_TILE = 1024
ATTN_TQ = 2048
ATTN_TK = 512
EXPERT_ROWS = 1024
LOCAL_ROWS = 2 * ROW_TILE + 256
assert LOCAL_ROWS >= 2 * ROW_TILE + N_EXPERTS * (SUBLANES - 1)
STRIP_BITS = (2 * ROW_TILE // SUBLANES).bit_length()
MAX_STRIPS = 128
assert MAX_STRIPS >= 2 * N_EXPERTS + (LOCAL_ROWS // SUBLANES - 3 * N_EXPERTS) // 4
DISPATCH_SLOTS = 2
VMEM_LIMIT = 56 * 1024 * 1024
NEG_BIG = -1e30
LOG2_E = 1.4426950408889634
HI_MASK = 0xFFFF0000


def _cparams(*sem):
    return pltpu.CompilerParams(dimension_semantics=sem, vmem_limit_bytes=VMEM_LIMIT)


def _rms(x, g):
    return x * lax.rsqrt(jnp.mean(x * x, axis=-1, keepdims=True) + EPS) * g


def _pack_bf16_pair(lo, hi):
    return lax.bitcast_convert_type(hi, U32) | (lax.bitcast_convert_type(lo, U32) >> 16)


def _unpack_bf16_pair(w):
    lo = lax.bitcast_convert_type(w << 16, F32).astype(BF16)
    hi = lax.bitcast_convert_type(w & jnp.uint32(HI_MASK), F32).astype(BF16)
    return lo, hi


def _mod_kernel(c_ref, w_ref, b_ref, o_ref):
    c = c_ref[...]
    ca = c / (1.0 + jnp.exp(-c))
    o_ref[...] = jnp.dot(ca.astype(BF16), w_ref[...].astype(BF16), preferred_element_type=F32) + b_ref[...]


def _mod_call(c, w_mod, b_mod):
    B, D = c.shape
    n_out = w_mod.shape[1]
    tn = MOD_COLS
    return pl.pallas_call(
        _mod_kernel,
        out_shape=jax.ShapeDtypeStruct((B, n_out), F32),
        grid=(n_out // tn,),
        in_specs=[pl.BlockSpec((B, D), lambda j: (0, 0)),
                  pl.BlockSpec((D, tn), lambda j: (0, j)),
                  pl.BlockSpec((1, tn), lambda j: (0, j))],
        out_specs=pl.BlockSpec((B, tn), lambda j: (0, j)),
        compiler_params=_cparams("arbitrary"),
        name="mod",
    )(c, w_mod, b_mod.reshape(1, n_out))


def _pre_kernel(x_ref, posr_ref, mod_ref, g_ref, win_ref, wkrT_ref, invfc_ref, wpool_ref, pscale_ref,
                qg_ref, wuqT_ref, kvg_ref, wuk_ref, wuvT_ref,
                qT_ref, k_ref, vT_ref, yp_ref,
                pbuf, b2, b4, b8):
    T = x_ref.shape[0]
    H = POOL_HALO
    i = pl.program_id(1)
    nt = (((1,), (1,)), ((), ()))

    shift = mod_ref[0]
    scale = mod_ref[1]
    hb = (_rms(x_ref[...], g_ref[...]) * (1.0 + scale) + shift).astype(BF16)
    u = jnp.dot(hb, win_ref[...], preferred_element_type=F32)
    krT = lax.dot_general(wkrT_ref[...], hb, nt, preferred_element_type=F32)

    p = u[:, :POOL_WIDTH]

    @pl.when(i == 0)
    def _():
        pbuf[0:H, :] = jnp.zeros((H, POOL_WIDTH), F32)

    pbuf[H:H + T, :] = p
    b2[8:T + H, :] = pbuf[8:T + H, :] + pbuf[7:T + H - 1, :]
    b4[16:T + H, :] = b2[16:T + H, :] + b2[14:T + H - 2, :]
    b8[24:T + H, :] = b4[24:T + H, :] + b4[20:T + H - 4, :]
    s2 = b2[H:T + H, :]
    s4 = b4[H:T + H, :]
    s8 = b8[H:T + H, :]
    s16 = b8[H:T + H, :] + b8[H - 8:T + H - 8, :]
    pbuf[0:H, :] = pbuf[T:T + H, :]

    lane = lax.broadcasted_iota(I32, (T, POOL_WIDTH), 1)
    t1 = (lax.broadcasted_iota(I32, (T, 1), 0) + (i * T + 1)).astype(F32)
    inv2 = 1.0 / jnp.minimum(t1, 2.0)
    inv4 = 1.0 / jnp.minimum(t1, 4.0)
    inv8 = 1.0 / jnp.minimum(t1, 8.0)
    inv16 = 1.0 / jnp.minimum(t1, 16.0)
    gd = POOL_GROUP_DIM
    mean = jnp.where(lane < gd, s2 * inv2,
                     jnp.where(lane < 2 * gd, s4 * inv4,
                               jnp.where(lane < 3 * gd, s8 * inv8, s16 * inv16)))
    pooled = mean - p
    yp = jnp.dot(pooled.astype(BF16), wpool_ref[...], preferred_element_type=F32) * pscale_ref[...]
    yp_ref[...] = yp.astype(yp_ref.dtype)

    ang = invfc_ref[...] * posr_ref[...].astype(F32)
    cos_h = jnp.cos(ang)
    sin_h = jnp.sin(ang)
    hr = QK_ROPE_DIM // 2

    def rope(xt):
        x1, x2 = xt[0:hr], xt[hr:]
        return x1 * cos_h - x2 * sin_h, x2 * cos_h + x1 * sin_h

    k_rope = jnp.concatenate(rope(krT), axis=0).T.astype(BF16)

    cq = u[:, POOL_WIDTH:POOL_WIDTH + Q_LORA_RANK]
    ckv = u[:, POOL_WIDTH + Q_LORA_RANK:POOL_WIDTH + Q_LORA_RANK + KV_LORA_RANK]
    cqn = _rms(cq, qg_ref[...]).astype(BF16)
    ckvn = _rms(ckv, kvg_ref[...]).astype(BF16)
    qaT = lax.dot_general(wuqT_ref[...], cqn, nt, preferred_element_type=F32)
    kn = jnp.dot(ckvn, wuk_ref[...], preferred_element_type=F32)
    vT = lax.dot_general(wuvT_ref[...], ckvn, nt, preferred_element_type=F32)
    nq = MLA_HEADS * QK_NOPE_DIM
    ones_rows = jnp.where(lax.broadcasted_iota(I32, (V_ROWS - V_HEAD_DIM, T), 0) == 0, 1.0, 0.0).astype(BF16)
    for hd in range(MLA_HEADS):
        q1, q2 = rope(qaT[nq + hd * QK_ROPE_DIM:nq + (hd + 1) * QK_ROPE_DIM, :])
        qT_ref[hd, 0:QK_NOPE_DIM, :] = qaT[hd * QK_NOPE_DIM:(hd + 1) * QK_NOPE_DIM, :].astype(BF16)
        qT_ref[hd, QK_NOPE_DIM:QK_NOPE_DIM + hr, :] = q1.astype(BF16)
        qT_ref[hd, QK_NOPE_DIM + hr:QK_HEAD_DIM, :] = q2.astype(BF16)
        k_ref[hd, :, 0:QK_NOPE_DIM] = kn[:, hd * QK_NOPE_DIM:(hd + 1) * QK_NOPE_DIM].astype(BF16)
        k_ref[hd, :, QK_NOPE_DIM:QK_HEAD_DIM] = k_rope
        vT_ref[hd, 0:V_HEAD_DIM, :] = vT[hd * V_HEAD_DIM:(hd + 1) * V_HEAD_DIM, :].astype(BF16)
        vT_ref[hd, V_HEAD_DIM:V_ROWS, :] = ones_rows


def _pre_call(x, posr, mod4, norm_g, w_in_main, w_krT, inv_freq_col, wpool_bd, pool_scale, qg, w_uqT, kvg,
              w_uk, w_uvT, T):
    B, S, D = x.shape
    nH = MLA_HEADS
    const = lambda shape: pl.BlockSpec(shape, lambda b, i: (0,) * len(shape))
    return pl.pallas_call(
        _pre_kernel,
        out_shape=(jax.ShapeDtypeStruct((B, nH, QK_HEAD_DIM, S), BF16),
                   jax.ShapeDtypeStruct((B, nH, S, QK_HEAD_DIM), BF16),
                   jax.ShapeDtypeStruct((B, nH, V_ROWS, S), BF16),
                   jax.ShapeDtypeStruct((B, S, POOL_WIDTH), BF16)),
        grid=(B, S // T),
        in_specs=[pl.BlockSpec((None, T, D), lambda b, i: (b, i, 0)),
                  pl.BlockSpec((None, 1, T), lambda b, i: (b, 0, i)),
                  pl.BlockSpec((None, N_MOD, 1, D), lambda b, i: (b, 0, 0, 0)),
                  const((1, D)),
                  const(w_in_main.shape),
                  const(w_krT.shape),
                  const(inv_freq_col.shape),
                  const(wpool_bd.shape),
                  const((1, POOL_WIDTH)),
                  const((1, Q_LORA_RANK)),
                  const(w_uqT.shape),
                  const((1, KV_LORA_RANK)),
                  const(w_uk.shape),
                  const(w_uvT.shape)],
        out_specs=(pl.BlockSpec((None, nH, QK_HEAD_DIM, T), lambda b, i: (b, 0, 0, i)),
                   pl.BlockSpec((None, nH, T, QK_HEAD_DIM), lambda b, i: (b, 0, i, 0)),
                   pl.BlockSpec((None, nH, V_ROWS, T), lambda b, i: (b, 0, 0, i)),
                   pl.BlockSpec((None, T, POOL_WIDTH), lambda b, i: (b, i, 0))),
        scratch_shapes=[pltpu.VMEM((T + POOL_HALO, POOL_WIDTH), F32)] * 4,
        compiler_params=_cparams("arbitrary", "arbitrary"),
        name="pre",
    )(x, posr, mod4, norm_g, w_in_main, w_krT, inv_freq_col, wpool_bd, pool_scale, qg, w_uqT, kvg, w_uk, w_uvT)


def _attn_kernel(qT_ref, k_ref, vT_ref, o_ref):
    S = k_ref.shape[0]
    tq, tk = min(ATTN_TQ, S), ATTN_TK
    diag = lax.broadcasted_iota(I32, (tk, tk), 0) <= lax.broadcasted_iota(I32, (tk, tk), 1)
    blocks = [(i, j) for i in range(S // tq) for j in range((i + 1) * (tq // tk))]

    def scores(i, j):
        d = max(j * tk - i * tq, 0)
        sT = jnp.dot(k_ref[j * tk:(j + 1) * tk, :], qT_ref[:, i * tq + d:(i + 1) * tq],
                     preferred_element_type=F32)
        if j * tk >= i * tq:
            masked = jnp.where(diag, sT[:, :tk], NEG_BIG)
            sT = masked if sT.shape[1] == tk else jnp.concatenate([masked, sT[:, tk:]], axis=1)
        return sT

    s_next = scores(*blocks[0])
    m = acc = None
    for n, (i, j) in enumerate(blocks):
        sT = s_next
        if n + 1 < len(blocks):
            s_next = scores(*blocks[n + 1])
        if j == 0:
            m = jnp.full((1, tq), NEG_BIG, F32)
            acc = jnp.zeros((V_ROWS, tq), F32)
        d = max(j * tk - i * tq, 0)
        vT = vT_ref[:, j * tk:(j + 1) * tk]
        m_new = jnp.maximum(m[:, d:], jnp.max(sT, axis=0, keepdims=True))
        pT = jnp.exp2(sT - m_new).astype(BF16)
        acc_new = jnp.exp2(m[:, d:] - m_new) * acc[:, d:] + jnp.dot(vT, pT, preferred_element_type=F32)
        if d:
            m_new = jnp.concatenate([m[:, :d], m_new], axis=1)
            acc_new = jnp.concatenate([acc[:, :d], acc_new], axis=1)
        m, acc = m_new, acc_new
        if j == (i + 1) * (tq // tk) - 1:
            out = acc[0:V_HEAD_DIM] * (1.0 / acc[V_HEAD_DIM:V_HEAD_DIM + 1])
            o_ref[i * tq:(i + 1) * tq, :] = out.T.astype(o_ref.dtype)


def _attn_call(qT, k, vT):
    B, nH, S, _ = k.shape
    return pl.pallas_call(
        _attn_kernel,
        out_shape=jax.ShapeDtypeStruct((B, S, nH * V_HEAD_DIM), BF16),
        grid=(B, nH),
        in_specs=[pl.BlockSpec((None, None, QK_HEAD_DIM, S), lambda b, h: (b, h, 0, 0)),
                  pl.BlockSpec((None, None, S, QK_HEAD_DIM), lambda b, h: (b, h, 0, 0)),
                  pl.BlockSpec((None, None, V_ROWS, S), lambda b, h: (b, h, 0, 0))],
        out_specs=pl.BlockSpec((None, S, V_HEAD_DIM), lambda b, h: (b, 0, h)),
        compiler_params=_cparams("arbitrary", "arbitrary"),
        name="attn",
    )(qT, k, vT)


def _route_tile(hb, wgT_ref, bg_ref, wrT_ref, br_ref):
    T = hb.shape[0]
    nt = (((1,), (1,)), ((), ()))
    gl = lax.dot_general(wgT_ref[...], hb, nt, preferred_element_type=F32)
    el = lax.dot_general(wrT_ref[...], hb, nt, preferred_element_type=F32)

    r8 = lax.broadcasted_iota(I32, (SUBLANES, T), 0)
    gvalid = r8 < N_GROUPS
    gmax = jnp.max(jnp.where(gvalid, gl, NEG_BIG), axis=0, keepdims=True)
    gexp = jnp.where(gvalid, jnp.exp(gl - gmax), 0.0)
    g_prob = gexp / jnp.sum(gexp, axis=0, keepdims=True)
    gb = jnp.where(gvalid, gl + bg_ref[...], NEG_BIG)
    gbmax = jnp.max(gb, axis=0, keepdims=True)
    g_sel = jnp.min(jnp.where(gb == gbmax, r8, SUBLANES), axis=0, keepdims=True)
    gp = jnp.sum(jnp.where(r8 == g_sel, g_prob, 0.0), axis=0, keepdims=True)

    e_in = jnp.zeros((EXPERTS_PER_GROUP, T), F32)
    b_in = jnp.zeros((EXPERTS_PER_GROUP, T), F32)
    br = br_ref[...]
    for g in range(N_GROUPS):
        sel = g_sel == g
        e_in = jnp.where(sel, el[g * EXPERTS_PER_GROUP:(g + 1) * EXPERTS_PER_GROUP, :], e_in)
        b_in = jnp.where(sel, br[g * EXPERTS_PER_GROUP:(g + 1) * EXPERTS_PER_GROUP, :], b_in)
    eb = e_in + b_in
    m1 = jnp.max(eb, axis=0, keepdims=True)
    i1 = jnp.min(jnp.where(eb == m1, r8, SUBLANES), axis=0, keepdims=True)
    eb2 = jnp.where(r8 == i1, NEG_BIG, eb)
    m2 = jnp.max(eb2, axis=0, keepdims=True)
    i2 = jnp.min(jnp.where(eb2 == m2, r8, SUBLANES), axis=0, keepdims=True)
    emax = jnp.max(e_in, axis=0, keepdims=True)
    eexp = jnp.exp(e_in - emax)
    sp = eexp / jnp.sum(eexp, axis=0, keepdims=True)
    p1 = jnp.sum(jnp.where(r8 == i1, sp, 0.0), axis=0, keepdims=True)
    p2 = jnp.sum(jnp.where(r8 == i2, sp, 0.0), axis=0, keepdims=True)
    tot = p1 + p2
    w1 = gp * (p1 / tot)
    w2 = gp * (p2 / tot)
    e1 = g_sel * EXPERTS_PER_GROUP + i1
    e2 = g_sel * EXPERTS_PER_GROUP + i2

    r32 = lax.broadcasted_iota(I32, (N_EXPERTS, T), 0)
    oh1 = r32 == e1
    oh2 = r32 == e2
    oh = jnp.where(oh1 | oh2, 1.0, 0.0)
    upper = jnp.where(lax.broadcasted_iota(I32, (T, T), 0) < lax.broadcasted_iota(I32, (T, T), 1),
                      1.0, 0.0).astype(BF16)
    before = jnp.dot(oh.astype(BF16), upper, preferred_element_type=F32)
    cnt = jnp.sum(oh, axis=1, keepdims=True)
    run8 = jnp.floor((cnt + (SUBLANES - 1.0)) * (1.0 / SUBLANES))
    lower = jnp.where(lax.broadcasted_iota(I32, (N_EXPERTS, N_EXPERTS), 1)
                      < lax.broadcasted_iota(I32, (N_EXPERTS, N_EXPERTS), 0), 1.0, 0.0).astype(BF16)
    run_start = jnp.dot(lower, jnp.broadcast_to(run8, (N_EXPERTS, LANES)).astype(BF16),
                        preferred_element_type=F32)[:, 0:1] * float(SUBLANES)
    pos = before + run_start
    row1 = jnp.sum(jnp.where(oh1, pos, 0.0), axis=0, keepdims=True)
    row2 = jnp.sum(jnp.where(oh2, pos, 0.0), axis=0, keepdims=True)
    return row1, row2, w1, w2, cnt


def _post_kernel(x_ref, yp_ref, ya_ref, mod_ref, wo_p_ref, wo_a_ref, g_ref,
                 wgT_ref, bg_ref, wrT_ref, br_ref,
                 x1_ref, h2_ref, rows_ref, mf_ref, cnt_ref):
    T = ROW_TILE
    gate_a = mod_ref[2]
    shift_f = mod_ref[3]
    scale_f = mod_ref[4]
    mix = (jnp.dot(yp_ref[...], wo_p_ref[...], preferred_element_type=F32)
           + jnp.dot(ya_ref[...], wo_a_ref[...], preferred_element_type=F32))
    x1 = x_ref[...] + gate_a * mix
    x1_ref[...] = x1
    hb = (_rms(x1, g_ref[...]) * (1.0 + scale_f) + shift_f).astype(BF16)
    h2_ref[...] = hb

    for s in range(x_ref.shape[0] // T):
        cols = slice(s * T, (s + 1) * T)
        row1, row2, w1, w2, cnt = _route_tile(hb[cols], wgT_ref, bg_ref, wrT_ref, br_ref)
        cnt_ref[s] = cnt.astype(I32)
        rows_ref[0:1, cols] = row1.astype(I32)
        rows_ref[1:2, cols] = row2.astype(I32)
        mf_ref[0:1, cols] = row1
        mf_ref[1:2, cols] = row2
        mf_ref[2:3, cols] = w1
        mf_ref[3:4, cols] = w2


def _post_call(x2, yp2, ya2, mod4, wo_p, wo_a, g, wgT, bg, wrT, br, T, steps_per_batch):
    N, D = x2.shape
    const = lambda shape: pl.BlockSpec(shape, lambda i: (0,) * len(shape))
    return pl.pallas_call(
        _post_kernel,
        out_shape=(jax.ShapeDtypeStruct((N, D), F32),
                   jax.ShapeDtypeStruct((N, D), BF16),
                   jax.ShapeDtypeStruct((2, N), I32),
                   jax.ShapeDtypeStruct((4, N), F32),
                   jax.ShapeDtypeStruct((N // ROW_TILE, N_EXPERTS, 1), I32)),
        grid=(N // T,),
        in_specs=[pl.BlockSpec((T, D), lambda i: (i, 0)),
                  pl.BlockSpec((T, POOL_WIDTH), lambda i: (i, 0)),
                  pl.BlockSpec((T, MLA_HEADS * V_HEAD_DIM), lambda i: (i, 0)),
                  pl.BlockSpec((None, N_MOD, 1, D), lambda i: (i // steps_per_batch, 0, 0, 0)),
                  const(wo_p.shape), const(wo_a.shape), const((1, D)),
                  const(wgT.shape), const(bg.shape), const(wrT.shape), const(br.shape)],
        out_specs=(pl.BlockSpec((T, D), lambda i: (i, 0)),
                   pl.BlockSpec((T, D), lambda i: (i, 0)),
                   pl.BlockSpec((2, T), lambda i: (0, i)),
                   pl.BlockSpec((4, T), lambda i: (0, i)),
                   pl.BlockSpec((T // ROW_TILE, N_EXPERTS, 1), lambda i: (i, 0, 0))),
        compiler_params=_cparams("arbitrary"),
        name="post",
    )(x2, yp2, ya2, mod4, wo_p, wo_a, g, wgT, bg, wrT, br)


def _for_each_strip(tile, src_ref, dst_ref, n_ref, fn):
    first = tile * MAX_STRIPS
    for b in range(STRIP_BITS):
        last = first + n_ref[tile * STRIP_BITS + b]

        def body(r, c, n=SUBLANES << b):
            fn(pl.multiple_of(src_ref[r], SUBLANES), pl.multiple_of(dst_ref[r], SUBLANES), n)
            return c

        lax.fori_loop(first, last, body, 0)
        first = last


def _wait_strip_rows(n_rows, make):
    for b in range((LOCAL_ROWS // SUBLANES).bit_length()):
        n = SUBLANES << b

        @pl.when((n_rows & n) != 0)
        def _():
            make(n).wait()


def _dispatch_kernel(src_ref, dst_ref, n_ref, trows_ref, pends_ref, padlen_ref, paddst_ref, rows_ref, h_ref, xs_ref,
                     lbuf, zbuf, sems, zsem):
    T = h_ref.shape[0]
    RB = zbuf.shape[0]
    half = h_ref.shape[1] // 2
    step = pl.program_id(0)
    n_steps = pl.num_programs(0)
    slot = step % DISPATCH_SLOTS

    def zero_fill(act):
        def per_expert(e, c):
            pad = padlen_ref[e]
            dst = paddst_ref[e]
            for b in reversed(range((RB // SUBLANES).bit_length() - 1)):
                n = SUBLANES << b
                done = pad & ~(2 * n - 1)

                @pl.when((pad & n) != 0)
                def _():
                    act(pltpu.make_async_copy(zbuf.at[pl.ds(0, n)],
                                              xs_ref.at[pl.ds(pl.multiple_of(dst + done, SUBLANES), n)], zsem))
            return c

        lax.fori_loop(0, N_EXPERTS, per_expert, 0)

        def per_block(b, c):
            act(pltpu.make_async_copy(zbuf, xs_ref.at[pl.ds(pl.multiple_of(b * RB, RB), RB)], zsem))
            return c

        lax.fori_loop(pends_ref[N_EXPERTS - 1] // RB, xs_ref.shape[0] // RB, per_block, 0)

    @pl.when(step == 0)
    def _():
        zbuf[...] = jnp.zeros(zbuf.shape, zbuf.dtype)
        zero_fill(lambda cp: cp.start())

    r = lax.broadcasted_iota(I32, (LOCAL_ROWS, T), 0)
    perm = jnp.where((r == rows_ref[0:1, :]) | (r == rows_ref[1:2, :]), 1.0, 0.0).astype(BF16)
    h = h_ref[...]
    lo = jnp.dot(perm, h[:, :half], preferred_element_type=F32)
    hi = jnp.dot(perm, h[:, half:], preferred_element_type=F32)
    lbuf[slot] = _pack_bf16_pair(lo, hi)

    def strip(tile_slot):
        def make(loc, dst, n):
            return pltpu.make_async_copy(lbuf.at[tile_slot, pl.ds(loc, n)], xs_ref.at[pl.ds(dst, n)],
                                         sems.at[tile_slot])
        return make

    def wait_tile(tile):
        _wait_strip_rows(trows_ref[tile], lambda n: strip(tile % DISPATCH_SLOTS)(0, 0, n))

    _for_each_strip(step, src_ref, dst_ref, n_ref, lambda *a: strip(slot)(*a).start())

    @pl.when(step >= DISPATCH_SLOTS - 1)
    def _():
        wait_tile(step - (DISPATCH_SLOTS - 1))

    @pl.when(step == n_steps - 1)
    def _():
        for back in reversed(range(DISPATCH_SLOTS - 1)):
            @pl.when(step >= back)
            def _():
                wait_tile(step - back)
        zero_fill(lambda cp: cp.wait())


def _dispatch_call(strip_src, strip_dst, strip_n, tile_rows, pends, pad_len, pad_dst, rows, h2, P_pad, T, RB):
    N, D = h2.shape
    return pl.pallas_call(
        _dispatch_kernel,
        out_shape=jax.ShapeDtypeStruct((P_pad, D // 2), U32),
        grid_spec=pltpu.PrefetchScalarGridSpec(
            num_scalar_prefetch=7,
            grid=(N // T,),
            in_specs=[pl.BlockSpec((2, T), lambda i, *_: (0, i)),
                      pl.BlockSpec((T, D), lambda i, *_: (i, 0))],
            out_specs=pl.BlockSpec(memory_space=pl.ANY),
            scratch_shapes=[pltpu.VMEM((DISPATCH_SLOTS, LOCAL_ROWS, D // 2), U32),
                            pltpu.VMEM((RB, D // 2), U32),
                            pltpu.SemaphoreType.DMA((DISPATCH_SLOTS,)),
                            pltpu.SemaphoreType.DMA]),
        compiler_params=_cparams("arbitrary"),
        name="dispatch",
    )(strip_src, strip_dst, strip_n, tile_rows, pends, pad_len, pad_dst, rows, h2)


def _expert_kernel(seg_ref, sege_ref, nseg_ref, nused_ref, xs_ref, wgu_hbm, wd_hbm, ys_ref,
                   wgu_f32, wd_f32, wgu_bf, wd_bf, sems):
    i = pl.program_id(0)
    used = i < nused_ref[0]
    half = wgu_bf.shape[0] // 2
    k = seg_ref[i]

    def weight_copies(kk):
        slot = kk % 2
        e = sege_ref[kk]
        return (pltpu.make_async_copy(wgu_hbm.at[e], wgu_f32.at[slot], sems.at[0, slot]),
                pltpu.make_async_copy(wd_hbm.at[e], wd_f32.at[slot], sems.at[1, slot]))

    @pl.when(i == 0)
    def _():
        for cp in weight_copies(0):
            cp.start()

    @pl.when(used & ((i == 0) | (k != seg_ref[jnp.maximum(i - 1, 0)])))
    def _():
        for cp in weight_copies(k):
            cp.wait()

        @pl.when(k + 1 < nseg_ref[0])
        def _():
            for cp in weight_copies(k + 1):
                cp.start()

        wgu_bf[...] = wgu_f32[k % 2].astype(BF16)
        wd_bf[...] = wd_f32[k % 2].astype(BF16)

    @pl.when(used)
    def _():
        lo, hi = _unpack_bf16_pair(xs_ref[...])
        gu = (jnp.dot(lo, wgu_bf[0:half, :], preferred_element_type=F32)
              + jnp.dot(hi, wgu_bf[half:, :], preferred_element_type=F32))
        gate = gu[:, :D_EXPERT]
        up = gu[:, D_EXPERT:]
        act = gate / (1.0 + jnp.exp(-gate)) * up
        y = jnp.dot(act.astype(BF16), wd_bf[...], preferred_element_type=F32)
        yb = y.astype(BF16).astype(F32)
        ys_ref[...] = _pack_bf16_pair(yb[:, :half], yb[:, half:])

    @pl.when(jnp.logical_not(used))
    def _():
        ys_ref[...] = jnp.zeros(ys_ref.shape, ys_ref.dtype)


def _expert_call(block_seg, seg_expert, n_seg, n_used, xs, wgu, wd, RB):
    P, Dh = xs.shape
    D = 2 * Dh
    row_map = lambda i, *_: (i, 0)
    in_map = lambda i, seg, sege, nseg, nu: (jnp.minimum(i, nu[0] - 1), 0)
    return pl.pallas_call(
        _expert_kernel,
        out_shape=jax.ShapeDtypeStruct((P, Dh), U32),
        grid_spec=pltpu.PrefetchScalarGridSpec(
            num_scalar_prefetch=4,
            grid=(P // RB,),
            in_specs=[pl.BlockSpec((RB, Dh), in_map),
                      pl.BlockSpec(memory_space=pl.ANY),
                      pl.BlockSpec(memory_space=pl.ANY)],
            out_specs=pl.BlockSpec((RB, Dh), row_map),
            scratch_shapes=[pltpu.VMEM((2, D, 2 * D_EXPERT), F32), pltpu.VMEM((2, D_EXPERT, D), F32),
                            pltpu.VMEM((D, 2 * D_EXPERT), BF16), pltpu.VMEM((D_EXPERT, D), BF16),
                            pltpu.SemaphoreType.DMA((2, 2))]),
        compiler_params=_cparams("arbitrary"),
        name="experts",
    )(block_seg, seg_expert, n_seg, n_used, xs, wgu, wd)


def _combine_kernel(src_ref, dst_ref, n_ref, trows_ref, x1_ref, tok_ref, mod_ref, g_ref, ys_ref, o_ref, ybuf, sems):
    T = x1_ref.shape[0]
    step = pl.program_id(0)
    n_steps = pl.num_programs(0)
    slot = step % 2

    def strip(tile_slot):
        def make(loc, dst, n):
            return pltpu.make_async_copy(ys_ref.at[pl.ds(dst, n)], ybuf.at[tile_slot, pl.ds(loc, n)],
                                         sems.at[tile_slot])
        return make

    @pl.when(step == 0)
    def _():
        ybuf[...] = jnp.zeros(ybuf.shape, ybuf.dtype)
        _for_each_strip(step, src_ref, dst_ref, n_ref, lambda *a: strip(slot)(*a).start())

    @pl.when(step + 1 < n_steps)
    def _():
        _for_each_strip(step + 1, src_ref, dst_ref, n_ref, lambda *a: strip(1 - slot)(*a).start())

    _wait_strip_rows(trows_ref[step], lambda n: strip(slot)(0, 0, n))

    tok = tok_ref[...]
    lo, hi = _unpack_bf16_pair(ybuf[slot])
    n_chunks = 4
    tc = T // n_chunks

    def permw_chunk(ci):
        t = tok[ci * tc:(ci + 1) * tc]
        rows = t[:, 0:2].astype(I32)
        c = lax.broadcasted_iota(I32, (tc, LOCAL_ROWS), 1)
        return jnp.where(c == rows[:, 0:1], t[:, 2:3], jnp.where(c == rows[:, 1:2], t[:, 3:4], 0.0)).astype(BF16)

    p_next = permw_chunk(0)
    parts = []
    for ci in range(n_chunks):
        permw = p_next
        if ci + 1 < n_chunks:
            p_next = permw_chunk(ci + 1)
        parts.append(jnp.concatenate([jnp.dot(permw, lo, preferred_element_type=F32),
                                      jnp.dot(permw, hi, preferred_element_type=F32)], axis=1))
    moe = jnp.concatenate(parts, axis=0)
    gate_f = mod_ref[5]
    o_ref[...] = _rms(x1_ref[...] + gate_f * moe, g_ref[...])


def _combine_call(strip_src, strip_dst, strip_n, tile_rows, x1, tok, mod4, final_g, ys, T, tiles_per_batch):
    N, D = x1.shape
    return pl.pallas_call(
        _combine_kernel,
        out_shape=jax.ShapeDtypeStruct((N, D), F32),
        grid_spec=pltpu.PrefetchScalarGridSpec(
            num_scalar_prefetch=4,
            grid=(N // T,),
            in_specs=[pl.BlockSpec((T, D), lambda i, *_: (i, 0)),
                      pl.BlockSpec((T, 4), lambda i, *_: (i, 0)),
                      pl.BlockSpec((None, N_MOD, 1, D), lambda i, *_: (i // tiles_per_batch, 0, 0, 0)),
                      pl.BlockSpec((1, D), lambda i, *_: (0, 0)),
                      pl.BlockSpec(memory_space=pl.ANY)],
            out_specs=pl.BlockSpec((T, D), lambda i, *_: (i, 0)),
            scratch_shapes=[pltpu.VMEM((2, LOCAL_ROWS, D // 2), U32), pltpu.SemaphoreType.DMA((2,))]),
        compiler_params=_cparams("arbitrary"),
        name="combine",
    )(strip_src, strip_dst, strip_n, tile_rows, x1, tok, mod4, final_g, ys)


def _round_up(v, m):
    return (v + m - 1) // m * m


def kernel(x, c, positions, w_mod, b_mod, norm_mix_g, w_in, w_pool, pool_scale, q_norm_g, w_uq, kv_norm_g, w_ukv, w_o, norm_ffn_g, w_group, b_group, w_router, b_router, w_gate_up, w_down, final_g):
    B, S, D = x.shape
    N = B * S
    depth = w_mod.shape[0]
    T = ROW_TILE
    RB = EXPERT_ROWS
    assert depth == 1, "the final RMSNorm is fused into the layer's combine step"
    assert S % T == 0 and S % min(POST_TILE, S) == 0 and min(POST_TILE, S) % T == 0 and S % min(PRE_TILE, S) == 0 and S % min(ATTN_TQ, S) == 0 and min(ATTN_TQ, S) % ATTN_TK == 0
    tiles_per_batch = S // T
    post_tile = min(POST_TILE, S)
    n_tiles = N // T
    nH = MLA_HEADS
    l = 0

    inv_freq = ROPE_THETA ** (-(jnp.arange(0, QK_ROPE_DIM, 2, dtype=F32) / QK_ROPE_DIM))
    posr = positions.reshape(B, 1, S)
    cut3 = POOL_WIDTH + Q_LORA_RANK + KV_LORA_RANK

    mod4 = _mod_call(c, w_mod[l], b_mod[l]).reshape(B, N_MOD, 1, D)

    wi = w_in[l]
    w_in_main = wi[:, :cut3].astype(BF16)
    w_krT = wi[:, cut3:].T.astype(BF16)
    wq = w_uq[l].reshape(Q_LORA_RANK, nH, QK_HEAD_DIM)
    wq_n = wq[:, :, :QK_NOPE_DIM].reshape(Q_LORA_RANK, nH * QK_NOPE_DIM)
    wq_r = wq[:, :, QK_NOPE_DIM:]
    w_uqT = jnp.concatenate([wq_n, wq_r.reshape(Q_LORA_RANK, nH * QK_ROPE_DIM)], axis=1).T.astype(BF16)
    wkv = w_ukv[l].reshape(KV_LORA_RANK, nH, QK_NOPE_DIM + V_HEAD_DIM)
    w_uk = wkv[:, :, :QK_NOPE_DIM].reshape(KV_LORA_RANK, nH * QK_NOPE_DIM).astype(BF16)
    w_uvT = wkv[:, :, QK_NOPE_DIM:].reshape(KV_LORA_RANK, nH * V_HEAD_DIM).T.astype(BF16)
    qg = (q_norm_g[l] * (QK_HEAD_DIM ** -0.5 * LOG2_E)).reshape(1, Q_LORA_RANK)
    wpool_bd = jnp.zeros((POOL_WIDTH, POOL_WIDTH), F32)
    for g in range(len(POOL_WINDOWS)):
        sl = slice(g * POOL_GROUP_DIM, (g + 1) * POOL_GROUP_DIM)
        wpool_bd = wpool_bd.at[sl, sl].set(w_pool[l, g])
    wpool_bd = wpool_bd.astype(BF16)

    qT, k, vT, yp = _pre_call(
        x, posr, mod4, norm_mix_g[l].reshape(1, D), w_in_main, w_krT, inv_freq.reshape(QK_ROPE_DIM // 2, 1),
        wpool_bd, pool_scale[l].reshape(1, POOL_WIDTH), qg, w_uqT, kv_norm_g[l].reshape(1, KV_LORA_RANK),
        w_uk, w_uvT, min(PRE_TILE, S))
    ya = _attn_call(qT, k, vT)

    wo = w_o[l].astype(BF16)
    wgT = jnp.zeros((SUBLANES, D), F32).at[:N_GROUPS].set(w_group[l].T).astype(BF16)
    bg = jnp.zeros((SUBLANES, 1), F32).at[:N_GROUPS, 0].set(b_group[l])
    x1, h2, rows, meta_f, tile_cnt = _post_call(
        x.reshape(N, D), yp.reshape(N, POOL_WIDTH), ya.reshape(N, nH * V_HEAD_DIM), mod4,
        wo[:POOL_WIDTH], wo[POOL_WIDTH:], norm_ffn_g[l].reshape(1, D),
        wgT, bg, w_router[l].T.astype(BF16), b_router[l].reshape(N_EXPERTS, 1), post_tile, S // post_tile)

    run_len = _round_up(tile_cnt[:, :, 0], SUBLANES)
    run_loc = jnp.cumsum(run_len, axis=1) - run_len
    seg_rows = jnp.sum(run_len, axis=0)
    seg_len = _round_up(seg_rows, RB)
    pends = jnp.cumsum(seg_len).astype(I32)
    pstarts = pends - seg_len
    run_dst = pstarts[None, :] + jnp.cumsum(run_len, axis=0) - run_len
    pad_len = (seg_len - seg_rows).astype(I32)
    pad_dst = (pstarts + seg_rows).astype(I32)
    bits = jnp.arange(STRIP_BITS, dtype=I32)
    size = SUBLANES << bits
    valid = (run_len[:, None, :] & size[None, :, None]) != 0
    done = run_len[:, None, :] & ~(2 * size[None, :, None] - 1)
    flat = lambda a: a.reshape(n_tiles, STRIP_BITS * N_EXPERTS)
    valid_f = flat(valid)
    pos = jnp.cumsum(valid_f.astype(I32), axis=1) - valid_f.astype(I32)
    pick = valid_f[:, None, :] & (pos[:, None, :] == jnp.arange(MAX_STRIPS, dtype=I32)[None, :, None])
    gather = lambda a: jnp.sum(jnp.where(pick, flat(a)[:, None, :], 0), axis=-1).reshape(-1).astype(I32)
    strip_src = gather(run_loc[:, None, :] + done)
    strip_dst = gather(run_dst[:, None, :] + done)
    strip_n = jnp.sum(valid, axis=2).reshape(-1).astype(I32)
    tile_rows = jnp.sum(run_len, axis=1).astype(I32)
    P_pad = _round_up(2 * N + n_tiles * N_EXPERTS * (SUBLANES - 1), RB) + N_EXPERTS * RB
    n_rb = P_pad // RB
    n_used = (pends[-1:] // RB).astype(I32)
    block_start = jnp.minimum(jnp.arange(n_rb, dtype=I32), n_used - 1) * RB
    block_e = jnp.sum((pends[None, :] <= block_start[:, None]).astype(I32), axis=1)
    nonempty = seg_len > 0
    seg_index = jnp.cumsum(nonempty.astype(I32)) - 1
    n_seg = jnp.sum(nonempty.astype(I32)).reshape(1)
    experts = jnp.arange(N_EXPERTS, dtype=I32)
    block_seg = jnp.sum(jnp.where(block_e[:, None] == experts[None, :], seg_index[None, :], 0), axis=1).astype(I32)
    seg_expert = jnp.sum(jnp.where(nonempty[None, :] & (seg_index[None, :] == experts[:, None]), experts[None, :], 0),
                         axis=1).astype(I32)

    xs = _dispatch_call(strip_src, strip_dst, strip_n, tile_rows, pends, pad_len, pad_dst, rows, h2, P_pad, T, RB)
    ys = _expert_call(block_seg, seg_expert, n_seg, n_used, xs, w_gate_up[l], w_down[l], RB)
    out = _combine_call(strip_src, strip_dst, strip_n, tile_rows, x1, meta_f.T, mod4, final_g.reshape(1, D), ys,
                        T, tiles_per_batch)
    return out.reshape(B, S, D)
```

```python
import jax
import jax.numpy as jnp
from jax import lax
from jax.experimental import pallas as pl
from jax.experimental.pallas import tpu as pltpu

F32 = jnp.float32
BF16 = jnp.bfloat16
U32 = jnp.uint32
I32 = jnp.int32

POOL_WINDOWS = (2, 4, 8, 16)
POOL_GROUP_DIM = 64
POOL_WIDTH = 256
MLA_HEADS = 6
QK_NOPE_DIM = 128
QK_ROPE_DIM = 64
QK_HEAD_DIM = QK_NOPE_DIM + QK_ROPE_DIM
V_HEAD_DIM = 128
V_ROWS = 144
Q_LORA_RANK = 512
KV_LORA_RANK = 256
ROPE_THETA = 10000.0
N_GROUPS = 4
EXPERTS_PER_GROUP = 8
N_EXPERTS = N_GROUPS * EXPERTS_PER_GROUP
D_EXPERT = 256
N_MOD = 6
EPS = 1e-6

SUBLANES = 8
LANES = 128
assert EXPERTS_PER_GROUP == SUBLANES and N_GROUPS <= SUBLANES
assert POOL_WINDOWS == (2, 4, 8, 16) and POOL_GROUP_DIM * len(POOL_WINDOWS) == POOL_WIDTH
MOD_COLS = 512
POOL_HALO = 32
ROW_TILE = 512
PRE_TILE = 1024
POST_TILE = 1024
ATTN_TQ = 2048
ATTN_TK = 512
EXPERT_ROWS = 1024
LOCAL_ROWS = 2 * ROW_TILE + 256
assert LOCAL_ROWS >= 2 * ROW_TILE + N_EXPERTS * (SUBLANES - 1)
STRIP_BITS = (2 * ROW_TILE // SUBLANES).bit_length()
MAX_STRIPS = 128
assert MAX_STRIPS >= 2 * N_EXPERTS + (LOCAL_ROWS // SUBLANES - 3 * N_EXPERTS) // 4
DISPATCH_SLOTS = 2
VMEM_LIMIT = 56 * 1024 * 1024
NEG_BIG = -1e30
LOG2_E = 1.4426950408889634
HI_MASK = 0xFFFF0000


def _cparams(*sem):
    return pltpu.CompilerParams(dimension_semantics=sem, vmem_limit_bytes=VMEM_LIMIT)


def _rms(x, g):
    return x * lax.rsqrt(jnp.mean(x * x, axis=-1, keepdims=True) + EPS) * g


def _pack_bf16_pair(lo, hi):
    return lax.bitcast_convert_type(hi, U32) | (lax.bitcast_convert_type(lo, U32) >> 16)


def _unpack_bf16_pair(w):
    lo = lax.bitcast_convert_type(w << 16, F32).astype(BF16)
    hi = lax.bitcast_convert_type(w & jnp.uint32(HI_MASK), F32).astype(BF16)
    return lo, hi


def _mod_kernel(c_ref, w_ref, b_ref, o_ref):
    c = c_ref[...]
    ca = c / (1.0 + jnp.exp(-c))
    o_ref[...] = jnp.dot(ca.astype(BF16), w_ref[...].astype(BF16), preferred_element_type=F32) + b_ref[...]


def _mod_call(c, w_mod, b_mod):
    B, D = c.shape
    n_out = w_mod.shape[1]
    tn = MOD_COLS
    return pl.pallas_call(
        _mod_kernel,
        out_shape=jax.ShapeDtypeStruct((B, n_out), F32),
        grid=(n_out // tn,),
        in_specs=[pl.BlockSpec((B, D), lambda j: (0, 0)),
                  pl.BlockSpec((D, tn), lambda j: (0, j)),
                  pl.BlockSpec((1, tn), lambda j: (0, j))],
        out_specs=pl.BlockSpec((B, tn), lambda j: (0, j)),
        compiler_params=_cparams("arbitrary"),
        name="mod",
    )(c, w_mod, b_mod.reshape(1, n_out))


def _pre_kernel(x_ref, posr_ref, mod_ref, g_ref, win_ref, wkrT_ref, invfc_ref, wpool_ref, pscale_ref,
                qg_ref, wuqT_ref, kvg_ref, wuk_ref, wuvT_ref,
                qT_ref, k_ref, vT_ref, yp_ref,
                pbuf, b2, b4, b8):
    T = x_ref.shape[0]
    H = POOL_HALO
    i = pl.program_id(1)
    nt = (((1,), (1,)), ((), ()))

    shift = mod_ref[0]
    scale = mod_ref[1]
    hb = (_rms(x_ref[...], g_ref[...]) * (1.0 + scale) + shift).astype(BF16)
    u = jnp.dot(hb, win_ref[...], preferred_element_type=F32)
    krT = lax.dot_general(wkrT_ref[...], hb, nt, preferred_element_type=F32)

    p = u[:, :POOL_WIDTH]

    @pl.when(i == 0)
    def _():
        pbuf[0:H, :] = jnp.zeros((H, POOL_WIDTH), F32)

    pbuf[H:H + T, :] = p
    b2[8:T + H, :] = pbuf[8:T + H, :] + pbuf[7:T + H - 1, :]
    b4[16:T + H, :] = b2[16:T + H, :] + b2[14:T + H - 2, :]
    b8[24:T + H, :] = b4[24:T + H, :] + b4[20:T + H - 4, :]
    s2 = b2[H:T + H, :]
    s4 = b4[H:T + H, :]
    s8 = b8[H:T + H, :]
    s16 = b8[H:T + H, :] + b8[H - 8:T + H - 8, :]
    pbuf[0:H, :] = pbuf[T:T + H, :]

    lane = lax.broadcasted_iota(I32, (T, POOL_WIDTH), 1)
    t1 = (lax.broadcasted_iota(I32, (T, 1), 0) + (i * T + 1)).astype(F32)
    inv2 = 1.0 / jnp.minimum(t1, 2.0)
    inv4 = 1.0 / jnp.minimum(t1, 4.0)
    inv8 = 1.0 / jnp.minimum(t1, 8.0)
    inv16 = 1.0 / jnp.minimum(t1, 16.0)
    gd = POOL_GROUP_DIM
    mean = jnp.where(lane < gd, s2 * inv2,
                     jnp.where(lane < 2 * gd, s4 * inv4,
                               jnp.where(lane < 3 * gd, s8 * inv8, s16 * inv16)))
    pooled = mean - p
    yp = jnp.dot(pooled.astype(BF16), wpool_ref[...], preferred_element_type=F32) * pscale_ref[...]
    yp_ref[...] = yp.astype(yp_ref.dtype)

    ang = invfc_ref[...] * posr_ref[...].astype(F32)
    cos_h = jnp.cos(ang)
    sin_h = jnp.sin(ang)
    hr = QK_ROPE_DIM // 2

    def rope(xt):
        x1, x2 = xt[0:hr], xt[hr:]
        return x1 * cos_h - x2 * sin_h, x2 * cos_h + x1 * sin_h

    k_rope = jnp.concatenate(rope(krT), axis=0).T.astype(BF16)

    cq = u[:, POOL_WIDTH:POOL_WIDTH + Q_LORA_RANK]
    ckv = u[:, POOL_WIDTH + Q_LORA_RANK:POOL_WIDTH + Q_LORA_RANK + KV_LORA_RANK]
    cqn = _rms(cq, qg_ref[...]).astype(BF16)
    ckvn = _rms(ckv, kvg_ref[...]).astype(BF16)
    qaT = lax.dot_general(wuqT_ref[...], cqn, nt, preferred_element_type=F32)
    kn = jnp.dot(ckvn, wuk_ref[...], preferred_element_type=F32)
    vT = lax.dot_general(wuvT_ref[...], ckvn, nt, preferred_element_type=F32)
    nq = MLA_HEADS * QK_NOPE_DIM
    ones_rows = jnp.where(lax.broadcasted_iota(I32, (V_ROWS - V_HEAD_DIM, T), 0) == 0, 1.0, 0.0).astype(BF16)
    for hd in range(MLA_HEADS):
        q1, q2 = rope(qaT[nq + hd * QK_ROPE_DIM:nq + (hd + 1) * QK_ROPE_DIM, :])
        qT_ref[hd, 0:QK_NOPE_DIM, :] = qaT[hd * QK_NOPE_DIM:(hd + 1) * QK_NOPE_DIM, :].astype(BF16)
        qT_ref[hd, QK_NOPE_DIM:QK_NOPE_DIM + hr, :] = q1.astype(BF16)
        qT_ref[hd, QK_NOPE_DIM + hr:QK_HEAD_DIM, :] = q2.astype(BF16)
        k_ref[hd, :, 0:QK_NOPE_DIM] = kn[:, hd * QK_NOPE_DIM:(hd + 1) * QK_NOPE_DIM].astype(BF16)
        k_ref[hd, :, QK_NOPE_DIM:QK_HEAD_DIM] = k_rope
        vT_ref[hd, 0:V_HEAD_DIM, :] = vT[hd * V_HEAD_DIM:(hd + 1) * V_HEAD_DIM, :].astype(BF16)
        vT_ref[hd, V_HEAD_DIM:V_ROWS, :] = ones_rows


def _pre_call(x, posr, mod4, norm_g, w_in_main, w_krT, inv_freq_col, wpool_bd, pool_scale, qg, w_uqT, kvg,
              w_uk, w_uvT, T):
    B, S, D = x.shape
    nH = MLA_HEADS
    const = lambda shape: pl.BlockSpec(shape, lambda b, i: (0,) * len(shape))
    return pl.pallas_call(
        _pre_kernel,
        out_shape=(jax.ShapeDtypeStruct((B, nH, QK_HEAD_DIM, S), BF16),
                   jax.ShapeDtypeStruct((B, nH, S, QK_HEAD_DIM), BF16),
                   jax.ShapeDtypeStruct((B, nH, V_ROWS, S), BF16),
                   jax.ShapeDtypeStruct((B, S, POOL_WIDTH), BF16)),
        grid=(B, S // T),
        in_specs=[pl.BlockSpec((None, T, D), lambda b, i: (b, i, 0)),
                  pl.BlockSpec((None, 1, T), lambda b, i: (b, 0, i)),
                  pl.BlockSpec((None, N_MOD, 1, D), lambda b, i: (b, 0, 0, 0)),
                  const((1, D)),
                  const(w_in_main.shape),
                  const(w_krT.shape),
                  const(inv_freq_col.shape),
                  const(wpool_bd.shape),
                  const((1, POOL_WIDTH)),
                  const((1, Q_LORA_RANK)),
                  const(w_uqT.shape),
                  const((1, KV_LORA_RANK)),
                  const(w_uk.shape),
                  const(w_uvT.shape)],
        out_specs=(pl.BlockSpec((None, nH, QK_HEAD_DIM, T), lambda b, i: (b, 0, 0, i)),
                   pl.BlockSpec((None, nH, T, QK_HEAD_DIM), lambda b, i: (b, 0, i, 0)),
                   pl.BlockSpec((None, nH, V_ROWS, T), lambda b, i: (b, 0, 0, i)),
                   pl.BlockSpec((None, T, POOL_WIDTH), lambda b, i: (b, i, 0))),
        scratch_shapes=[pltpu.VMEM((T + POOL_HALO, POOL_WIDTH), F32)] * 4,
        compiler_params=_cparams("arbitrary", "arbitrary"),
        name="pre",
    )(x, posr, mod4, norm_g, w_in_main, w_krT, inv_freq_col, wpool_bd, pool_scale, qg, w_uqT, kvg, w_uk, w_uvT)


def _attn_kernel(qT_ref, k_ref, vT_ref, o_ref):
    S = k_ref.shape[0]
    tq, tk = min(ATTN_TQ, S), ATTN_TK
    diag = lax.broadcasted_iota(I32, (tk, tk), 0) <= lax.broadcasted_iota(I32, (tk, tk), 1)
    blocks = [(i, j) for i in range(S // tq) for j in range((i + 1) * (tq // tk))]

    def scores(i, j):
        d = max(j * tk - i * tq, 0)
        sT = jnp.dot(k_ref[j * tk:(j + 1) * tk, :], qT_ref[:, i * tq + d:(i + 1) * tq],
                     preferred_element_type=F32)
        if j * tk >= i * tq:
            masked = jnp.where(diag, sT[:, :tk], NEG_BIG)
            sT = masked if sT.shape[1] == tk else jnp.concatenate([masked, sT[:, tk:]], axis=1)
        return sT

    s_next = scores(*blocks[0])
    m = acc = None
    for n, (i, j) in enumerate(blocks):
        sT = s_next
        if n + 1 < len(blocks):
            s_next = scores(*blocks[n + 1])
        if j == 0:
            m = jnp.full((1, tq), NEG_BIG, F32)
            acc = jnp.zeros((V_ROWS, tq), F32)
        d = max(j * tk - i * tq, 0)
        vT = vT_ref[:, j * tk:(j + 1) * tk]
        m_new = jnp.maximum(m[:, d:], jnp.max(sT, axis=0, keepdims=True))
        pT = jnp.exp2(sT - m_new).astype(BF16)
        acc_new = jnp.exp2(m[:, d:] - m_new) * acc[:, d:] + jnp.dot(vT, pT, preferred_element_type=F32)
        if d:
            m_new = jnp.concatenate([m[:, :d], m_new], axis=1)
            acc_new = jnp.concatenate([acc[:, :d], acc_new], axis=1)
        m, acc = m_new, acc_new
        if j == (i + 1) * (tq // tk) - 1:
            out = acc[0:V_HEAD_DIM] * (1.0 / acc[V_HEAD_DIM:V_HEAD_DIM + 1])
            o_ref[i * tq:(i + 1) * tq, :] = out.T.astype(o_ref.dtype)


def _attn_call(qT, k, vT):
    B, nH, S, _ = k.shape
    return pl.pallas_call(
        _attn_kernel,
        out_shape=jax.ShapeDtypeStruct((B, S, nH * V_HEAD_DIM), BF16),
        grid=(B, nH),
        in_specs=[pl.BlockSpec((None, None, QK_HEAD_DIM, S), lambda b, h: (b, h, 0, 0)),
                  pl.BlockSpec((None, None, S, QK_HEAD_DIM), lambda b, h: (b, h, 0, 0)),
                  pl.BlockSpec((None, None, V_ROWS, S), lambda b, h: (b, h, 0, 0))],
        out_specs=pl.BlockSpec((None, S, V_HEAD_DIM), lambda b, h: (b, 0, h)),
        compiler_params=_cparams("arbitrary", "arbitrary"),
        name="attn",
    )(qT, k, vT)


def _route_tile(hb, wgT_ref, bg_ref, wrT_ref, br_ref):
    T = hb.shape[0]
    nt = (((1,), (1,)), ((), ()))
    gl = lax.dot_general(wgT_ref[...], hb, nt, preferred_element_type=F32)
    el = lax.dot_general(wrT_ref[...], hb, nt, preferred_element_type=F32)

    r8 = lax.broadcasted_iota(I32, (SUBLANES, T), 0)
    gvalid = r8 < N_GROUPS
    gmax = jnp.max(jnp.where(gvalid, gl, NEG_BIG), axis=0, keepdims=True)
    gexp = jnp.where(gvalid, jnp.exp(gl - gmax), 0.0)
    g_prob = gexp / jnp.sum(gexp, axis=0, keepdims=True)
    gb = jnp.where(gvalid, gl + bg_ref[...], NEG_BIG)
    gbmax = jnp.max(gb, axis=0, keepdims=True)
    g_sel = jnp.min(jnp.where(gb == gbmax, r8, SUBLANES), axis=0, keepdims=True)
    gp = jnp.sum(jnp.where(r8 == g_sel, g_prob, 0.0), axis=0, keepdims=True)

    e_in = jnp.zeros((EXPERTS_PER_GROUP, T), F32)
    b_in = jnp.zeros((EXPERTS_PER_GROUP, T), F32)
    br = br_ref[...]
    for g in range(N_GROUPS):
        sel = g_sel == g
        e_in = jnp.where(sel, el[g * EXPERTS_PER_GROUP:(g + 1) * EXPERTS_PER_GROUP, :], e_in)
        b_in = jnp.where(sel, br[g * EXPERTS_PER_GROUP:(g + 1) * EXPERTS_PER_GROUP, :], b_in)
    eb = e_in + b_in
    m1 = jnp.max(eb, axis=0, keepdims=True)
    i1 = jnp.min(jnp.where(eb == m1, r8, SUBLANES), axis=0, keepdims=True)
    eb2 = jnp.where(r8 == i1, NEG_BIG, eb)
    m2 = jnp.max(eb2, axis=0, keepdims=True)
    i2 = jnp.min(jnp.where(eb2 == m2, r8, SUBLANES), axis=0, keepdims=True)
    emax = jnp.max(e_in, axis=0, keepdims=True)
    eexp = jnp.exp(e_in - emax)
    sp = eexp / jnp.sum(eexp, axis=0, keepdims=True)
    p1 = jnp.sum(jnp.where(r8 == i1, sp, 0.0), axis=0, keepdims=True)
    p2 = jnp.sum(jnp.where(r8 == i2, sp, 0.0), axis=0, keepdims=True)
    tot = p1 + p2
    w1 = gp * (p1 / tot)
    w2 = gp * (p2 / tot)
    e1 = g_sel * EXPERTS_PER_GROUP + i1
    e2 = g_sel * EXPERTS_PER_GROUP + i2

    r32 = lax.broadcasted_iota(I32, (N_EXPERTS, T), 0)
    oh1 = r32 == e1
    oh2 = r32 == e2
    oh = jnp.where(oh1 | oh2, 1.0, 0.0)
    upper = jnp.where(lax.broadcasted_iota(I32, (T, T), 0) < lax.broadcasted_iota(I32, (T, T), 1),
                      1.0, 0.0).astype(BF16)
    before = jnp.dot(oh.astype(BF16), upper, preferred_element_type=F32)
    cnt = jnp.sum(oh, axis=1, keepdims=True)
    run8 = jnp.floor((cnt + (SUBLANES - 1.0)) * (1.0 / SUBLANES))
    lower = jnp.where(lax.broadcasted_iota(I32, (N_EXPERTS, N_EXPERTS), 1)
                      < lax.broadcasted_iota(I32, (N_EXPERTS, N_EXPERTS), 0), 1.0, 0.0).astype(BF16)
    run_start = jnp.dot(lower, jnp.broadcast_to(run8, (N_EXPERTS, LANES)).astype(BF16),
                        preferred_element_type=F32)[:, 0:1] * float(SUBLANES)
    pos = before + run_start
    row1 = jnp.sum(jnp.where(oh1, pos, 0.0), axis=0, keepdims=True)
    row2 = jnp.sum(jnp.where(oh2, pos, 0.0), axis=0, keepdims=True)
    return row1, row2, w1, w2, cnt


def _post_kernel(x_ref, yp_ref, ya_ref, mod_ref, wo_p_ref, wo_a_ref, g_ref,
                 wgT_ref, bg_ref, wrT_ref, br_ref,
                 x1_ref, h2_ref, rows_ref, mf_ref, cnt_ref):
    T = ROW_TILE
    gate_a = mod_ref[2]
    shift_f = mod_ref[3]
    scale_f = mod_ref[4]
    mix = (jnp.dot(yp_ref[...], wo_p_ref[...], preferred_element_type=F32)
           + jnp.dot(ya_ref[...], wo_a_ref[...], preferred_element_type=F32))
    x1 = x_ref[...] + gate_a * mix
    x1_ref[...] = x1
    hb = (_rms(x1, g_ref[...]) * (1.0 + scale_f) + shift_f).astype(BF16)
    h2_ref[...] = hb

    for s in range(x_ref.shape[0] // T):
        cols = slice(s * T, (s + 1) * T)
        row1, row2, w1, w2, cnt = _route_tile(hb[cols], wgT_ref, bg_ref, wrT_ref, br_ref)
        cnt_ref[s] = cnt.astype(I32)
        rows_ref[0:1, cols] = row1.astype(I32)
        rows_ref[1:2, cols] = row2.astype(I32)
        mf_ref[0:1, cols] = row1
        mf_ref[1:2, cols] = row2
        mf_ref[2:3, cols] = w1
        mf_ref[3:4, cols] = w2


def _post_call(x2, yp2, ya2, mod4, wo_p, wo_a, g, wgT, bg, wrT, br, T, steps_per_batch):
    N, D = x2.shape
    const = lambda shape: pl.BlockSpec(shape, lambda i: (0,) * len(shape))
    return pl.pallas_call(
        _post_kernel,
        out_shape=(jax.ShapeDtypeStruct((N, D), F32),
                   jax.ShapeDtypeStruct((N, D), BF16),
                   jax.ShapeDtypeStruct((2, N), I32),
                   jax.ShapeDtypeStruct((4, N), F32),
                   jax.ShapeDtypeStruct((N // ROW_TILE, N_EXPERTS, 1), I32)),
        grid=(N // T,),
        in_specs=[pl.BlockSpec((T, D), lambda i: (i, 0)),
                  pl.BlockSpec((T, POOL_WIDTH), lambda i: (i, 0)),
                  pl.BlockSpec((T, MLA_HEADS * V_HEAD_DIM), lambda i: (i, 0)),
                  pl.BlockSpec((None, N_MOD, 1, D), lambda i: (i // steps_per_batch, 0, 0, 0)),
                  const(wo_p.shape), const(wo_a.shape), const((1, D)),
                  const(wgT.shape), const(bg.shape), const(wrT.shape), const(br.shape)],
        out_specs=(pl.BlockSpec((T, D), lambda i: (i, 0)),
                   pl.BlockSpec((T, D), lambda i: (i, 0)),
                   pl.BlockSpec((2, T), lambda i: (0, i)),
                   pl.BlockSpec((4, T), lambda i: (0, i)),
                   pl.BlockSpec((T // ROW_TILE, N_EXPERTS, 1), lambda i: (i, 0, 0))),
        compiler_params=_cparams("arbitrary"),
        name="post",
    )(x2, yp2, ya2, mod4, wo_p, wo_a, g, wgT, bg, wrT, br)


def _for_each_strip(tile, src_ref, dst_ref, n_ref, fn):
    first = tile * MAX_STRIPS
    for b in range(STRIP_BITS):
        last = first + n_ref[tile * STRIP_BITS + b]

        def body(r, c, n=SUBLANES << b):
            fn(pl.multiple_of(src_ref[r], SUBLANES), pl.multiple_of(dst_ref[r], SUBLANES), n)
            return c

        lax.fori_loop(first, last, body, 0)
        first = last


def _wait_strip_rows(n_rows, make):
    for b in range((LOCAL_ROWS // SUBLANES).bit_length()):
        n = SUBLANES << b

        @pl.when((n_rows & n) != 0)
        def _():
            make(n).wait()


def _dispatch_kernel(src_ref, dst_ref, n_ref, trows_ref, pends_ref, padlen_ref, paddst_ref, rows_ref, h_ref, xs_ref,
                     lbuf, zbuf, sems, zsem):
    T = h_ref.shape[0]
    RB = zbuf.shape[0]
    half = h_ref.shape[1] // 2
    step = pl.program_id(0)
    n_steps = pl.num_programs(0)
    slot = step % DISPATCH_SLOTS

    def zero_fill(act, part, n_parts):
        def per_expert(e, c):
            pad = padlen_ref[e]
            dst = paddst_ref[e]
            for b in reversed(range((RB // SUBLANES).bit_length() - 1)):
                n = SUBLANES << b
                done = pad & ~(2 * n - 1)

                @pl.when((pad & n) != 0)
                def _():
                    act(pltpu.make_async_copy(zbuf.at[pl.ds(0, n)],
                                              xs_ref.at[pl.ds(pl.multiple_of(dst + done, SUBLANES), n)], zsem))
            return c

        def per_block(j, c):
            b = pends_ref[N_EXPERTS - 1] // RB + part + j * n_parts
            act(pltpu.make_async_copy(zbuf, xs_ref.at[pl.ds(pl.multiple_of(b * RB, RB), RB)], zsem))
            return c

        n_tail = xs_ref.shape[0] // RB - pends_ref[N_EXPERTS - 1] // RB
        lax.fori_loop(0, (N_EXPERTS - part + n_parts - 1) // n_parts,
                      lambda j, c: per_expert(part + j * n_parts, c), 0)
        lax.fori_loop(0, (n_tail - part + n_parts - 1) // n_parts, per_block, 0)

    @pl.when(step == 0)
    def _():
        zbuf[...] = jnp.zeros(zbuf.shape, zbuf.dtype)

    zero_fill(lambda cp: cp.start(), step, n_steps)

    r = lax.broadcasted_iota(I32, (LOCAL_ROWS, T), 0)
    perm = jnp.where((r == rows_ref[0:1, :]) | (r == rows_ref[1:2, :]), 1.0, 0.0).astype(BF16)
    h = h_ref[...]
    lo = jnp.dot(perm, h[:, :half], preferred_element_type=F32)
    hi = jnp.dot(perm, h[:, half:], preferred_element_type=F32)
    lbuf[slot] = _pack_bf16_pair(lo, hi)

    def strip(tile_slot):
        def make(loc, dst, n):
            return pltpu.make_async_copy(lbuf.at[tile_slot, pl.ds(loc, n)], xs_ref.at[pl.ds(dst, n)],
                                         sems.at[tile_slot])
        return make

    def wait_tile(tile):
        _wait_strip_rows(trows_ref[tile], lambda n: strip(tile % DISPATCH_SLOTS)(0, 0, n))

    _for_each_strip(step, src_ref, dst_ref, n_ref, lambda *a: strip(slot)(*a).start())

    @pl.when(step >= DISPATCH_SLOTS - 1)
    def _():
        wait_tile(step - (DISPATCH_SLOTS - 1))

    @pl.when(step == n_steps - 1)
    def _():
        for back in reversed(range(DISPATCH_SLOTS - 1)):
            @pl.when(step >= back)
            def _():
                wait_tile(step - back)
        zero_fill(lambda cp: cp.wait(), 0, 1)


def _dispatch_call(strip_src, strip_dst, strip_n, tile_rows, pends, pad_len, pad_dst, rows, h2, P_pad, T, RB):
    N, D = h2.shape
    return pl.pallas_call(
        _dispatch_kernel,
        out_shape=jax.ShapeDtypeStruct((P_pad, D // 2), U32),
        grid_spec=pltpu.PrefetchScalarGridSpec(
            num_scalar_prefetch=7,
            grid=(N // T,),
            in_specs=[pl.BlockSpec((2, T), lambda i, *_: (0, i)),
                      pl.BlockSpec((T, D), lambda i, *_: (i, 0))],
            out_specs=pl.BlockSpec(memory_space=pl.ANY),
            scratch_shapes=[pltpu.VMEM((DISPATCH_SLOTS, LOCAL_ROWS, D // 2), U32),
                            pltpu.VMEM((RB, D // 2), U32),
                            pltpu.SemaphoreType.DMA((DISPATCH_SLOTS,)),
                            pltpu.SemaphoreType.DMA]),
        compiler_params=_cparams("arbitrary"),
        name="dispatch",
    )(strip_src, strip_dst, strip_n, tile_rows, pends, pad_len, pad_dst, rows, h2)


def _expert_kernel(seg_ref, sege_ref, nseg_ref, nused_ref, xs_ref, wgu_hbm, wd_hbm, ys_ref,
                   wgu_f32, wd_f32, wgu_bf, wd_bf, sems):
    i = pl.program_id(0)
    used = i < nused_ref[0]
    half = wgu_bf.shape[0] // 2
    k = seg_ref[i]

    def weight_copies(kk):
        slot = kk % 2
        e = sege_ref[kk]
        return (pltpu.make_async_copy(wgu_hbm.at[e], wgu_f32.at[slot], sems.at[0, slot]),
                pltpu.make_async_copy(wd_hbm.at[e], wd_f32.at[slot], sems.at[1, slot]))

    @pl.when(i == 0)
    def _():
        for cp in weight_copies(0):
            cp.start()

    @pl.when(used & ((i == 0) | (k != seg_ref[jnp.maximum(i - 1, 0)])))
    def _():
        for cp in weight_copies(k):
            cp.wait()

        @pl.when(k + 1 < nseg_ref[0])
        def _():
            for cp in weight_copies(k + 1):
                cp.start()

        wgu_bf[...] = wgu_f32[k % 2].astype(BF16)
        wd_bf[...] = wd_f32[k % 2].astype(BF16)

    @pl.when(used)
    def _():
        lo, hi = _unpack_bf16_pair(xs_ref[...])
        gu = (jnp.dot(lo, wgu_bf[0:half, :], preferred_element_type=F32)
              + jnp.dot(hi, wgu_bf[half:, :], preferred_element_type=F32))
        gate = gu[:, :D_EXPERT]
        up = gu[:, D_EXPERT:]
        act = gate / (1.0 + jnp.exp(-gate)) * up
        y = jnp.dot(act.astype(BF16), wd_bf[...], preferred_element_type=F32)
        yb = y.astype(BF16).astype(F32)
        ys_ref[...] = _pack_bf16_pair(yb[:, :half], yb[:, half:])

    @pl.when(jnp.logical_not(used))
    def _():
        ys_ref[...] = jnp.zeros(ys_ref.shape, ys_ref.dtype)


def _expert_call(block_seg, seg_expert, n_seg, n_used, xs, wgu, wd, RB):
    P, Dh = xs.shape
    D = 2 * Dh
    row_map = lambda i, *_: (i, 0)
    in_map = lambda i, seg, sege, nseg, nu: (jnp.minimum(i, nu[0] - 1), 0)
    return pl.pallas_call(
        _expert_kernel,
        out_shape=jax.ShapeDtypeStruct((P, Dh), U32),
        grid_spec=pltpu.PrefetchScalarGridSpec(
            num_scalar_prefetch=4,
            grid=(P // RB,),
            in_specs=[pl.BlockSpec((RB, Dh), in_map),
                      pl.BlockSpec(memory_space=pl.ANY),
                      pl.BlockSpec(memory_space=pl.ANY)],
            out_specs=pl.BlockSpec((RB, Dh), row_map),
            scratch_shapes=[pltpu.VMEM((2, D, 2 * D_EXPERT), F32), pltpu.VMEM((2, D_EXPERT, D), F32),
                            pltpu.VMEM((D, 2 * D_EXPERT), BF16), pltpu.VMEM((D_EXPERT, D), BF16),
                            pltpu.SemaphoreType.DMA((2, 2))]),
        compiler_params=_cparams("arbitrary"),
        name="experts",
    )(block_seg, seg_expert, n_seg, n_used, xs, wgu, wd)


def _combine_kernel(src_ref, dst_ref, n_ref, trows_ref, x1_ref, tok_ref, mod_ref, g_ref, ys_ref, o_ref, ybuf, sems):
    T = x1_ref.shape[0]
    step = pl.program_id(0)
    n_steps = pl.num_programs(0)
    slot = step % 2

    def strip(tile_slot):
        def make(loc, dst, n):
            return pltpu.make_async_copy(ys_ref.at[pl.ds(dst, n)], ybuf.at[tile_slot, pl.ds(loc, n)],
                                         sems.at[tile_slot])
        return make

    @pl.when(step == 0)
    def _():
        ybuf[...] = jnp.zeros(ybuf.shape, ybuf.dtype)
        _for_each_strip(step, src_ref, dst_ref, n_ref, lambda *a: strip(slot)(*a).start())

    @pl.when(step + 1 < n_steps)
    def _():
        _for_each_strip(step + 1, src_ref, dst_ref, n_ref, lambda *a: strip(1 - slot)(*a).start())

    _wait_strip_rows(trows_ref[step], lambda n: strip(slot)(0, 0, n))

    tok = tok_ref[...]
    lo, hi = _unpack_bf16_pair(ybuf[slot])
    n_chunks = 4
    tc = T // n_chunks

    def permw_chunk(ci):
        t = tok[ci * tc:(ci + 1) * tc]
        rows = t[:, 0:2].astype(I32)
        c = lax.broadcasted_iota(I32, (tc, LOCAL_ROWS), 1)
        return jnp.where(c == rows[:, 0:1], t[:, 2:3], jnp.where(c == rows[:, 1:2], t[:, 3:4], 0.0)).astype(BF16)

    p_next = permw_chunk(0)
    parts = []
    for ci in range(n_chunks):
        permw = p_next
        if ci + 1 < n_chunks:
            p_next = permw_chunk(ci + 1)
        parts.append(jnp.concatenate([jnp.dot(permw, lo, preferred_element_type=F32),
                                      jnp.dot(permw, hi, preferred_element_type=F32)], axis=1))
    moe = jnp.concatenate(parts, axis=0)
    gate_f = mod_ref[5]
    o_ref[...] = _rms(x1_ref[...] + gate_f * moe, g_ref[...])


def _combine_call(strip_src, strip_dst, strip_n, tile_rows, x1, tok, mod4, final_g, ys, T, tiles_per_batch):
    N, D = x1.shape
    return pl.pallas_call(
        _combine_kernel,
        out_shape=jax.ShapeDtypeStruct((N, D), F32),
        grid_spec=pltpu.PrefetchScalarGridSpec(
            num_scalar_prefetch=4,
            grid=(N // T,),
            in_specs=[pl.BlockSpec((T, D), lambda i, *_: (i, 0)),
                      pl.BlockSpec((T, 4), lambda i, *_: (i, 0)),
                      pl.BlockSpec((None, N_MOD, 1, D), lambda i, *_: (i // tiles_per_batch, 0, 0, 0)),
                      pl.BlockSpec((1, D), lambda i, *_: (0, 0)),
                      pl.BlockSpec(memory_space=pl.ANY)],
            out_specs=pl.BlockSpec((T, D), lambda i, *_: (i, 0)),
            scratch_shapes=[pltpu.VMEM((2, LOCAL_ROWS, D // 2), U32), pltpu.SemaphoreType.DMA((2,))]),
        compiler_params=_cparams("arbitrary"),
        name="combine",
    )(strip_src, strip_dst, strip_n, tile_rows, x1, tok, mod4, final_g, ys)


def _round_up(v, m):
    return (v + m - 1) // m * m


def kernel(x, c, positions, w_mod, b_mod, norm_mix_g, w_in, w_pool, pool_scale, q_norm_g, w_uq, kv_norm_g, w_ukv, w_o, norm_ffn_g, w_group, b_group, w_router, b_router, w_gate_up, w_down, final_g):
    B, S, D = x.shape
    N = B * S
    depth = w_mod.shape[0]
    T = ROW_TILE
    RB = EXPERT_ROWS
    assert depth == 1, "the final RMSNorm is fused into the layer's combine step"
    assert S % T == 0 and S % min(POST_TILE, S) == 0 and min(POST_TILE, S) % T == 0 and S % min(PRE_TILE, S) == 0 and S % min(ATTN_TQ, S) == 0 and min(ATTN_TQ, S) % ATTN_TK == 0
    tiles_per_batch = S // T
    post_tile = min(POST_TILE, S)
    n_tiles = N // T
    nH = MLA_HEADS
    l = 0

    inv_freq = ROPE_THETA ** (-(jnp.arange(0, QK_ROPE_DIM, 2, dtype=F32) / QK_ROPE_DIM))
    posr = positions.reshape(B, 1, S)
    cut3 = POOL_WIDTH + Q_LORA_RANK + KV_LORA_RANK

    mod4 = _mod_call(c, w_mod[l], b_mod[l]).reshape(B, N_MOD, 1, D)

    wi = w_in[l]
    w_in_main = wi[:, :cut3].astype(BF16)
    w_krT = wi[:, cut3:].T.astype(BF16)
    wq = w_uq[l].reshape(Q_LORA_RANK, nH, QK_HEAD_DIM)
    wq_n = wq[:, :, :QK_NOPE_DIM].reshape(Q_LORA_RANK, nH * QK_NOPE_DIM)
    wq_r = wq[:, :, QK_NOPE_DIM:]
    w_uqT = jnp.concatenate([wq_n, wq_r.reshape(Q_LORA_RANK, nH * QK_ROPE_DIM)], axis=1).T.astype(BF16)
    wkv = w_ukv[l].reshape(KV_LORA_RANK, nH, QK_NOPE_DIM + V_HEAD_DIM)
    w_uk = wkv[:, :, :QK_NOPE_DIM].reshape(KV_LORA_RANK, nH * QK_NOPE_DIM).astype(BF16)
    w_uvT = wkv[:, :, QK_NOPE_DIM:].reshape(KV_LORA_RANK, nH * V_HEAD_DIM).T.astype(BF16)
    qg = (q_norm_g[l] * (QK_HEAD_DIM ** -0.5 * LOG2_E)).reshape(1, Q_LORA_RANK)
    wpool_bd = jnp.zeros((POOL_WIDTH, POOL_WIDTH), F32)
    for g in range(len(POOL_WINDOWS)):
        sl = slice(g * POOL_GROUP_DIM, (g + 1) * POOL_GROUP_DIM)
        wpool_bd = wpool_bd.at[sl, sl].set(w_pool[l, g])
    wpool_bd = wpool_bd.astype(BF16)

    qT, k, vT, yp = _pre_call(
        x, posr, mod4, norm_mix_g[l].reshape(1, D), w_in_main, w_krT, inv_freq.reshape(QK_ROPE_DIM // 2, 1),
        wpool_bd, pool_scale[l].reshape(1, POOL_WIDTH), qg, w_uqT, kv_norm_g[l].reshape(1, KV_LORA_RANK),
        w_uk, w_uvT, min(PRE_TILE, S))
    ya = _attn_call(qT, k, vT)

    wo = w_o[l].astype(BF16)
    wgT = jnp.zeros((SUBLANES, D), F32).at[:N_GROUPS].set(w_group[l].T).astype(BF16)
    bg = jnp.zeros((SUBLANES, 1), F32).at[:N_GROUPS, 0].set(b_group[l])
    x1, h2, rows, meta_f, tile_cnt = _post_call(
        x.reshape(N, D), yp.reshape(N, POOL_WIDTH), ya.reshape(N, nH * V_HEAD_DIM), mod4,
        wo[:POOL_WIDTH], wo[POOL_WIDTH:], norm_ffn_g[l].reshape(1, D),
        wgT, bg, w_router[l].T.astype(BF16), b_router[l].reshape(N_EXPERTS, 1), post_tile, S // post_tile)

    run_len = _round_up(tile_cnt[:, :, 0], SUBLANES)
    run_loc = jnp.cumsum(run_len, axis=1) - run_len
    seg_rows = jnp.sum(run_len, axis=0)
    seg_len = _round_up(seg_rows, RB)
    pends = jnp.cumsum(seg_len).astype(I32)
    pstarts = pends - seg_len
    run_dst = pstarts[None, :] + jnp.cumsum(run_len, axis=0) - run_len
    pad_len = (seg_len - seg_rows).astype(I32)
    pad_dst = (pstarts + seg_rows).astype(I32)
    bits = jnp.arange(STRIP_BITS, dtype=I32)
    size = SUBLANES << bits
    valid = (run_len[:, None, :] & size[None, :, None]) != 0
    done = run_len[:, None, :] & ~(2 * size[None, :, None] - 1)
    flat = lambda a: a.reshape(n_tiles, STRIP_BITS * N_EXPERTS)
    valid_f = flat(valid)
    pos = jnp.cumsum(valid_f.astype(I32), axis=1) - valid_f.astype(I32)
    pick = valid_f[:, None, :] & (pos[:, None, :] == jnp.arange(MAX_STRIPS, dtype=I32)[None, :, None])
    gather = lambda a: jnp.sum(jnp.where(pick, flat(a)[:, None, :], 0), axis=-1).reshape(-1).astype(I32)
    strip_src = gather(run_loc[:, None, :] + done)
    strip_dst = gather(run_dst[:, None, :] + done)
    strip_n = jnp.sum(valid, axis=2).reshape(-1).astype(I32)
    tile_rows = jnp.sum(run_len, axis=1).astype(I32)
    P_pad = _round_up(2 * N + n_tiles * N_EXPERTS * (SUBLANES - 1), RB) + N_EXPERTS * RB
    n_rb = P_pad // RB
    n_used = (pends[-1:] // RB).astype(I32)
    block_start = jnp.minimum(jnp.arange(n_rb, dtype=I32), n_used - 1) * RB
    block_e = jnp.sum((pends[None, :] <= block_start[:, None]).astype(I32), axis=1)
    nonempty = seg_len > 0
    seg_index = jnp.cumsum(nonempty.astype(I32)) - 1
    n_seg = jnp.sum(nonempty.astype(I32)).reshape(1)
    experts = jnp.arange(N_EXPERTS, dtype=I32)
    block_seg = jnp.sum(jnp.where(block_e[:, None] == experts[None, :], seg_index[None, :], 0), axis=1).astype(I32)
    seg_expert = jnp.sum(jnp.where(nonempty[None, :] & (seg_index[None, :] == experts[:, None]), experts[None, :], 0),
                         axis=1).astype(I32)

    xs = _dispatch_call(strip_src, strip_dst, strip_n, tile_rows, pends, pad_len, pad_dst, rows, h2, P_pad, T, RB)
    ys = _expert_call(block_seg, seg_expert, n_seg, n_used, xs, w_gate_up[l], w_down[l], RB)
    out = _combine_call(strip_src, strip_dst, strip_n, tile_rows, x1, meta_f.T, mod4, final_g.reshape(1, D), ys,
                        T, tiles_per_batch)
    return out.reshape(B, S, D)
```

```python
import jax
import jax.numpy as jnp
from jax import lax
from jax.experimental import pallas as pl
from jax.experimental.pallas import tpu as pltpu

F32 = jnp.float32
BF16 = jnp.bfloat16
U32 = jnp.uint32
I32 = jnp.int32

POOL_WINDOWS = (2, 4, 8, 16)
POOL_GROUP_DIM = 64
POOL_WIDTH = 256
MLA_HEADS = 6
QK_NOPE_DIM = 128
QK_ROPE_DIM = 64
QK_HEAD_DIM = QK_NOPE_DIM + QK_ROPE_DIM
V_HEAD_DIM = 128
V_ROWS = 144
Q_LORA_RANK = 512
KV_LORA_RANK = 256
ROPE_THETA = 10000.0
N_GROUPS = 4
EXPERTS_PER_GROUP = 8
N_EXPERTS = N_GROUPS * EXPERTS_PER_GROUP
D_EXPERT = 256
N_MOD = 6
EPS = 1e-6

SUBLANES = 8
LANES = 128
assert EXPERTS_PER_GROUP == SUBLANES and N_GROUPS <= SUBLANES
assert POOL_WINDOWS == (2, 4, 8, 16) and POOL_GROUP_DIM * len(POOL_WINDOWS) == POOL_WIDTH
MOD_COLS = 512
POOL_HALO = 32
ROW_TILE = 512
PRE_TILE = 1024
POST_TILE = 1024
ATTN_TQ = 2048
ATTN_TK = 512
EXPERT_ROWS = 1024
LOCAL_ROWS = 2 * ROW_TILE + 256
assert LOCAL_ROWS >= 2 * ROW_TILE + N_EXPERTS * (SUBLANES - 1)
STRIP_BITS = (2 * ROW_TILE // SUBLANES).bit_length()
MAX_STRIPS = 128
assert MAX_STRIPS >= 2 * N_EXPERTS + (LOCAL_ROWS // SUBLANES - 3 * N_EXPERTS) // 4
DISPATCH_SLOTS = 2
VMEM_LIMIT = 56 * 1024 * 1024
NEG_BIG = -1e30
LOG2_E = 1.4426950408889634
HI_MASK = 0xFFFF0000


def _cparams(*sem):
    return pltpu.CompilerParams(dimension_semantics=sem, vmem_limit_bytes=VMEM_LIMIT)


def _rms(x, g):
    return x * lax.rsqrt(jnp.mean(x * x, axis=-1, keepdims=True) + EPS) * g


def _pack_bf16_pair(lo, hi):
    return lax.bitcast_convert_type(hi, U32) | (lax.bitcast_convert_type(lo, U32) >> 16)


def _unpack_bf16_pair(w):
    lo = lax.bitcast_convert_type(w << 16, F32).astype(BF16)
    hi = lax.bitcast_convert_type(w & jnp.uint32(HI_MASK), F32).astype(BF16)
    return lo, hi


def _mod_kernel(c_ref, w_ref, b_ref, o_ref):
    c = c_ref[...]
    ca = c / (1.0 + jnp.exp(-c))
    o_ref[...] = jnp.dot(ca.astype(BF16), w_ref[...].astype(BF16), preferred_element_type=F32) + b_ref[...]


def _mod_call(c, w_mod, b_mod):
    B, D = c.shape
    n_out = w_mod.shape[1]
    tn = MOD_COLS
    return pl.pallas_call(
        _mod_kernel,
        out_shape=jax.ShapeDtypeStruct((B, n_out), F32),
        grid=(n_out // tn,),
        in_specs=[pl.BlockSpec((B, D), lambda j: (0, 0)),
                  pl.BlockSpec((D, tn), lambda j: (0, j)),
                  pl.BlockSpec((1, tn), lambda j: (0, j))],
        out_specs=pl.BlockSpec((B, tn), lambda j: (0, j)),
        compiler_params=_cparams("arbitrary"),
        name="mod",
    )(c, w_mod, b_mod.reshape(1, n_out))


def _pre_kernel(x_ref, posr_ref, mod_ref, g_ref, win_ref, wkrT_ref, invfc_ref, wpool_ref, pscale_ref,
                qg_ref, wuqT_ref, kvg_ref, wuk_ref, wuvT_ref,
                qT_ref, k_ref, vT_ref, yp_ref,
                pbuf, b2, b4, b8):
    T = x_ref.shape[0]
    H = POOL_HALO
    i = pl.program_id(1)
    nt = (((1,), (1,)), ((), ()))

    shift = mod_ref[0]
    scale = mod_ref[1]
    hb = (_rms(x_ref[...], g_ref[...]) * (1.0 + scale) + shift).astype(BF16)
    u = jnp.dot(hb, win_ref[...], preferred_element_type=F32)
    krT = lax.dot_general(wkrT_ref[...], hb, nt, preferred_element_type=F32)

    p = u[:, :POOL_WIDTH]

    @pl.when(i == 0)
    def _():
        pbuf[0:H, :] = jnp.zeros((H, POOL_WIDTH), F32)

    pbuf[H:H + T, :] = p
    b2[8:T + H, :] = pbuf[8:T + H, :] + pbuf[7:T + H - 1, :]
    b4[16:T + H, :] = b2[16:T + H, :] + b2[14:T + H - 2, :]
    b8[24:T + H, :] = b4[24:T + H, :] + b4[20:T + H - 4, :]
    s2 = b2[H:T + H, :]
    s4 = b4[H:T + H, :]
    s8 = b8[H:T + H, :]
    s16 = b8[H:T + H, :] + b8[H - 8:T + H - 8, :]
    pbuf[0:H, :] = pbuf[T:T + H, :]

    lane = lax.broadcasted_iota(I32, (T, POOL_WIDTH), 1)
    t1 = (lax.broadcasted_iota(I32, (T, 1), 0) + (i * T + 1)).astype(F32)
    inv2 = 1.0 / jnp.minimum(t1, 2.0)
    inv4 = 1.0 / jnp.minimum(t1, 4.0)
    inv8 = 1.0 / jnp.minimum(t1, 8.0)
    inv16 = 1.0 / jnp.minimum(t1, 16.0)
    gd = POOL_GROUP_DIM
    mean = jnp.where(lane < gd, s2 * inv2,
                     jnp.where(lane < 2 * gd, s4 * inv4,
                               jnp.where(lane < 3 * gd, s8 * inv8, s16 * inv16)))
    pooled = mean - p
    yp = jnp.dot(pooled.astype(BF16), wpool_ref[...], preferred_element_type=F32) * pscale_ref[...]
    yp_ref[...] = yp.astype(yp_ref.dtype)

    ang = invfc_ref[...] * posr_ref[...].astype(F32)
    cos_h = jnp.cos(ang)
    sin_h = jnp.sin(ang)
    hr = QK_ROPE_DIM // 2

    def rope(xt):
        x1, x2 = xt[0:hr], xt[hr:]
        return x1 * cos_h - x2 * sin_h, x2 * cos_h + x1 * sin_h

    k_rope = jnp.concatenate(rope(krT), axis=0).T.astype(BF16)

    cq = u[:, POOL_WIDTH:POOL_WIDTH + Q_LORA_RANK]
    ckv = u[:, POOL_WIDTH + Q_LORA_RANK:POOL_WIDTH + Q_LORA_RANK + KV_LORA_RANK]
    cqn = _rms(cq, qg_ref[...]).astype(BF16)
    ckvn = _rms(ckv, kvg_ref[...]).astype(BF16)
    qaT = lax.dot_general(wuqT_ref[...], cqn, nt, preferred_element_type=F32)
    kn = jnp.dot(ckvn, wuk_ref[...], preferred_element_type=F32)
    vT = lax.dot_general(wuvT_ref[...], ckvn, nt, preferred_element_type=F32)
    nq = MLA_HEADS * QK_NOPE_DIM
    ones_rows = jnp.where(lax.broadcasted_iota(I32, (V_ROWS - V_HEAD_DIM, T), 0) == 0, 1.0, 0.0).astype(BF16)
    for hd in range(MLA_HEADS):
        q1, q2 = rope(qaT[nq + hd * QK_ROPE_DIM:nq + (hd + 1) * QK_ROPE_DIM, :])
        qT_ref[hd, 0:QK_NOPE_DIM, :] = qaT[hd * QK_NOPE_DIM:(hd + 1) * QK_NOPE_DIM, :].astype(BF16)
        qT_ref[hd, QK_NOPE_DIM:QK_NOPE_DIM + hr, :] = q1.astype(BF16)
        qT_ref[hd, QK_NOPE_DIM + hr:QK_HEAD_DIM, :] = q2.astype(BF16)
        k_ref[hd, :, 0:QK_NOPE_DIM] = kn[:, hd * QK_NOPE_DIM:(hd + 1) * QK_NOPE_DIM].astype(BF16)
        k_ref[hd, :, QK_NOPE_DIM:QK_HEAD_DIM] = k_rope
        vT_ref[hd, 0:V_HEAD_DIM, :] = vT[hd * V_HEAD_DIM:(hd + 1) * V_HEAD_DIM, :].astype(BF16)
        vT_ref[hd, V_HEAD_DIM:V_ROWS, :] = ones_rows


def _pre_call(x, posr, mod4, norm_g, w_in_main, w_krT, inv_freq_col, wpool_bd, pool_scale, qg, w_uqT, kvg,
              w_uk, w_uvT, T):
    B, S, D = x.shape
    nH = MLA_HEADS
    const = lambda shape: pl.BlockSpec(shape, lambda b, i: (0,) * len(shape))
    return pl.pallas_call(
        _pre_kernel,
        out_shape=(jax.ShapeDtypeStruct((B, nH, QK_HEAD_DIM, S), BF16),
                   jax.ShapeDtypeStruct((B, nH, S, QK_HEAD_DIM), BF16),
                   jax.ShapeDtypeStruct((B, nH, V_ROWS, S), BF16),
                   jax.ShapeDtypeStruct((B, S, POOL_WIDTH), BF16)),
        grid=(B, S // T),
        in_specs=[pl.BlockSpec((None, T, D), lambda b, i: (b, i, 0)),
                  pl.BlockSpec((None, 1, T), lambda b, i: (b, 0, i)),
                  pl.BlockSpec((None, N_MOD, 1, D), lambda b, i: (b, 0, 0, 0)),
                  const((1, D)),
                  const(w_in_main.shape),
                  const(w_krT.shape),
                  const(inv_freq_col.shape),
                  const(wpool_bd.shape),
                  const((1, POOL_WIDTH)),
                  const((1, Q_LORA_RANK)),
                  const(w_uqT.shape),
                  const((1, KV_LORA_RANK)),
                  const(w_uk.shape),
                  const(w_uvT.shape)],
        out_specs=(pl.BlockSpec((None, nH, QK_HEAD_DIM, T), lambda b, i: (b, 0, 0, i)),
                   pl.BlockSpec((None, nH, T, QK_HEAD_DIM), lambda b, i: (b, 0, i, 0)),
                   pl.BlockSpec((None, nH, V_ROWS, T), lambda b, i: (b, 0, 0, i)),
                   pl.BlockSpec((None, T, POOL_WIDTH), lambda b, i: (b, i, 0))),
        scratch_shapes=[pltpu.VMEM((T + POOL_HALO, POOL_WIDTH), F32)] * 4,
        compiler_params=_cparams("arbitrary", "arbitrary"),
        name="pre",
    )(x, posr, mod4, norm_g, w_in_main, w_krT, inv_freq_col, wpool_bd, pool_scale, qg, w_uqT, kvg, w_uk, w_uvT)


def _attn_kernel(qT_ref, k_ref, vT_ref, o_ref):
    S = k_ref.shape[0]
    tq, tk = min(ATTN_TQ, S), ATTN_TK
    diag = lax.broadcasted_iota(I32, (tk, tk), 0) <= lax.broadcasted_iota(I32, (tk, tk), 1)
    blocks = [(i, j) for i in range(S // tq) for j in range((i + 1) * (tq // tk))]

    def scores(i, j):
        d = max(j * tk - i * tq, 0)
        sT = jnp.dot(k_ref[j * tk:(j + 1) * tk, :], qT_ref[:, i * tq + d:(i + 1) * tq],
                     preferred_element_type=F32)
        if j * tk >= i * tq:
            masked = jnp.where(diag, sT[:, :tk], NEG_BIG)
            sT = masked if sT.shape[1] == tk else jnp.concatenate([masked, sT[:, tk:]], axis=1)
        return sT

    s_next = scores(*blocks[0])
    m = acc = None
    for n, (i, j) in enumerate(blocks):
        sT = s_next
        if n + 1 < len(blocks):
            s_next = scores(*blocks[n + 1])
        if j == 0:
            m = jnp.full((1, tq), NEG_BIG, F32)
            acc = jnp.zeros((V_ROWS, tq), F32)
        d = max(j * tk - i * tq, 0)
        vT = vT_ref[:, j * tk:(j + 1) * tk]
        m_new = jnp.maximum(m[:, d:], jnp.max(sT, axis=0, keepdims=True))
        pT = jnp.exp2(sT - m_new).astype(BF16)
        acc_new = jnp.exp2(m[:, d:] - m_new) * acc[:, d:] + jnp.dot(vT, pT, preferred_element_type=F32)
        if d:
            m_new = jnp.concatenate([m[:, :d], m_new], axis=1)
            acc_new = jnp.concatenate([acc[:, :d], acc_new], axis=1)
        m, acc = m_new, acc_new
        if j == (i + 1) * (tq // tk) - 1:
            out = acc[0:V_HEAD_DIM] * (1.0 / acc[V_HEAD_DIM:V_HEAD_DIM + 1])
            o_ref[i * tq:(i + 1) * tq, :] = out.T.astype(o_ref.dtype)


def _attn_call(qT, k, vT):
    B, nH, S, _ = k.shape
    return pl.pallas_call(
        _attn_kernel,
        out_shape=jax.ShapeDtypeStruct((B, S, nH * V_HEAD_DIM), BF16),
        grid=(B, nH),
        in_specs=[pl.BlockSpec((None, None, QK_HEAD_DIM, S), lambda b, h: (b, h, 0, 0)),
                  pl.BlockSpec((None, None, S, QK_HEAD_DIM), lambda b, h: (b, h, 0, 0)),
                  pl.BlockSpec((None, None, V_ROWS, S), lambda b, h: (b, h, 0, 0))],
        out_specs=pl.BlockSpec((None, S, V_HEAD_DIM), lambda b, h: (b, 0, h)),
        compiler_params=_cparams("arbitrary", "arbitrary"),
        name="attn",
    )(qT, k, vT)


def _route_tile(hb, wgT_ref, bg_ref, wrT_ref, br_ref):
    T = hb.shape[0]
    nt = (((1,), (1,)), ((), ()))
    gl = lax.dot_general(wgT_ref[...], hb, nt, preferred_element_type=F32)
    el = lax.dot_general(wrT_ref[...], hb, nt, preferred_element_type=F32)

    r8 = lax.broadcasted_iota(I32, (SUBLANES, T), 0)
    gvalid = r8 < N_GROUPS
    gmax = jnp.max(jnp.where(gvalid, gl, NEG_BIG), axis=0, keepdims=True)
    gexp = jnp.where(gvalid, jnp.exp(gl - gmax), 0.0)
    g_prob = gexp / jnp.sum(gexp, axis=0, keepdims=True)
    gb = jnp.where(gvalid, gl + bg_ref[...], NEG_BIG)
    gbmax = jnp.max(gb, axis=0, keepdims=True)
    g_sel = jnp.min(jnp.where(gb == gbmax, r8, SUBLANES), axis=0, keepdims=True)
    gp = jnp.sum(jnp.where(r8 == g_sel, g_prob, 0.0), axis=0, keepdims=True)

    e_in = jnp.zeros((EXPERTS_PER_GROUP, T), F32)
    b_in = jnp.zeros((EXPERTS_PER_GROUP, T), F32)
    br = br_ref[...]
    for g in range(N_GROUPS):
        sel = g_sel == g
        e_in = jnp.where(sel, el[g * EXPERTS_PER_GROUP:(g + 1) * EXPERTS_PER_GROUP, :], e_in)
        b_in = jnp.where(sel, br[g * EXPERTS_PER_GROUP:(g + 1) * EXPERTS_PER_GROUP, :], b_in)
    eb = e_in + b_in
    m1 = jnp.max(eb, axis=0, keepdims=True)
    i1 = jnp.min(jnp.where(eb == m1, r8, SUBLANES), axis=0, keepdims=True)
    eb2 = jnp.where(r8 == i1, NEG_BIG, eb)
    m2 = jnp.max(eb2, axis=0, keepdims=True)
    i2 = jnp.min(jnp.where(eb2 == m2, r8, SUBLANES), axis=0, keepdims=True)
    emax = jnp.max(e_in, axis=0, keepdims=True)
    eexp = jnp.exp(e_in - emax)
    sp = eexp / jnp.sum(eexp, axis=0, keepdims=True)
    p1 = jnp.sum(jnp.where(r8 == i1, sp, 0.0), axis=0, keepdims=True)
    p2 = jnp.sum(jnp.where(r8 == i2, sp, 0.0), axis=0, keepdims=True)
    tot = p1 + p2
    w1 = gp * (p1 / tot)
    w2 = gp * (p2 / tot)
    e1 = g_sel * EXPERTS_PER_GROUP + i1
    e2 = g_sel * EXPERTS_PER_GROUP + i2

    r32 = lax.broadcasted_iota(I32, (N_EXPERTS, T), 0)
    oh1 = r32 == e1
    oh2 = r32 == e2
    oh = jnp.where(oh1 | oh2, 1.0, 0.0)
    upper = jnp.where(lax.broadcasted_iota(I32, (T, T), 0) < lax.broadcasted_iota(I32, (T, T), 1),
                      1.0, 0.0).astype(BF16)
    before = jnp.dot(oh.astype(BF16), upper, preferred_element_type=F32)
    cnt = jnp.sum(oh, axis=1, keepdims=True)
    run8 = jnp.floor((cnt + (SUBLANES - 1.0)) * (1.0 / SUBLANES))
    lower = jnp.where(lax.broadcasted_iota(I32, (N_EXPERTS, N_EXPERTS), 1)
                      < lax.broadcasted_iota(I32, (N_EXPERTS, N_EXPERTS), 0), 1.0, 0.0).astype(BF16)
    run_start = jnp.dot(lower, jnp.broadcast_to(run8, (N_EXPERTS, LANES)).astype(BF16),
                        preferred_element_type=F32)[:, 0:1] * float(SUBLANES)
    pos = before + run_start
    row1 = jnp.sum(jnp.where(oh1, pos, 0.0), axis=0, keepdims=True)
    row2 = jnp.sum(jnp.where(oh2, pos, 0.0), axis=0, keepdims=True)
    return row1, row2, w1, w2, cnt


def _post_kernel(x_ref, yp_ref, ya_ref, mod_ref, wo_ref, g_ref,
                 wgT_ref, bg_ref, wrT_ref, br_ref,
                 x1_ref, h2_ref, rows_ref, mf_ref, cnt_ref):
    T = ROW_TILE
    gate_a = mod_ref[2]
    shift_f = mod_ref[3]
    scale_f = mod_ref[4]
    mix = jnp.dot(jnp.concatenate([yp_ref[...], ya_ref[...]], axis=1), wo_ref[...], preferred_element_type=F32)
    x1 = x_ref[...] + gate_a * mix
    x1_ref[...] = x1
    hb = (_rms(x1, g_ref[...]) * (1.0 + scale_f) + shift_f).astype(BF16)
    h2_ref[...] = hb

    for s in range(x_ref.shape[0] // T):
        cols = slice(s * T, (s + 1) * T)
        row1, row2, w1, w2, cnt = _route_tile(hb[cols], wgT_ref, bg_ref, wrT_ref, br_ref)
        cnt_ref[s] = cnt.astype(I32)
        rows_ref[0:1, cols] = row1.astype(I32)
        rows_ref[1:2, cols] = row2.astype(I32)
        zero = jnp.zeros_like(w1)
        mf_ref[cols, :] = jnp.concatenate([row1, row2, w1, w2, zero, zero, zero, zero], axis=0).T


def _post_call(x2, yp2, ya2, mod4, wo, g, wgT, bg, wrT, br, T, steps_per_batch):
    N, D = x2.shape
    const = lambda shape: pl.BlockSpec(shape, lambda i: (0,) * len(shape))
    return pl.pallas_call(
        _post_kernel,
        out_shape=(jax.ShapeDtypeStruct((N, D), F32),
                   jax.ShapeDtypeStruct((N, D), BF16),
                   jax.ShapeDtypeStruct((2, N), I32),
                   jax.ShapeDtypeStruct((N, SUBLANES), F32),
                   jax.ShapeDtypeStruct((N // ROW_TILE, N_EXPERTS, 1), I32)),
        grid=(N // T,),
        in_specs=[pl.BlockSpec((T, D), lambda i: (i, 0)),
                  pl.BlockSpec((T, POOL_WIDTH), lambda i: (i, 0)),
                  pl.BlockSpec((T, MLA_HEADS * V_HEAD_DIM), lambda i: (i, 0)),
                  pl.BlockSpec((None, N_MOD, 1, D), lambda i: (i // steps_per_batch, 0, 0, 0)),
                  const(wo.shape), const((1, D)),
                  const(wgT.shape), const(bg.shape), const(wrT.shape), const(br.shape)],
        out_specs=(pl.BlockSpec((T, D), lambda i: (i, 0)),
                   pl.BlockSpec((T, D), lambda i: (i, 0)),
                   pl.BlockSpec((2, T), lambda i: (0, i)),
                   pl.BlockSpec((T, SUBLANES), lambda i: (i, 0)),
                   pl.BlockSpec((T // ROW_TILE, N_EXPERTS, 1), lambda i: (i, 0, 0))),
        compiler_params=_cparams("arbitrary"),
        name="post",
    )(x2, yp2, ya2, mod4, wo, g, wgT, bg, wrT, br)


def _for_each_strip(tile, src_ref, dst_ref, n_ref, fn):
    first = tile * MAX_STRIPS
    for b in range(STRIP_BITS):
        last = first + n_ref[tile * STRIP_BITS + b]

        def body(r, c, n=SUBLANES << b):
            fn(pl.multiple_of(src_ref[r], SUBLANES), pl.multiple_of(dst_ref[r], SUBLANES), n)
            return c

        lax.fori_loop(first, last, body, 0)
        first = last


def _wait_strip_rows(n_rows, make):
    for b in range((LOCAL_ROWS // SUBLANES).bit_length()):
        n = SUBLANES << b

        @pl.when((n_rows & n) != 0)
        def _():
            make(n).wait()


def _dispatch_kernel(src_ref, dst_ref, n_ref, trows_ref, pends_ref, padlen_ref, paddst_ref, rows_ref, h_ref, xs_ref,
                     lbuf, zbuf, sems, zsem):
    T = h_ref.shape[0]
    RB = zbuf.shape[0]
    half = h_ref.shape[1] // 2
    step = pl.program_id(0)
    n_steps = pl.num_programs(0)
    slot = step % DISPATCH_SLOTS

    def zero_fill(act, part, n_parts):
        def per_expert(e, c):
            pad = padlen_ref[e]
            dst = paddst_ref[e]
            for b in reversed(range((RB // SUBLANES).bit_length() - 1)):
                n = SUBLANES << b
                done = pad & ~(2 * n - 1)

                @pl.when((pad & n) != 0)
                def _():
                    act(pltpu.make_async_copy(zbuf.at[pl.ds(0, n)],
                                              xs_ref.at[pl.ds(pl.multiple_of(dst + done, SUBLANES), n)], zsem))
            return c

        def per_block(j, c):
            b = pends_ref[N_EXPERTS - 1] // RB + part + j * n_parts
            act(pltpu.make_async_copy(zbuf, xs_ref.at[pl.ds(pl.multiple_of(b * RB, RB), RB)], zsem))
            return c

        n_tail = xs_ref.shape[0] // RB - pends_ref[N_EXPERTS - 1] // RB
        lax.fori_loop(0, (N_EXPERTS - part + n_parts - 1) // n_parts,
                      lambda j, c: per_expert(part + j * n_parts, c), 0)
        lax.fori_loop(0, (n_tail - part + n_parts - 1) // n_parts, per_block, 0)

    @pl.when(step == 0)
    def _():
        zbuf[...] = jnp.zeros(zbuf.shape, zbuf.dtype)

    zero_fill(lambda cp: cp.start(), step, n_steps)

    r = lax.broadcasted_iota(I32, (LOCAL_ROWS, T), 0)
    perm = jnp.where((r == rows_ref[0:1, :]) | (r == rows_ref[1:2, :]), 1.0, 0.0).astype(BF16)
    h = h_ref[...]
    lo = jnp.dot(perm, h[:, :half], preferred_element_type=F32)
    hi = jnp.dot(perm, h[:, half:], preferred_element_type=F32)
    lbuf[slot] = _pack_bf16_pair(lo, hi)

    def strip(tile_slot):
        def make(loc, dst, n):
            return pltpu.make_async_copy(lbuf.at[tile_slot, pl.ds(loc, n)], xs_ref.at[pl.ds(dst, n)],
                                         sems.at[tile_slot])
        return make

    def wait_tile(tile):
        _wait_strip_rows(trows_ref[tile], lambda n: strip(tile % DISPATCH_SLOTS)(0, 0, n))

    _for_each_strip(step, src_ref, dst_ref, n_ref, lambda *a: strip(slot)(*a).start())

    @pl.when(step >= DISPATCH_SLOTS - 1)
    def _():
        wait_tile(step - (DISPATCH_SLOTS - 1))

    @pl.when(step == n_steps - 1)
    def _():
        for back in reversed(range(DISPATCH_SLOTS - 1)):
            @pl.when(step >= back)
            def _():
                wait_tile(step - back)
        zero_fill(lambda cp: cp.wait(), 0, 1)


def _dispatch_call(strip_src, strip_dst, strip_n, tile_rows, pends, pad_len, pad_dst, rows, h2, P_pad, T, RB):
    N, D = h2.shape
    return pl.pallas_call(
        _dispatch_kernel,
        out_shape=jax.ShapeDtypeStruct((P_pad, D // 2), U32),
        grid_spec=pltpu.PrefetchScalarGridSpec(
            num_scalar_prefetch=7,
            grid=(N // T,),
            in_specs=[pl.BlockSpec((2, T), lambda i, *_: (0, i)),
                      pl.BlockSpec((T, D), lambda i, *_: (i, 0))],
            out_specs=pl.BlockSpec(memory_space=pl.ANY),
            scratch_shapes=[pltpu.VMEM((DISPATCH_SLOTS, LOCAL_ROWS, D // 2), U32),
                            pltpu.VMEM((RB, D // 2), U32),
                            pltpu.SemaphoreType.DMA((DISPATCH_SLOTS,)),
                            pltpu.SemaphoreType.DMA]),
        compiler_params=_cparams("arbitrary"),
        name="dispatch",
    )(strip_src, strip_dst, strip_n, tile_rows, pends, pad_len, pad_dst, rows, h2)


def _expert_kernel(seg_ref, sege_ref, nseg_ref, nused_ref, xs_ref, wgu_hbm, wd_hbm, ys_ref,
                   wgu_f32, wd_f32, wgu_bf, wd_bf, sems):
    i = pl.program_id(0)
    used = i < nused_ref[0]
    half = wgu_bf.shape[0] // 2
    k = seg_ref[i]

    def weight_copies(kk):
        slot = kk % 2
        e = sege_ref[kk]
        return (pltpu.make_async_copy(wgu_hbm.at[e], wgu_f32.at[slot], sems.at[0, slot]),
                pltpu.make_async_copy(wd_hbm.at[e], wd_f32.at[slot], sems.at[1, slot]))

    @pl.when(i == 0)
    def _():
        for cp in weight_copies(0):
            cp.start()

    @pl.when(used & ((i == 0) | (k != seg_ref[jnp.maximum(i - 1, 0)])))
    def _():
        for cp in weight_copies(k):
            cp.wait()

        @pl.when(k + 1 < nseg_ref[0])
        def _():
            for cp in weight_copies(k + 1):
                cp.start()

        wgu_bf[...] = wgu_f32[k % 2].astype(BF16)
        wd_bf[...] = wd_f32[k % 2].astype(BF16)

    @pl.when(used)
    def _():
        lo, hi = _unpack_bf16_pair(xs_ref[...])
        gu = (jnp.dot(lo, wgu_bf[0:half, :], preferred_element_type=F32)
              + jnp.dot(hi, wgu_bf[half:, :], preferred_element_type=F32))
        gate = gu[:, :D_EXPERT]
        up = gu[:, D_EXPERT:]
        act = gate / (1.0 + jnp.exp(-gate)) * up
        y = jnp.dot(act.astype(BF16), wd_bf[...], preferred_element_type=F32)
        yb = y.astype(BF16).astype(F32)
        ys_ref[...] = _pack_bf16_pair(yb[:, :half], yb[:, half:])

    @pl.when(jnp.logical_not(used))
    def _():
        ys_ref[...] = jnp.zeros(ys_ref.shape, ys_ref.dtype)


def _expert_call(block_seg, seg_expert, n_seg, n_used, xs, wgu, wd, RB):
    P, Dh = xs.shape
    D = 2 * Dh
    row_map = lambda i, *_: (i, 0)
    in_map = lambda i, seg, sege, nseg, nu: (jnp.minimum(i, nu[0] - 1), 0)
    return pl.pallas_call(
        _expert_kernel,
        out_shape=jax.ShapeDtypeStruct((P, Dh), U32),
        grid_spec=pltpu.PrefetchScalarGridSpec(
            num_scalar_prefetch=4,
            grid=(P // RB,),
            in_specs=[pl.BlockSpec((RB, Dh), in_map),
                      pl.BlockSpec(memory_space=pl.ANY),
                      pl.BlockSpec(memory_space=pl.ANY)],
            out_specs=pl.BlockSpec((RB, Dh), row_map),
            scratch_shapes=[pltpu.VMEM((2, D, 2 * D_EXPERT), F32), pltpu.VMEM((2, D_EXPERT, D), F32),
                            pltpu.VMEM((D, 2 * D_EXPERT), BF16), pltpu.VMEM((D_EXPERT, D), BF16),
                            pltpu.SemaphoreType.DMA((2, 2))]),
        compiler_params=_cparams("arbitrary"),
        name="experts",
    )(block_seg, seg_expert, n_seg, n_used, xs, wgu, wd)


def _combine_kernel(src_ref, dst_ref, n_ref, trows_ref, x1_ref, tok_ref, mod_ref, g_ref, ys_ref, o_ref, ybuf, sems):
    T = x1_ref.shape[0]
    step = pl.program_id(0)
    n_steps = pl.num_programs(0)
    slot = step % 2

    def strip(tile_slot):
        def make(loc, dst, n):
            return pltpu.make_async_copy(ys_ref.at[pl.ds(dst, n)], ybuf.at[tile_slot, pl.ds(loc, n)],
                                         sems.at[tile_slot])
        return make

    @pl.when(step == 0)
    def _():
        ybuf[...] = jnp.zeros(ybuf.shape, ybuf.dtype)
        _for_each_strip(step, src_ref, dst_ref, n_ref, lambda *a: strip(slot)(*a).start())

    @pl.when(step + 1 < n_steps)
    def _():
        _for_each_strip(step + 1, src_ref, dst_ref, n_ref, lambda *a: strip(1 - slot)(*a).start())

    _wait_strip_rows(trows_ref[step], lambda n: strip(slot)(0, 0, n))

    tok = tok_ref[...]
    lo, hi = _unpack_bf16_pair(ybuf[slot])
    n_chunks = 4
    tc = T // n_chunks

    def permw_chunk(ci):
        t = tok[ci * tc:(ci + 1) * tc]
        rows = t[:, 0:2].astype(I32)
        c = lax.broadcasted_iota(I32, (tc, LOCAL_ROWS), 1)
        return jnp.where(c == rows[:, 0:1], t[:, 2:3], jnp.where(c == rows[:, 1:2], t[:, 3:4], 0.0)).astype(BF16)

    p_next = permw_chunk(0)
    parts = []
    for ci in range(n_chunks):
        permw = p_next
        if ci + 1 < n_chunks:
            p_next = permw_chunk(ci + 1)
        parts.append(jnp.concatenate([jnp.dot(permw, lo, preferred_element_type=F32),
                                      jnp.dot(permw, hi, preferred_element_type=F32)], axis=1))
    moe = jnp.concatenate(parts, axis=0)
    gate_f = mod_ref[5]
    o_ref[...] = _rms(x1_ref[...] + gate_f * moe, g_ref[...])


def _combine_call(strip_src, strip_dst, strip_n, tile_rows, x1, tok, mod4, final_g, ys, T, tiles_per_batch):
    N, D = x1.shape
    return pl.pallas_call(
        _combine_kernel,
        out_shape=jax.ShapeDtypeStruct((N, D), F32),
        grid_spec=pltpu.PrefetchScalarGridSpec(
            num_scalar_prefetch=4,
            grid=(N // T,),
            in_specs=[pl.BlockSpec((T, D), lambda i, *_: (i, 0)),
                      pl.BlockSpec((T, SUBLANES), lambda i, *_: (i, 0)),
                      pl.BlockSpec((None, N_MOD, 1, D), lambda i, *_: (i // tiles_per_batch, 0, 0, 0)),
                      pl.BlockSpec((1, D), lambda i, *_: (0, 0)),
                      pl.BlockSpec(memory_space=pl.ANY)],
            out_specs=pl.BlockSpec((T, D), lambda i, *_: (i, 0)),
            scratch_shapes=[pltpu.VMEM((2, LOCAL_ROWS, D // 2), U32), pltpu.SemaphoreType.DMA((2,))]),
        compiler_params=_cparams("arbitrary"),
        name="combine",
    )(strip_src, strip_dst, strip_n, tile_rows, x1, tok, mod4, final_g, ys)


def _round_up(v, m):
    return (v + m - 1) // m * m


def kernel(x, c, positions, w_mod, b_mod, norm_mix_g, w_in, w_pool, pool_scale, q_norm_g, w_uq, kv_norm_g, w_ukv, w_o, norm_ffn_g, w_group, b_group, w_router, b_router, w_gate_up, w_down, final_g):
    B, S, D = x.shape
    N = B * S
    depth = w_mod.shape[0]
    T = ROW_TILE
    RB = EXPERT_ROWS
    assert depth == 1, "the final RMSNorm is fused into the layer's combine step"
    assert S % T == 0 and S % min(POST_TILE, S) == 0 and min(POST_TILE, S) % T == 0 and S % min(PRE_TILE, S) == 0 and S % min(ATTN_TQ, S) == 0 and min(ATTN_TQ, S) % ATTN_TK == 0
    tiles_per_batch = S // T
    post_tile = min(POST_TILE, S)
    n_tiles = N // T
    nH = MLA_HEADS
    l = 0

    inv_freq = ROPE_THETA ** (-(jnp.arange(0, QK_ROPE_DIM, 2, dtype=F32) / QK_ROPE_DIM))
    posr = positions.reshape(B, 1, S)
    cut3 = POOL_WIDTH + Q_LORA_RANK + KV_LORA_RANK

    mod4 = _mod_call(c, w_mod[l], b_mod[l]).reshape(B, N_MOD, 1, D)

    wi = w_in[l]
    w_in_main = wi[:, :cut3].astype(BF16)
    w_krT = wi[:, cut3:].T.astype(BF16)
    wq = w_uq[l].reshape(Q_LORA_RANK, nH, QK_HEAD_DIM)
    wq_n = wq[:, :, :QK_NOPE_DIM].reshape(Q_LORA_RANK, nH * QK_NOPE_DIM)
    wq_r = wq[:, :, QK_NOPE_DIM:]
    w_uqT = jnp.concatenate([wq_n, wq_r.reshape(Q_LORA_RANK, nH * QK_ROPE_DIM)], axis=1).T.astype(BF16)
    wkv = w_ukv[l].reshape(KV_LORA_RANK, nH, QK_NOPE_DIM + V_HEAD_DIM)
    w_uk = wkv[:, :, :QK_NOPE_DIM].reshape(KV_LORA_RANK, nH * QK_NOPE_DIM).astype(BF16)
    w_uvT = wkv[:, :, QK_NOPE_DIM:].reshape(KV_LORA_RANK, nH * V_HEAD_DIM).T.astype(BF16)
    qg = (q_norm_g[l] * (QK_HEAD_DIM ** -0.5 * LOG2_E)).reshape(1, Q_LORA_RANK)
    wpool_bd = jnp.zeros((POOL_WIDTH, POOL_WIDTH), F32)
    for g in range(len(POOL_WINDOWS)):
        sl = slice(g * POOL_GROUP_DIM, (g + 1) * POOL_GROUP_DIM)
        wpool_bd = wpool_bd.at[sl, sl].set(w_pool[l, g])
    wpool_bd = wpool_bd.astype(BF16)

    qT, k, vT, yp = _pre_call(
        x, posr, mod4, norm_mix_g[l].reshape(1, D), w_in_main, w_krT, inv_freq.reshape(QK_ROPE_DIM // 2, 1),
        wpool_bd, pool_scale[l].reshape(1, POOL_WIDTH), qg, w_uqT, kv_norm_g[l].reshape(1, KV_LORA_RANK),
        w_uk, w_uvT, min(PRE_TILE, S))
    ya = _attn_call(qT, k, vT)

    wo = w_o[l].astype(BF16)
    wgT = jnp.zeros((SUBLANES, D), F32).at[:N_GROUPS].set(w_group[l].T).astype(BF16)
    bg = jnp.zeros((SUBLANES, 1), F32).at[:N_GROUPS, 0].set(b_group[l])
    x1, h2, rows, meta_f, tile_cnt = _post_call(
        x.reshape(N, D), yp.reshape(N, POOL_WIDTH), ya.reshape(N, nH * V_HEAD_DIM), mod4,
        wo, norm_ffn_g[l].reshape(1, D),
        wgT, bg, w_router[l].T.astype(BF16), b_router[l].reshape(N_EXPERTS, 1), post_tile, S // post_tile)

    run_len = _round_up(tile_cnt[:, :, 0], SUBLANES)
    run_loc = jnp.cumsum(run_len, axis=1) - run_len
    seg_rows = jnp.sum(run_len, axis=0)
    seg_len = _round_up(seg_rows, RB)
    pends = jnp.cumsum(seg_len).astype(I32)
    pstarts = pends - seg_len
    run_dst = pstarts[None, :] + jnp.cumsum(run_len, axis=0) - run_len
    pad_len = (seg_len - seg_rows).astype(I32)
    pad_dst = (pstarts + seg_rows).astype(I32)
    bits = jnp.arange(STRIP_BITS, dtype=I32)
    size = SUBLANES << bits
    valid = (run_len[:, None, :] & size[None, :, None]) != 0
    done = run_len[:, None, :] & ~(2 * size[None, :, None] - 1)
    flat = lambda a: a.reshape(n_tiles, STRIP_BITS * N_EXPERTS)
    valid_f = flat(valid)
    pos = jnp.cumsum(valid_f.astype(I32), axis=1) - valid_f.astype(I32)
    pick = valid_f[:, None, :] & (pos[:, None, :] == jnp.arange(MAX_STRIPS, dtype=I32)[None, :, None])
    gather = lambda a: jnp.sum(jnp.where(pick, flat(a)[:, None, :], 0), axis=-1).reshape(-1).astype(I32)
    strip_src = gather(run_loc[:, None, :] + done)
    strip_dst = gather(run_dst[:, None, :] + done)
    strip_n = jnp.sum(valid, axis=2).reshape(-1).astype(I32)
    tile_rows = jnp.sum(run_len, axis=1).astype(I32)
    P_pad = _round_up(2 * N + n_tiles * N_EXPERTS * (SUBLANES - 1), RB) + N_EXPERTS * RB
    n_rb = P_pad // RB
    n_used = (pends[-1:] // RB).astype(I32)
    block_start = jnp.minimum(jnp.arange(n_rb, dtype=I32), n_used - 1) * RB
    block_e = jnp.sum((pends[None, :] <= block_start[:, None]).astype(I32), axis=1)
    nonempty = seg_len > 0
    seg_index = jnp.cumsum(nonempty.astype(I32)) - 1
    n_seg = jnp.sum(nonempty.astype(I32)).reshape(1)
    experts = jnp.arange(N_EXPERTS, dtype=I32)
    block_seg = jnp.sum(jnp.where(block_e[:, None] == experts[None, :], seg_index[None, :], 0), axis=1).astype(I32)
    seg_expert = jnp.sum(jnp.where(nonempty[None, :] & (seg_index[None, :] == experts[:, None]), experts[None, :], 0),
                         axis=1).astype(I32)

    xs = _dispatch_call(strip_src, strip_dst, strip_n, tile_rows, pends, pad_len, pad_dst, rows, h2, P_pad, T, RB)
    ys = _expert_call(block_seg, seg_expert, n_seg, n_used, xs, w_gate_up[l], w_down[l], RB)
    out = _combine_call(strip_src, strip_dst, strip_n, tile_rows, x1, meta_f, mod4, final_g.reshape(1, D), ys,
                        T, tiles_per_batch)
    return out.reshape(B, S, D)
```

```python
import jax
import jax.numpy as jnp
from jax import lax
from jax.experimental import pallas as pl
from jax.experimental.pallas import tpu as pltpu

F32 = jnp.float32
BF16 = jnp.bfloat16
U32 = jnp.uint32
I32 = jnp.int32

POOL_WINDOWS = (2, 4, 8, 16)
POOL_GROUP_DIM = 64
POOL_WIDTH = 256
MLA_HEADS = 6
QK_NOPE_DIM = 128
QK_ROPE_DIM = 64
QK_HEAD_DIM = QK_NOPE_DIM + QK_ROPE_DIM
V_HEAD_DIM = 128
V_ROWS = 144
Q_LORA_RANK = 512
KV_LORA_RANK = 256
ROPE_THETA = 10000.0
N_GROUPS = 4
EXPERTS_PER_GROUP = 8
N_EXPERTS = N_GROUPS * EXPERTS_PER_GROUP
D_EXPERT = 256
N_MOD = 6
EPS = 1e-6

SUBLANES = 8
LANES = 128
assert EXPERTS_PER_GROUP == SUBLANES and N_GROUPS <= SUBLANES
assert POOL_WINDOWS == (2, 4, 8, 16) and POOL_GROUP_DIM * len(POOL_WINDOWS) == POOL_WIDTH
MOD_COLS = 512
POOL_HALO = 32
ROW_TILE = 512
PRE_TILE = 1024
POST_TILE = 1024
ATTN_TQ = 2048
ATTN_TK = 512
EXPERT_ROWS = 1024
EXPERT_X_SLOTS = 3
LOCAL_ROWS = 2 * ROW_TILE + 256
assert LOCAL_ROWS >= 2 * ROW_TILE + N_EXPERTS * (SUBLANES - 1)
STRIP_BITS = (2 * ROW_TILE // SUBLANES).bit_length()
MAX_STRIPS = 128
assert MAX_STRIPS >= 2 * N_EXPERTS + (LOCAL_ROWS // SUBLANES - 3 * N_EXPERTS) // 4
DISPATCH_SLOTS = 2
VMEM_LIMIT = 56 * 1024 * 1024
NEG_BIG = -1e30
LOG2_E = 1.4426950408889634
HI_MASK = 0xFFFF0000


def _cparams(*sem):
    return pltpu.CompilerParams(dimension_semantics=sem, vmem_limit_bytes=VMEM_LIMIT)


def _rms(x, g):
    return x * lax.rsqrt(jnp.mean(x * x, axis=-1, keepdims=True) + EPS) * g


def _pack_bf16_pair(lo, hi):
    return lax.bitcast_convert_type(hi, U32) | (lax.bitcast_convert_type(lo, U32) >> 16)


def _unpack_bf16_pair(w):
    lo = lax.bitcast_convert_type(w << 16, F32).astype(BF16)
    hi = lax.bitcast_convert_type(w & jnp.uint32(HI_MASK), F32).astype(BF16)
    return lo, hi


def _mod_kernel(c_ref, w_ref, b_ref, o_ref):
    c = c_ref[...]
    ca = c / (1.0 + jnp.exp(-c))
    o_ref[...] = jnp.dot(ca.astype(BF16), w_ref[...].astype(BF16), preferred_element_type=F32) + b_ref[...]


def _mod_call(c, w_mod, b_mod):
    B, D = c.shape
    n_out = w_mod.shape[1]
    tn = MOD_COLS
    return pl.pallas_call(
        _mod_kernel,
        out_shape=jax.ShapeDtypeStruct((B, n_out), F32),
        grid=(n_out // tn,),
        in_specs=[pl.BlockSpec((B, D), lambda j: (0, 0)),
                  pl.BlockSpec((D, tn), lambda j: (0, j)),
                  pl.BlockSpec((1, tn), lambda j: (0, j))],
        out_specs=pl.BlockSpec((B, tn), lambda j: (0, j)),
        compiler_params=_cparams("arbitrary"),
        name="mod",
    )(c, w_mod, b_mod.reshape(1, n_out))


def _pre_kernel(x_ref, posr_ref, mod_ref, g_ref, win_ref, wkrT_ref, invfc_ref, wpool_ref, pscale_ref,
                qg_ref, wuqT_ref, kvg_ref, wuk_ref, wuvT_ref,
                qT_ref, k_ref, vT_ref, yp_ref,
                pbuf, b2, b4, b8):
    T = x_ref.shape[0]
    H = POOL_HALO
    i = pl.program_id(1)
    nt = (((1,), (1,)), ((), ()))

    shift = mod_ref[0]
    scale = mod_ref[1]
    hb = (_rms(x_ref[...], g_ref[...]) * (1.0 + scale) + shift).astype(BF16)
    u = jnp.dot(hb, win_ref[...], preferred_element_type=F32)
    krT = lax.dot_general(wkrT_ref[...], hb, nt, preferred_element_type=F32)

    p = u[:, :POOL_WIDTH]

    @pl.when(i == 0)
    def _():
        pbuf[0:H, :] = jnp.zeros((H, POOL_WIDTH), F32)

    pbuf[H:H + T, :] = p
    b2[8:T + H, :] = pbuf[8:T + H, :] + pbuf[7:T + H - 1, :]
    b4[16:T + H, :] = b2[16:T + H, :] + b2[14:T + H - 2, :]
    b8[24:T + H, :] = b4[24:T + H, :] + b4[20:T + H - 4, :]
    s2 = b2[H:T + H, :]
    s4 = b4[H:T + H, :]
    s8 = b8[H:T + H, :]
    s16 = b8[H:T + H, :] + b8[H - 8:T + H - 8, :]
    pbuf[0:H, :] = pbuf[T:T + H, :]

    lane = lax.broadcasted_iota(I32, (T, POOL_WIDTH), 1)
    t1 = (lax.broadcasted_iota(I32, (T, 1), 0) + (i * T + 1)).astype(F32)
    inv2 = 1.0 / jnp.minimum(t1, 2.0)
    inv4 = 1.0 / jnp.minimum(t1, 4.0)
    inv8 = 1.0 / jnp.minimum(t1, 8.0)
    inv16 = 1.0 / jnp.minimum(t1, 16.0)
    gd = POOL_GROUP_DIM
    mean = jnp.where(lane < gd, s2 * inv2,
                     jnp.where(lane < 2 * gd, s4 * inv4,
                               jnp.where(lane < 3 * gd, s8 * inv8, s16 * inv16)))
    pooled = mean - p
    yp = jnp.dot(pooled.astype(BF16), wpool_ref[...], preferred_element_type=F32) * pscale_ref[...]
    yp_ref[...] = yp.astype(yp_ref.dtype)

    ang = invfc_ref[...] * posr_ref[...].astype(F32)
    cos_h = jnp.cos(ang)
    sin_h = jnp.sin(ang)
    hr = QK_ROPE_DIM // 2

    def rope(xt):
        x1, x2 = xt[0:hr], xt[hr:]
        return x1 * cos_h - x2 * sin_h, x2 * cos_h + x1 * sin_h

    k_rope = jnp.concatenate(rope(krT), axis=0).T.astype(BF16)

    cq = u[:, POOL_WIDTH:POOL_WIDTH + Q_LORA_RANK]
    ckv = u[:, POOL_WIDTH + Q_LORA_RANK:POOL_WIDTH + Q_LORA_RANK + KV_LORA_RANK]
    cqn = _rms(cq, qg_ref[...]).astype(BF16)
    ckvn = _rms(ckv, kvg_ref[...]).astype(BF16)
    qaT = lax.dot_general(wuqT_ref[...], cqn, nt, preferred_element_type=F32)
    kn = jnp.dot(ckvn, wuk_ref[...], preferred_element_type=F32)
    vT = lax.dot_general(wuvT_ref[...], ckvn, nt, preferred_element_type=F32)
    nq = MLA_HEADS * QK_NOPE_DIM
    ones_rows = jnp.where(lax.broadcasted_iota(I32, (V_ROWS - V_HEAD_DIM, T), 0) == 0, 1.0, 0.0).astype(BF16)
    for hd in range(MLA_HEADS):
        q1, q2 = rope(qaT[nq + hd * QK_ROPE_DIM:nq + (hd + 1) * QK_ROPE_DIM, :])
        qT_ref[hd, 0:QK_NOPE_DIM, :] = qaT[hd * QK_NOPE_DIM:(hd + 1) * QK_NOPE_DIM, :].astype(BF16)
        qT_ref[hd, QK_NOPE_DIM:QK_NOPE_DIM + hr, :] = q1.astype(BF16)
        qT_ref[hd, QK_NOPE_DIM + hr:QK_HEAD_DIM, :] = q2.astype(BF16)
        k_ref[hd, :, 0:QK_NOPE_DIM] = kn[:, hd * QK_NOPE_DIM:(hd + 1) * QK_NOPE_DIM].astype(BF16)
        k_ref[hd, :, QK_NOPE_DIM:QK_HEAD_DIM] = k_rope
        vT_ref[hd, 0:V_HEAD_DIM, :] = vT[hd * V_HEAD_DIM:(hd + 1) * V_HEAD_DIM, :].astype(BF16)
        vT_ref[hd, V_HEAD_DIM:V_ROWS, :] = ones_rows


def _pre_call(x, posr, mod4, norm_g, w_in_main, w_krT, inv_freq_col, wpool_bd, pool_scale, qg, w_uqT, kvg,
              w_uk, w_uvT, T):
    B, S, D = x.shape
    nH = MLA_HEADS
    const = lambda shape: pl.BlockSpec(shape, lambda b, i: (0,) * len(shape))
    return pl.pallas_call(
        _pre_kernel,
        out_shape=(jax.ShapeDtypeStruct((B, nH, QK_HEAD_DIM, S), BF16),
                   jax.ShapeDtypeStruct((B, nH, S, QK_HEAD_DIM), BF16),
                   jax.ShapeDtypeStruct((B, nH, V_ROWS, S), BF16),
                   jax.ShapeDtypeStruct((B, S, POOL_WIDTH), BF16)),
        grid=(B, S // T),
        in_specs=[pl.BlockSpec((None, T, D), lambda b, i: (b, i, 0)),
                  pl.BlockSpec((None, 1, T), lambda b, i: (b, 0, i)),
                  pl.BlockSpec((None, N_MOD, 1, D), lambda b, i: (b, 0, 0, 0)),
                  const((1, D)),
                  const(w_in_main.shape),
                  const(w_krT.shape),
                  const(inv_freq_col.shape),
                  const(wpool_bd.shape),
                  const((1, POOL_WIDTH)),
                  const((1, Q_LORA_RANK)),
                  const(w_uqT.shape),
                  const((1, KV_LORA_RANK)),
                  const(w_uk.shape),
                  const(w_uvT.shape)],
        out_specs=(pl.BlockSpec((None, nH, QK_HEAD_DIM, T), lambda b, i: (b, 0, 0, i)),
                   pl.BlockSpec((None, nH, T, QK_HEAD_DIM), lambda b, i: (b, 0, i, 0)),
                   pl.BlockSpec((None, nH, V_ROWS, T), lambda b, i: (b, 0, 0, i)),
                   pl.BlockSpec((None, T, POOL_WIDTH), lambda b, i: (b, i, 0))),
        scratch_shapes=[pltpu.VMEM((T + POOL_HALO, POOL_WIDTH), F32)] * 4,
        compiler_params=_cparams("arbitrary", "arbitrary"),
        name="pre",
    )(x, posr, mod4, norm_g, w_in_main, w_krT, inv_freq_col, wpool_bd, pool_scale, qg, w_uqT, kvg, w_uk, w_uvT)


def _attn_kernel(qT_ref, k_ref, vT_ref, o_ref):
    S = k_ref.shape[0]
    tq, tk = min(ATTN_TQ, S), ATTN_TK
    diag = lax.broadcasted_iota(I32, (tk, tk), 0) <= lax.broadcasted_iota(I32, (tk, tk), 1)
    blocks = [(i, j) for i in range(S // tq) for j in range((i + 1) * (tq // tk))]

    def scores(i, j):
        d = max(j * tk - i * tq, 0)
        sT = jnp.dot(k_ref[j * tk:(j + 1) * tk, :], qT_ref[:, i * tq + d:(i + 1) * tq],
                     preferred_element_type=F32)
        if j * tk >= i * tq:
            masked = jnp.where(diag, sT[:, :tk], NEG_BIG)
            sT = masked if sT.shape[1] == tk else jnp.concatenate([masked, sT[:, tk:]], axis=1)
        return sT

    s_next = scores(*blocks[0])
    m = acc = None
    for n, (i, j) in enumerate(blocks):
        sT = s_next
        if n + 1 < len(blocks):
            s_next = scores(*blocks[n + 1])
        if j == 0:
            m = jnp.full((1, tq), NEG_BIG, F32)
            acc = jnp.zeros((V_ROWS, tq), F32)
        d = max(j * tk - i * tq, 0)
        vT = vT_ref[:, j * tk:(j + 1) * tk]
        m_new = jnp.maximum(m[:, d:], jnp.max(sT, axis=0, keepdims=True))
        pT = jnp.exp2(sT - m_new).astype(BF16)
        acc_new = jnp.exp2(m[:, d:] - m_new) * acc[:, d:] + jnp.dot(vT, pT, preferred_element_type=F32)
        if d:
            m_new = jnp.concatenate([m[:, :d], m_new], axis=1)
            acc_new = jnp.concatenate([acc[:, :d], acc_new], axis=1)
        m, acc = m_new, acc_new
        if j == (i + 1) * (tq // tk) - 1:
            out = acc[0:V_HEAD_DIM] * (1.0 / acc[V_HEAD_DIM:V_HEAD_DIM + 1])
            o_ref[i * tq:(i + 1) * tq, :] = out.T.astype(o_ref.dtype)


def _attn_call(qT, k, vT):
    B, nH, S, _ = k.shape
    return pl.pallas_call(
        _attn_kernel,
        out_shape=jax.ShapeDtypeStruct((B, S, nH * V_HEAD_DIM), BF16),
        grid=(B, nH),
        in_specs=[pl.BlockSpec((None, None, QK_HEAD_DIM, S), lambda b, h: (b, h, 0, 0)),
                  pl.BlockSpec((None, None, S, QK_HEAD_DIM), lambda b, h: (b, h, 0, 0)),
                  pl.BlockSpec((None, None, V_ROWS, S), lambda b, h: (b, h, 0, 0))],
        out_specs=pl.BlockSpec((None, S, V_HEAD_DIM), lambda b, h: (b, 0, h)),
        compiler_params=_cparams("arbitrary", "arbitrary"),
        name="attn",
    )(qT, k, vT)


def _route_tile(hb, wgT_ref, bg_ref, wrT_ref, br_ref):
    T = hb.shape[0]
    nt = (((1,), (1,)), ((), ()))
    gl = lax.dot_general(wgT_ref[...], hb, nt, preferred_element_type=F32)
    el = lax.dot_general(wrT_ref[...], hb, nt, preferred_element_type=F32)

    r8 = lax.broadcasted_iota(I32, (SUBLANES, T), 0)
    gvalid = r8 < N_GROUPS
    gmax = jnp.max(jnp.where(gvalid, gl, NEG_BIG), axis=0, keepdims=True)
    gexp = jnp.where(gvalid, jnp.exp(gl - gmax), 0.0)
    g_prob = gexp / jnp.sum(gexp, axis=0, keepdims=True)
    gb = jnp.where(gvalid, gl + bg_ref[...], NEG_BIG)
    gbmax = jnp.max(gb, axis=0, keepdims=True)
    g_sel = jnp.min(jnp.where(gb == gbmax, r8, SUBLANES), axis=0, keepdims=True)
    gp = jnp.sum(jnp.where(r8 == g_sel, g_prob, 0.0), axis=0, keepdims=True)

    e_in = jnp.zeros((EXPERTS_PER_GROUP, T), F32)
    b_in = jnp.zeros((EXPERTS_PER_GROUP, T), F32)
    br = br_ref[...]
    for g in range(N_GROUPS):
        sel = g_sel == g
        e_in = jnp.where(sel, el[g * EXPERTS_PER_GROUP:(g + 1) * EXPERTS_PER_GROUP, :], e_in)
        b_in = jnp.where(sel, br[g * EXPERTS_PER_GROUP:(g + 1) * EXPERTS_PER_GROUP, :], b_in)
    eb = e_in + b_in
    m1 = jnp.max(eb, axis=0, keepdims=True)
    i1 = jnp.min(jnp.where(eb == m1, r8, SUBLANES), axis=0, keepdims=True)
    eb2 = jnp.where(r8 == i1, NEG_BIG, eb)
    m2 = jnp.max(eb2, axis=0, keepdims=True)
    i2 = jnp.min(jnp.where(eb2 == m2, r8, SUBLANES), axis=0, keepdims=True)
    emax = jnp.max(e_in, axis=0, keepdims=True)
    eexp = jnp.exp(e_in - emax)
    sp = eexp / jnp.sum(eexp, axis=0, keepdims=True)
    p1 = jnp.sum(jnp.where(r8 == i1, sp, 0.0), axis=0, keepdims=True)
    p2 = jnp.sum(jnp.where(r8 == i2, sp, 0.0), axis=0, keepdims=True)
    tot = p1 + p2
    w1 = gp * (p1 / tot)
    w2 = gp * (p2 / tot)
    e1 = g_sel * EXPERTS_PER_GROUP + i1
    e2 = g_sel * EXPERTS_PER_GROUP + i2

    r32 = lax.broadcasted_iota(I32, (N_EXPERTS, T), 0)
    oh1 = r32 == e1
    oh2 = r32 == e2
    oh = jnp.where(oh1 | oh2, 1.0, 0.0)
    upper = jnp.where(lax.broadcasted_iota(I32, (T, T), 0) < lax.broadcasted_iota(I32, (T, T), 1),
                      1.0, 0.0).astype(BF16)
    before = jnp.dot(oh.astype(BF16), upper, preferred_element_type=F32)
    cnt = jnp.sum(oh, axis=1, keepdims=True)
    run8 = jnp.floor((cnt + (SUBLANES - 1.0)) * (1.0 / SUBLANES))
    lower = jnp.where(lax.broadcasted_iota(I32, (N_EXPERTS, N_EXPERTS), 1)
                      < lax.broadcasted_iota(I32, (N_EXPERTS, N_EXPERTS), 0), 1.0, 0.0).astype(BF16)
    run_start = jnp.dot(lower, jnp.broadcast_to(run8, (N_EXPERTS, LANES)).astype(BF16),
                        preferred_element_type=F32)[:, 0:1] * float(SUBLANES)
    pos = before + run_start
    row1 = jnp.sum(jnp.where(oh1, pos, 0.0), axis=0, keepdims=True)
    row2 = jnp.sum(jnp.where(oh2, pos, 0.0), axis=0, keepdims=True)
    return row1, row2, w1, w2, cnt


def _post_kernel(x_ref, yp_ref, ya_ref, mod_ref, wo_ref, g_ref,
                 wgT_ref, bg_ref, wrT_ref, br_ref,
                 x1_ref, h2_ref, rows_ref, mf_ref, cnt_ref):
    T = ROW_TILE
    gate_a = mod_ref[2]
    shift_f = mod_ref[3]
    scale_f = mod_ref[4]
    mix = jnp.dot(jnp.concatenate([yp_ref[...], ya_ref[...]], axis=1), wo_ref[...], preferred_element_type=F32)
    x1 = x_ref[...] + gate_a * mix
    x1_ref[...] = x1
    hb = (_rms(x1, g_ref[...]) * (1.0 + scale_f) + shift_f).astype(BF16)
    h2_ref[...] = hb

    for s in range(x_ref.shape[0] // T):
        cols = slice(s * T, (s + 1) * T)
        row1, row2, w1, w2, cnt = _route_tile(hb[cols], wgT_ref, bg_ref, wrT_ref, br_ref)
        cnt_ref[s] = cnt.astype(I32)
        rows_ref[0:1, cols] = row1.astype(I32)
        rows_ref[1:2, cols] = row2.astype(I32)
        zero = jnp.zeros_like(w1)
        mf_ref[cols, :] = jnp.concatenate([row1, row2, w1, w2, zero, zero, zero, zero], axis=0).T


def _post_call(x2, yp2, ya2, mod4, wo, g, wgT, bg, wrT, br, T, steps_per_batch):
    N, D = x2.shape
    const = lambda shape: pl.BlockSpec(shape, lambda i: (0,) * len(shape))
    return pl.pallas_call(
        _post_kernel,
        out_shape=(jax.ShapeDtypeStruct((N, D), F32),
                   jax.ShapeDtypeStruct((N, D), BF16),
                   jax.ShapeDtypeStruct((2, N), I32),
                   jax.ShapeDtypeStruct((N, SUBLANES), F32),
                   jax.ShapeDtypeStruct((N // ROW_TILE, N_EXPERTS, 1), I32)),
        grid=(N // T,),
        in_specs=[pl.BlockSpec((T, D), lambda i: (i, 0)),
                  pl.BlockSpec((T, POOL_WIDTH), lambda i: (i, 0)),
                  pl.BlockSpec((T, MLA_HEADS * V_HEAD_DIM), lambda i: (i, 0)),
                  pl.BlockSpec((None, N_MOD, 1, D), lambda i: (i // steps_per_batch, 0, 0, 0)),
                  const(wo.shape), const((1, D)),
                  const(wgT.shape), const(bg.shape), const(wrT.shape), const(br.shape)],
        out_specs=(pl.BlockSpec((T, D), lambda i: (i, 0)),
                   pl.BlockSpec((T, D), lambda i: (i, 0)),
                   pl.BlockSpec((2, T), lambda i: (0, i)),
                   pl.BlockSpec((T, SUBLANES), lambda i: (i, 0)),
                   pl.BlockSpec((T // ROW_TILE, N_EXPERTS, 1), lambda i: (i, 0, 0))),
        compiler_params=_cparams("arbitrary"),
        name="post",
    )(x2, yp2, ya2, mod4, wo, g, wgT, bg, wrT, br)


def _for_each_strip(tile, src_ref, dst_ref, n_ref, fn):
    first = tile * MAX_STRIPS
    for b in range(STRIP_BITS):
        last = first + n_ref[tile * STRIP_BITS + b]

        def body(r, c, n=SUBLANES << b):
            fn(pl.multiple_of(src_ref[r], SUBLANES), pl.multiple_of(dst_ref[r], SUBLANES), n)
            return c

        lax.fori_loop(first, last, body, 0)
        first = last


def _wait_strip_rows(n_rows, make):
    for b in range((LOCAL_ROWS // SUBLANES).bit_length()):
        n = SUBLANES << b

        @pl.when((n_rows & n) != 0)
        def _():
            make(n).wait()


def _dispatch_kernel(src_ref, dst_ref, n_ref, trows_ref, pends_ref, padlen_ref, paddst_ref, rows_ref, h_ref, xs_ref,
                     lbuf, zbuf, sems, zsem):
    T = h_ref.shape[0]
    RB = zbuf.shape[0]
    half = h_ref.shape[1] // 2
    step = pl.program_id(0)
    n_steps = pl.num_programs(0)
    slot = step % DISPATCH_SLOTS

    def zero_fill(act, part, n_parts):
        def per_expert(e, c):
            pad = padlen_ref[e]
            dst = paddst_ref[e]
            for b in reversed(range((RB // SUBLANES).bit_length() - 1)):
                n = SUBLANES << b
                done = pad & ~(2 * n - 1)

                @pl.when((pad & n) != 0)
                def _():
                    act(pltpu.make_async_copy(zbuf.at[pl.ds(0, n)],
                                              xs_ref.at[pl.ds(pl.multiple_of(dst + done, SUBLANES), n)], zsem))
            return c

        def per_block(j, c):
            b = pends_ref[N_EXPERTS - 1] // RB + part + j * n_parts
            act(pltpu.make_async_copy(zbuf, xs_ref.at[pl.ds(pl.multiple_of(b * RB, RB), RB)], zsem))
            return c

        n_tail = xs_ref.shape[0] // RB - pends_ref[N_EXPERTS - 1] // RB
        lax.fori_loop(0, (N_EXPERTS - part + n_parts - 1) // n_parts,
                      lambda j, c: per_expert(part + j * n_parts, c), 0)
        lax.fori_loop(0, (n_tail - part + n_parts - 1) // n_parts, per_block, 0)

    @pl.when(step == 0)
    def _():
        zbuf[...] = jnp.zeros(zbuf.shape, zbuf.dtype)

    zero_fill(lambda cp: cp.start(), step, n_steps)

    r = lax.broadcasted_iota(I32, (LOCAL_ROWS, T), 0)
    perm = jnp.where((r == rows_ref[0:1, :]) | (r == rows_ref[1:2, :]), 1.0, 0.0).astype(BF16)
    h = h_ref[...]
    lo = jnp.dot(perm, h[:, :half], preferred_element_type=F32)
    hi = jnp.dot(perm, h[:, half:], preferred_element_type=F32)
    lbuf[slot] = _pack_bf16_pair(lo, hi)

    def strip(tile_slot):
        def make(loc, dst, n):
            return pltpu.make_async_copy(lbuf.at[tile_slot, pl.ds(loc, n)], xs_ref.at[pl.ds(dst, n)],
                                         sems.at[tile_slot])
        return make

    def wait_tile(tile):
        _wait_strip_rows(trows_ref[tile], lambda n: strip(tile % DISPATCH_SLOTS)(0, 0, n))

    _for_each_strip(step, src_ref, dst_ref, n_ref, lambda *a: strip(slot)(*a).start())

    @pl.when(step >= DISPATCH_SLOTS - 1)
    def _():
        wait_tile(step - (DISPATCH_SLOTS - 1))

    @pl.when(step == n_steps - 1)
    def _():
        for back in reversed(range(DISPATCH_SLOTS - 1)):
            @pl.when(step >= back)
            def _():
                wait_tile(step - back)
        zero_fill(lambda cp: cp.wait(), 0, 1)


def _dispatch_call(strip_src, strip_dst, strip_n, tile_rows, pends, pad_len, pad_dst, rows, h2, P_pad, T, RB):
    N, D = h2.shape
    return pl.pallas_call(
        _dispatch_kernel,
        out_shape=jax.ShapeDtypeStruct((P_pad, D // 2), U32),
        grid_spec=pltpu.PrefetchScalarGridSpec(
            num_scalar_prefetch=7,
            grid=(N // T,),
            in_specs=[pl.BlockSpec((2, T), lambda i, *_: (0, i)),
                      pl.BlockSpec((T, D), lambda i, *_: (i, 0))],
            out_specs=pl.BlockSpec(memory_space=pl.ANY),
            scratch_shapes=[pltpu.VMEM((DISPATCH_SLOTS, LOCAL_ROWS, D // 2), U32),
                            pltpu.VMEM((RB, D // 2), U32),
                            pltpu.SemaphoreType.DMA((DISPATCH_SLOTS,)),
                            pltpu.SemaphoreType.DMA]),
        compiler_params=_cparams("arbitrary"),
        name="dispatch",
    )(strip_src, strip_dst, strip_n, tile_rows, pends, pad_len, pad_dst, rows, h2)


def _expert_kernel(seg_ref, sege_ref, nseg_ref, nused_ref, xs_hbm, wgu_hbm, wd_hbm, ys_hbm,
                   xbuf, ybuf, zbuf, wgu_f32, wd_f32, wgu_bf, wd_bf, xsem, ysem, wsem, zsem):
    RB = xbuf.shape[1]
    n_x = xbuf.shape[0]
    half = wgu_bf.shape[0] // 2
    n_used = nused_ref[0]
    n_blocks = ys_hbm.shape[0] // RB

    def rows(b):
        return pl.ds(pl.multiple_of(b * RB, RB), RB)

    def x_copy(b):
        return pltpu.make_async_copy(xs_hbm.at[rows(b)], xbuf.at[b % n_x], xsem.at[b % n_x])

    def y_copy(b):
        return pltpu.make_async_copy(ybuf.at[b % 2], ys_hbm.at[rows(b)], ysem.at[b % 2])

    def tail_copy(b):
        return pltpu.make_async_copy(zbuf, ys_hbm.at[rows(b)], zsem)

    def weight_copies(kk):
        slot = kk % 2
        e = sege_ref[kk]
        return (pltpu.make_async_copy(wgu_hbm.at[e], wgu_f32.at[slot], wsem.at[0, slot]),
                pltpu.make_async_copy(wd_hbm.at[e], wd_f32.at[slot], wsem.at[1, slot]))

    zbuf[...] = jnp.zeros(zbuf.shape, zbuf.dtype)
    for cp in weight_copies(0):
        cp.start()
    for b in range(n_x - 1):
        @pl.when(b < n_used)
        def _():
            x_copy(b).start()

    def body(i, c):
        k = seg_ref[i]

        @pl.when((i == 0) | (k != seg_ref[jnp.maximum(i - 1, 0)]))
        def _():
            for cp in weight_copies(k):
                cp.wait()

            @pl.when(k + 1 < nseg_ref[0])
            def _():
                for cp in weight_copies(k + 1):
                    cp.start()

            wgu_bf[...] = wgu_f32[k % 2].astype(BF16)
            wd_bf[...] = wd_f32[k % 2].astype(BF16)

        @pl.when(i + n_x - 1 < n_used)
        def _():
            x_copy(i + n_x - 1).start()

        @pl.when(n_used + i < n_blocks)
        def _():
            tail_copy(n_used + i).start()

        x_copy(i).wait()
        lo, hi = _unpack_bf16_pair(xbuf[i % n_x])
        gu = (jnp.dot(lo, wgu_bf[0:half, :], preferred_element_type=F32)
              + jnp.dot(hi, wgu_bf[half:, :], preferred_element_type=F32))
        gate = gu[:, :D_EXPERT]
        up = gu[:, D_EXPERT:]
        act = gate / (1.0 + jnp.exp(-gate)) * up
        y = jnp.dot(act.astype(BF16), wd_bf[...], preferred_element_type=F32)
        yb = y.astype(BF16).astype(F32)

        @pl.when(i >= 2)
        def _():
            y_copy(i - 2).wait()

        ybuf[i % 2] = _pack_bf16_pair(yb[:, :half], yb[:, half:])
        y_copy(i).start()
        return c

    lax.fori_loop(0, n_used, body, 0)

    for back in (2, 1):
        @pl.when(n_used >= back)
        def _():
            y_copy(n_used - back).wait()

    lax.fori_loop(2 * n_used, n_blocks, lambda b, c: (tail_copy(b).start(), c)[1], 0)
    lax.fori_loop(n_used, n_blocks, lambda b, c: (tail_copy(b).wait(), c)[1], 0)


def _expert_call(block_seg, seg_expert, n_seg, n_used, xs, wgu, wd, RB):
    P, Dh = xs.shape
    D = 2 * Dh
    any_spec = pl.BlockSpec(memory_space=pl.ANY)
    return pl.pallas_call(
        _expert_kernel,
        out_shape=jax.ShapeDtypeStruct((P, Dh), U32),
        grid_spec=pltpu.PrefetchScalarGridSpec(
            num_scalar_prefetch=4,
            grid=(1,),
            in_specs=[any_spec, any_spec, any_spec],
            out_specs=any_spec,
            scratch_shapes=[pltpu.VMEM((EXPERT_X_SLOTS, RB, Dh), U32), pltpu.VMEM((2, RB, Dh), U32),
                            pltpu.VMEM((RB, Dh), U32),
                            pltpu.VMEM((2, D, 2 * D_EXPERT), F32), pltpu.VMEM((2, D_EXPERT, D), F32),
                            pltpu.VMEM((D, 2 * D_EXPERT), BF16), pltpu.VMEM((D_EXPERT, D), BF16),
                            pltpu.SemaphoreType.DMA((EXPERT_X_SLOTS,)), pltpu.SemaphoreType.DMA((2,)),
                            pltpu.SemaphoreType.DMA((2, 2)), pltpu.SemaphoreType.DMA]),
        compiler_params=_cparams("arbitrary"),
        name="experts",
    )(block_seg, seg_expert, n_seg, n_used, xs, wgu, wd)


def _combine_kernel(src_ref, dst_ref, n_ref, trows_ref, x1_ref, tok_ref, mod_ref, g_ref, ys_ref, o_ref, ybuf, sems):
    T = x1_ref.shape[0]
    step = pl.program_id(0)
    n_steps = pl.num_programs(0)
    slot = step % 2

    def strip(tile_slot):
        def make(loc, dst, n):
            return pltpu.make_async_copy(ys_ref.at[pl.ds(dst, n)], ybuf.at[tile_slot, pl.ds(loc, n)],
                                         sems.at[tile_slot])
        return make

    @pl.when(step == 0)
    def _():
        ybuf[...] = jnp.zeros(ybuf.shape, ybuf.dtype)
        _for_each_strip(step, src_ref, dst_ref, n_ref, lambda *a: strip(slot)(*a).start())

    @pl.when(step + 1 < n_steps)
    def _():
        _for_each_strip(step + 1, src_ref, dst_ref, n_ref, lambda *a: strip(1 - slot)(*a).start())

    _wait_strip_rows(trows_ref[step], lambda n: strip(slot)(0, 0, n))

    tok = tok_ref[...]
    lo, hi = _unpack_bf16_pair(ybuf[slot])
    n_chunks = 4
    tc = T // n_chunks

    def permw_chunk(ci):
        t = tok[ci * tc:(ci + 1) * tc]
        rows = t[:, 0:2].astype(I32)
        c = lax.broadcasted_iota(I32, (tc, LOCAL_ROWS), 1)
        return jnp.where(c == rows[:, 0:1], t[:, 2:3], jnp.where(c == rows[:, 1:2], t[:, 3:4], 0.0)).astype(BF16)

    p_next = permw_chunk(0)
    parts = []
    for ci in range(n_chunks):
        permw = p_next
        if ci + 1 < n_chunks:
            p_next = permw_chunk(ci + 1)
        parts.append(jnp.concatenate([jnp.dot(permw, lo, preferred_element_type=F32),
                                      jnp.dot(permw, hi, preferred_element_type=F32)], axis=1))
    moe = jnp.concatenate(parts, axis=0)
    gate_f = mod_ref[5]
    o_ref[...] = _rms(x1_ref[...] + gate_f * moe, g_ref[...])


def _combine_call(strip_src, strip_dst, strip_n, tile_rows, x1, tok, mod4, final_g, ys, T, tiles_per_batch):
    N, D = x1.shape
    return pl.pallas_call(
        _combine_kernel,
        out_shape=jax.ShapeDtypeStruct((N, D), F32),
        grid_spec=pltpu.PrefetchScalarGridSpec(
            num_scalar_prefetch=4,
            grid=(N // T,),
            in_specs=[pl.BlockSpec((T, D), lambda i, *_: (i, 0)),
                      pl.BlockSpec((T, SUBLANES), lambda i, *_: (i, 0)),
                      pl.BlockSpec((None, N_MOD, 1, D), lambda i, *_: (i // tiles_per_batch, 0, 0, 0)),
                      pl.BlockSpec((1, D), lambda i, *_: (0, 0)),
                      pl.BlockSpec(memory_space=pl.ANY)],
            out_specs=pl.BlockSpec((T, D), lambda i, *_: (i, 0)),
            scratch_shapes=[pltpu.VMEM((2, LOCAL_ROWS, D // 2), U32), pltpu.SemaphoreType.DMA((2,))]),
        compiler_params=_cparams("arbitrary"),
        name="combine",
    )(strip_src, strip_dst, strip_n, tile_rows, x1, tok, mod4, final_g, ys)


def _round_up(v, m):
    return (v + m - 1) // m * m


def kernel(x, c, positions, w_mod, b_mod, norm_mix_g, w_in, w_pool, pool_scale, q_norm_g, w_uq, kv_norm_g, w_ukv, w_o, norm_ffn_g, w_group, b_group, w_router, b_router, w_gate_up, w_down, final_g):
    B, S, D = x.shape
    N = B * S
    depth = w_mod.shape[0]
    T = ROW_TILE
    RB = EXPERT_ROWS
    assert depth == 1, "the final RMSNorm is fused into the layer's combine step"
    assert S % T == 0 and S % min(POST_TILE, S) == 0 and min(POST_TILE, S) % T == 0 and S % min(PRE_TILE, S) == 0 and S % min(ATTN_TQ, S) == 0 and min(ATTN_TQ, S) % ATTN_TK == 0
    tiles_per_batch = S // T
    post_tile = min(POST_TILE, S)
    n_tiles = N // T
    nH = MLA_HEADS
    l = 0

    inv_freq = ROPE_THETA ** (-(jnp.arange(0, QK_ROPE_DIM, 2, dtype=F32) / QK_ROPE_DIM))
    posr = positions.reshape(B, 1, S)
    cut3 = POOL_WIDTH + Q_LORA_RANK + KV_LORA_RANK

    mod4 = _mod_call(c, w_mod[l], b_mod[l]).reshape(B, N_MOD, 1, D)

    wi = w_in[l]
    w_in_main = wi[:, :cut3].astype(BF16)
    w_krT = wi[:, cut3:].T.astype(BF16)
    wq = w_uq[l].reshape(Q_LORA_RANK, nH, QK_HEAD_DIM)
    wq_n = wq[:, :, :QK_NOPE_DIM].reshape(Q_LORA_RANK, nH * QK_NOPE_DIM)
    wq_r = wq[:, :, QK_NOPE_DIM:]
    w_uqT = jnp.concatenate([wq_n, wq_r.reshape(Q_LORA_RANK, nH * QK_ROPE_DIM)], axis=1).T.astype(BF16)
    wkv = w_ukv[l].reshape(KV_LORA_RANK, nH, QK_NOPE_DIM + V_HEAD_DIM)
    w_uk = wkv[:, :, :QK_NOPE_DIM].reshape(KV_LORA_RANK, nH * QK_NOPE_DIM).astype(BF16)
    w_uvT = wkv[:, :, QK_NOPE_DIM:].reshape(KV_LORA_RANK, nH * V_HEAD_DIM).T.astype(BF16)
    qg = (q_norm_g[l] * (QK_HEAD_DIM ** -0.5 * LOG2_E)).reshape(1, Q_LORA_RANK)
    wpool_bd = jnp.zeros((POOL_WIDTH, POOL_WIDTH), F32)
    for g in range(len(POOL_WINDOWS)):
        sl = slice(g * POOL_GROUP_DIM, (g + 1) * POOL_GROUP_DIM)
        wpool_bd = wpool_bd.at[sl, sl].set(w_pool[l, g])
    wpool_bd = wpool_bd.astype(BF16)

    qT, k, vT, yp = _pre_call(
        x, posr, mod4, norm_mix_g[l].reshape(1, D), w_in_main, w_krT, inv_freq.reshape(QK_ROPE_DIM // 2, 1),
        wpool_bd, pool_scale[l].reshape(1, POOL_WIDTH), qg, w_uqT, kv_norm_g[l].reshape(1, KV_LORA_RANK),
        w_uk, w_uvT, min(PRE_TILE, S))
    ya = _attn_call(qT, k, vT)

    wo = w_o[l].astype(BF16)
    wgT = jnp.zeros((SUBLANES, D), F32).at[:N_GROUPS].set(w_group[l].T).astype(BF16)
    bg = jnp.zeros((SUBLANES, 1), F32).at[:N_GROUPS, 0].set(b_group[l])
    x1, h2, rows, meta_f, tile_cnt = _post_call(
        x.reshape(N, D), yp.reshape(N, POOL_WIDTH), ya.reshape(N, nH * V_HEAD_DIM), mod4,
        wo, norm_ffn_g[l].reshape(1, D),
        wgT, bg, w_router[l].T.astype(BF16), b_router[l].reshape(N_EXPERTS, 1), post_tile, S // post_tile)

    run_len = _round_up(tile_cnt[:, :, 0], SUBLANES)
    run_loc = jnp.cumsum(run_len, axis=1) - run_len
    seg_rows = jnp.sum(run_len, axis=0)
    seg_len = _round_up(seg_rows, RB)
    pends = jnp.cumsum(seg_len).astype(I32)
    pstarts = pends - seg_len
    run_dst = pstarts[None, :] + jnp.cumsum(run_len, axis=0) - run_len
    pad_len = (seg_len - seg_rows).astype(I32)
    pad_dst = (pstarts + seg_rows).astype(I32)
    bits = jnp.arange(STRIP_BITS, dtype=I32)
    size = SUBLANES << bits
    valid = (run_len[:, None, :] & size[None, :, None]) != 0
    done = run_len[:, None, :] & ~(2 * size[None, :, None] - 1)
    flat = lambda a: a.reshape(n_tiles, STRIP_BITS * N_EXPERTS)
    valid_f = flat(valid)
    pos = jnp.cumsum(valid_f.astype(I32), axis=1) - valid_f.astype(I32)
    pick = valid_f[:, None, :] & (pos[:, None, :] == jnp.arange(MAX_STRIPS, dtype=I32)[None, :, None])
    gather = lambda a: jnp.sum(jnp.where(pick, flat(a)[:, None, :], 0), axis=-1).reshape(-1).astype(I32)
    strip_src = gather(run_loc[:, None, :] + done)
    strip_dst = gather(run_dst[:, None, :] + done)
    strip_n = jnp.sum(valid, axis=2).reshape(-1).astype(I32)
    tile_rows = jnp.sum(run_len, axis=1).astype(I32)
    P_pad = _round_up(2 * N + n_tiles * N_EXPERTS * (SUBLANES - 1), RB) + N_EXPERTS * RB
    n_rb = P_pad // RB
    n_used = (pends[-1:] // RB).astype(I32)
    block_start = jnp.minimum(jnp.arange(n_rb, dtype=I32), n_used - 1) * RB
    block_e = jnp.sum((pends[None, :] <= block_start[:, None]).astype(I32), axis=1)
    nonempty = seg_len > 0
    seg_index = jnp.cumsum(nonempty.astype(I32)) - 1
    n_seg = jnp.sum(nonempty.astype(I32)).reshape(1)
    experts = jnp.arange(N_EXPERTS, dtype=I32)
    block_seg = jnp.sum(jnp.where(block_e[:, None] == experts[None, :], seg_index[None, :], 0), axis=1).astype(I32)
    seg_expert = jnp.sum(jnp.where(nonempty[None, :] & (seg_index[None, :] == experts[:, None]), experts[None, :], 0),
                         axis=1).astype(I32)

    xs = _dispatch_call(strip_src, strip_dst, strip_n, tile_rows, pends, pad_len, pad_dst, rows, h2, P_pad, T, RB)
    ys = _expert_call(block_seg, seg_expert, n_seg, n_used, xs, w_gate_up[l], w_down[l], RB)
    out = _combine_call(strip_src, strip_dst, strip_n, tile_rows, x1, meta_f, mod4, final_g.reshape(1, D), ys,
                        T, tiles_per_batch)
    return out.reshape(B, S, D)
```

```python
import jax
import jax.numpy as jnp
from jax import lax
from jax.experimental import pallas as pl
from jax.experimental.pallas import tpu as pltpu

F32 = jnp.float32
BF16 = jnp.bfloat16
U32 = jnp.uint32
I32 = jnp.int32

POOL_WINDOWS = (2, 4, 8, 16)
POOL_GROUP_DIM = 64
POOL_WIDTH = 256
MLA_HEADS = 6
QK_NOPE_DIM = 128
QK_ROPE_DIM = 64
QK_HEAD_DIM = QK_NOPE_DIM + QK_ROPE_DIM
V_HEAD_DIM = 128
V_ROWS = 144
Q_LORA_RANK = 512
KV_LORA_RANK = 256
ROPE_THETA = 10000.0
N_GROUPS = 4
EXPERTS_PER_GROUP = 8
N_EXPERTS = N_GROUPS * EXPERTS_PER_GROUP
D_EXPERT = 256
N_MOD = 6
EPS = 1e-6

SUBLANES = 8
LANES = 128
assert EXPERTS_PER_GROUP == SUBLANES and N_GROUPS <= SUBLANES
assert POOL_WINDOWS == (2, 4, 8, 16) and POOL_GROUP_DIM * len(POOL_WINDOWS) == POOL_WIDTH
MOD_COLS = 512
POOL_HALO = 32
ROW_TILE = 512
PRE_TILE = 1024
POST_TILE = 1024
ATTN_TQ = 2048
ATTN_TK = 512
EXPERT_ROWS = 1024
EXPERT_X_SLOTS = 3
LOCAL_ROWS = 2 * ROW_TILE + 256
assert LOCAL_ROWS >= 2 * ROW_TILE + N_EXPERTS * (SUBLANES - 1)
STRIP_BITS = (2 * ROW_TILE // SUBLANES).bit_length()
MAX_STRIPS = 128
assert MAX_STRIPS >= 2 * N_EXPERTS + (LOCAL_ROWS // SUBLANES - 3 * N_EXPERTS) // 4
DISPATCH_SLOTS = 2
VMEM_LIMIT = 56 * 1024 * 1024
NEG_BIG = -1e30
LOG2_E = 1.4426950408889634
HI_MASK = 0xFFFF0000


def _cparams(*sem):
    return pltpu.CompilerParams(dimension_semantics=sem, vmem_limit_bytes=VMEM_LIMIT)


def _rms(x, g):
    return x * lax.rsqrt(jnp.mean(x * x, axis=-1, keepdims=True) + EPS) * g


def _pack_bf16_pair(lo, hi):
    return lax.bitcast_convert_type(hi, U32) | (lax.bitcast_convert_type(lo, U32) >> 16)


def _unpack_bf16_pair(w):
    lo = lax.bitcast_convert_type(w << 16, F32).astype(BF16)
    hi = lax.bitcast_convert_type(w & jnp.uint32(HI_MASK), F32).astype(BF16)
    return lo, hi


def _mod_kernel(c_ref, w_ref, b_ref, o_ref):
    c = c_ref[...]
    ca = c / (1.0 + jnp.exp(-c))
    o_ref[...] = jnp.dot(ca.astype(BF16), w_ref[...].astype(BF16), preferred_element_type=F32) + b_ref[...]


def _mod_call(c, w_mod, b_mod):
    B, D = c.shape
    n_out = w_mod.shape[1]
    tn = MOD_COLS
    return pl.pallas_call(
        _mod_kernel,
        out_shape=jax.ShapeDtypeStruct((B, n_out), F32),
        grid=(n_out // tn,),
        in_specs=[pl.BlockSpec((B, D), lambda j: (0, 0)),
                  pl.BlockSpec((D, tn), lambda j: (0, j)),
                  pl.BlockSpec((1, tn), lambda j: (0, j))],
        out_specs=pl.BlockSpec((B, tn), lambda j: (0, j)),
        compiler_params=_cparams("arbitrary"),
        name="mod",
    )(c, w_mod, b_mod.reshape(1, n_out))


def _pre_kernel(x_ref, posr_ref, mod_ref, g_ref, win_ref, wkrT_ref, invfc_ref, wpool_ref, pscale_ref,
                qg_ref, wuqT_ref, kvg_ref, wuk_ref, wuvT_ref,
                qT_ref, k_ref, vT_ref, yp_ref,
                pbuf, b2, b4, b8):
    T = x_ref.shape[0]
    H = POOL_HALO
    i = pl.program_id(1)
    nt = (((1,), (1,)), ((), ()))

    shift = mod_ref[0]
    scale = mod_ref[1]
    hb = (_rms(x_ref[...], g_ref[...]) * (1.0 + scale) + shift).astype(BF16)
    u = jnp.dot(hb, win_ref[...], preferred_element_type=F32)
    krT = lax.dot_general(wkrT_ref[...], hb, nt, preferred_element_type=F32)

    p = u[:, :POOL_WIDTH]

    @pl.when(i == 0)
    def _():
        pbuf[0:H, :] = jnp.zeros((H, POOL_WIDTH), F32)

    pbuf[H:H + T, :] = p
    b2[8:T + H, :] = pbuf[8:T + H, :] + pbuf[7:T + H - 1, :]
    b4[16:T + H, :] = b2[16:T + H, :] + b2[14:T + H - 2, :]
    b8[24:T + H, :] = b4[24:T + H, :] + b4[20:T + H - 4, :]
    s2 = b2[H:T + H, :]
    s4 = b4[H:T + H, :]
    s8 = b8[H:T + H, :]
    s16 = b8[H:T + H, :] + b8[H - 8:T + H - 8, :]
    pbuf[0:H, :] = pbuf[T:T + H, :]

    lane = lax.broadcasted_iota(I32, (T, POOL_WIDTH), 1)
    t1 = (lax.broadcasted_iota(I32, (T, 1), 0) + (i * T + 1)).astype(F32)
    inv2 = 1.0 / jnp.minimum(t1, 2.0)
    inv4 = 1.0 / jnp.minimum(t1, 4.0)
    inv8 = 1.0 / jnp.minimum(t1, 8.0)
    inv16 = 1.0 / jnp.minimum(t1, 16.0)
    gd = POOL_GROUP_DIM
    mean = jnp.where(lane < gd, s2 * inv2,
                     jnp.where(lane < 2 * gd, s4 * inv4,
                               jnp.where(lane < 3 * gd, s8 * inv8, s16 * inv16)))
    pooled = mean - p
    yp = jnp.dot(pooled.astype(BF16), wpool_ref[...], preferred_element_type=F32) * pscale_ref[...]
    yp_ref[...] = yp.astype(yp_ref.dtype)

    ang = invfc_ref[...] * posr_ref[...].astype(F32)
    cos_h = jnp.cos(ang)
    sin_h = jnp.sin(ang)
    hr = QK_ROPE_DIM // 2

    def rope(xt):
        x1, x2 = xt[0:hr], xt[hr:]
        return x1 * cos_h - x2 * sin_h, x2 * cos_h + x1 * sin_h

    k_rope = jnp.concatenate(rope(krT), axis=0).T.astype(BF16)

    cq = u[:, POOL_WIDTH:POOL_WIDTH + Q_LORA_RANK]
    ckv = u[:, POOL_WIDTH + Q_LORA_RANK:POOL_WIDTH + Q_LORA_RANK + KV_LORA_RANK]
    cqn = _rms(cq, qg_ref[...]).astype(BF16)
    ckvn = _rms(ckv, kvg_ref[...]).astype(BF16)
    qaT = lax.dot_general(wuqT_ref[...], cqn, nt, preferred_element_type=F32)
    kn = jnp.dot(ckvn, wuk_ref[...], preferred_element_type=F32)
    vT = lax.dot_general(wuvT_ref[...], ckvn, nt, preferred_element_type=F32)
    nq = MLA_HEADS * QK_NOPE_DIM
    ones_rows = jnp.where(lax.broadcasted_iota(I32, (V_ROWS - V_HEAD_DIM, T), 0) == 0, 1.0, 0.0).astype(BF16)
    for hd in range(MLA_HEADS):
        q1, q2 = rope(qaT[nq + hd * QK_ROPE_DIM:nq + (hd + 1) * QK_ROPE_DIM, :])
        qT_ref[hd, 0:QK_NOPE_DIM, :] = qaT[hd * QK_NOPE_DIM:(hd + 1) * QK_NOPE_DIM, :].astype(BF16)
        qT_ref[hd, QK_NOPE_DIM:QK_NOPE_DIM + hr, :] = q1.astype(BF16)
        qT_ref[hd, QK_NOPE_DIM + hr:QK_HEAD_DIM, :] = q2.astype(BF16)
        k_ref[hd, :, 0:QK_NOPE_DIM] = kn[:, hd * QK_NOPE_DIM:(hd + 1) * QK_NOPE_DIM].astype(BF16)
        k_ref[hd, :, QK_NOPE_DIM:QK_HEAD_DIM] = k_rope
        vT_ref[hd, 0:V_HEAD_DIM, :] = vT[hd * V_HEAD_DIM:(hd + 1) * V_HEAD_DIM, :].astype(BF16)
        vT_ref[hd, V_HEAD_DIM:V_ROWS, :] = ones_rows


def _pre_call(x, posr, mod4, norm_g, w_in_main, w_krT, inv_freq_col, wpool_bd, pool_scale, qg, w_uqT, kvg,
              w_uk, w_uvT, T):
    B, S, D = x.shape
    nH = MLA_HEADS
    const = lambda shape: pl.BlockSpec(shape, lambda b, i: (0,) * len(shape))
    return pl.pallas_call(
        _pre_kernel,
        out_shape=(jax.ShapeDtypeStruct((B, nH, QK_HEAD_DIM, S), BF16),
                   jax.ShapeDtypeStruct((B, nH, S, QK_HEAD_DIM), BF16),
                   jax.ShapeDtypeStruct((B, nH, V_ROWS, S), BF16),
                   jax.ShapeDtypeStruct((B, S, POOL_WIDTH), BF16)),
        grid=(B, S // T),
        in_specs=[pl.BlockSpec((None, T, D), lambda b, i: (b, i, 0)),
                  pl.BlockSpec((None, 1, T), lambda b, i: (b, 0, i)),
                  pl.BlockSpec((None, N_MOD, 1, D), lambda b, i: (b, 0, 0, 0)),
                  const((1, D)),
                  const(w_in_main.shape),
                  const(w_krT.shape),
                  const(inv_freq_col.shape),
                  const(wpool_bd.shape),
                  const((1, POOL_WIDTH)),
                  const((1, Q_LORA_RANK)),
                  const(w_uqT.shape),
                  const((1, KV_LORA_RANK)),
                  const(w_uk.shape),
                  const(w_uvT.shape)],
        out_specs=(pl.BlockSpec((None, nH, QK_HEAD_DIM, T), lambda b, i: (b, 0, 0, i)),
                   pl.BlockSpec((None, nH, T, QK_HEAD_DIM), lambda b, i: (b, 0, i, 0)),
                   pl.BlockSpec((None, nH, V_ROWS, T), lambda b, i: (b, 0, 0, i)),
                   pl.BlockSpec((None, T, POOL_WIDTH), lambda b, i: (b, i, 0))),
        scratch_shapes=[pltpu.VMEM((T + POOL_HALO, POOL_WIDTH), F32)] * 4,
        compiler_params=_cparams("arbitrary", "arbitrary"),
        name="pre",
    )(x, posr, mod4, norm_g, w_in_main, w_krT, inv_freq_col, wpool_bd, pool_scale, qg, w_uqT, kvg, w_uk, w_uvT)


def _attn_kernel(qT_ref, k_ref, vT_ref, o_ref):
    S = k_ref.shape[0]
    tq, tk = min(ATTN_TQ, S), ATTN_TK
    diag = lax.broadcasted_iota(I32, (tk, tk), 0) <= lax.broadcasted_iota(I32, (tk, tk), 1)
    blocks = [(i, j) for i in range(S // tq) for j in range((i + 1) * (tq // tk))]

    def scores(i, j):
        d = max(j * tk - i * tq, 0)
        sT = jnp.dot(k_ref[j * tk:(j + 1) * tk, :], qT_ref[:, i * tq + d:(i + 1) * tq],
                     preferred_element_type=F32)
        if j * tk >= i * tq:
            masked = jnp.where(diag, sT[:, :tk], NEG_BIG)
            sT = masked if sT.shape[1] == tk else jnp.concatenate([masked, sT[:, tk:]], axis=1)
        return sT

    s_next = scores(*blocks[0])
    m = acc = None
    for n, (i, j) in enumerate(blocks):
        sT = s_next
        if n + 1 < len(blocks):
            s_next = scores(*blocks[n + 1])
        if j == 0:
            m = jnp.full((1, tq), NEG_BIG, F32)
            acc = jnp.zeros((V_ROWS, tq), F32)
        d = max(j * tk - i * tq, 0)
        vT = vT_ref[:, j * tk:(j + 1) * tk]
        m_new = jnp.maximum(m[:, d:], jnp.max(sT, axis=0, keepdims=True))
        pT = jnp.exp2(sT - m_new).astype(BF16)
        acc_new = jnp.exp2(m[:, d:] - m_new) * acc[:, d:] + jnp.dot(vT, pT, preferred_element_type=F32)
        if d:
            m_new = jnp.concatenate([m[:, :d], m_new], axis=1)
            acc_new = jnp.concatenate([acc[:, :d], acc_new], axis=1)
        m, acc = m_new, acc_new
        if j == (i + 1) * (tq // tk) - 1:
            out = acc[0:V_HEAD_DIM] * (1.0 / acc[V_HEAD_DIM:V_HEAD_DIM + 1])
            o_ref[i * tq:(i + 1) * tq, :] = out.T.astype(o_ref.dtype)


def _attn_call(qT, k, vT):
    B, nH, S, _ = k.shape
    return pl.pallas_call(
        _attn_kernel,
        out_shape=jax.ShapeDtypeStruct((B, S, nH * V_HEAD_DIM), BF16),
        grid=(B, nH),
        in_specs=[pl.BlockSpec((None, None, QK_HEAD_DIM, S), lambda b, h: (b, h, 0, 0)),
                  pl.BlockSpec((None, None, S, QK_HEAD_DIM), lambda b, h: (b, h, 0, 0)),
                  pl.BlockSpec((None, None, V_ROWS, S), lambda b, h: (b, h, 0, 0))],
        out_specs=pl.BlockSpec((None, S, V_HEAD_DIM), lambda b, h: (b, 0, h)),
        compiler_params=_cparams("arbitrary", "arbitrary"),
        name="attn",
    )(qT, k, vT)


def _route_tile(hb, wgT_ref, bg_ref, wrT_ref, br_ref):
    T = hb.shape[0]
    nt = (((1,), (1,)), ((), ()))
    gl = lax.dot_general(wgT_ref[...], hb, nt, preferred_element_type=F32)
    el = lax.dot_general(wrT_ref[...], hb, nt, preferred_element_type=F32)

    r8 = lax.broadcasted_iota(I32, (SUBLANES, T), 0)
    gvalid = r8 < N_GROUPS
    gmax = jnp.max(jnp.where(gvalid, gl, NEG_BIG), axis=0, keepdims=True)
    gexp = jnp.where(gvalid, jnp.exp(gl - gmax), 0.0)
    g_prob = gexp / jnp.sum(gexp, axis=0, keepdims=True)
    gb = jnp.where(gvalid, gl + bg_ref[...], NEG_BIG)
    gbmax = jnp.max(gb, axis=0, keepdims=True)
    g_sel = jnp.min(jnp.where(gb == gbmax, r8, SUBLANES), axis=0, keepdims=True)
    gp = jnp.sum(jnp.where(r8 == g_sel, g_prob, 0.0), axis=0, keepdims=True)

    e_in = jnp.zeros((EXPERTS_PER_GROUP, T), F32)
    b_in = jnp.zeros((EXPERTS_PER_GROUP, T), F32)
    br = br_ref[...]
    for g in range(N_GROUPS):
        sel = g_sel == g
        e_in = jnp.where(sel, el[g * EXPERTS_PER_GROUP:(g + 1) * EXPERTS_PER_GROUP, :], e_in)
        b_in = jnp.where(sel, br[g * EXPERTS_PER_GROUP:(g + 1) * EXPERTS_PER_GROUP, :], b_in)
    eb = e_in + b_in
    m1 = jnp.max(eb, axis=0, keepdims=True)
    i1 = jnp.min(jnp.where(eb == m1, r8, SUBLANES), axis=0, keepdims=True)
    eb2 = jnp.where(r8 == i1, NEG_BIG, eb)
    m2 = jnp.max(eb2, axis=0, keepdims=True)
    i2 = jnp.min(jnp.where(eb2 == m2, r8, SUBLANES), axis=0, keepdims=True)
    emax = jnp.max(e_in, axis=0, keepdims=True)
    eexp = jnp.exp(e_in - emax)
    sp = eexp / jnp.sum(eexp, axis=0, keepdims=True)
    p1 = jnp.sum(jnp.where(r8 == i1, sp, 0.0), axis=0, keepdims=True)
    p2 = jnp.sum(jnp.where(r8 == i2, sp, 0.0), axis=0, keepdims=True)
    tot = p1 + p2
    w1 = gp * (p1 / tot)
    w2 = gp * (p2 / tot)
    e1 = g_sel * EXPERTS_PER_GROUP + i1
    e2 = g_sel * EXPERTS_PER_GROUP + i2

    r32 = lax.broadcasted_iota(I32, (N_EXPERTS, T), 0)
    oh1 = r32 == e1
    oh2 = r32 == e2
    oh = jnp.where(oh1 | oh2, 1.0, 0.0)
    upper = jnp.where(lax.broadcasted_iota(I32, (T, T), 0) < lax.broadcasted_iota(I32, (T, T), 1),
                      1.0, 0.0).astype(BF16)
    before = jnp.dot(oh.astype(BF16), upper, preferred_element_type=F32)
    cnt = jnp.sum(oh, axis=1, keepdims=True)
    run8 = jnp.floor((cnt + (SUBLANES - 1.0)) * (1.0 / SUBLANES))
    lower = jnp.where(lax.broadcasted_iota(I32, (N_EXPERTS, N_EXPERTS), 1)
                      < lax.broadcasted_iota(I32, (N_EXPERTS, N_EXPERTS), 0), 1.0, 0.0).astype(BF16)
    run_start = jnp.dot(lower, jnp.broadcast_to(run8, (N_EXPERTS, LANES)).astype(BF16),
                        preferred_element_type=F32)[:, 0:1] * float(SUBLANES)
    pos = before + run_start
    row1 = jnp.sum(jnp.where(oh1, pos, 0.0), axis=0, keepdims=True)
    row2 = jnp.sum(jnp.where(oh2, pos, 0.0), axis=0, keepdims=True)
    return row1, row2, w1, w2, cnt


def _post_kernel(x_ref, yp_ref, ya_ref, mod_ref, wo_ref, g_ref,
                 wgT_ref, bg_ref, wrT_ref, br_ref,
                 x1_ref, h2_ref, rows_ref, mf_ref, cnt_ref):
    T = ROW_TILE
    gate_a = mod_ref[2]
    shift_f = mod_ref[3]
    scale_f = mod_ref[4]
    mix = jnp.dot(jnp.concatenate([yp_ref[...], ya_ref[...]], axis=1), wo_ref[...], preferred_element_type=F32)
    x1 = x_ref[...] + gate_a * mix
    x1_ref[...] = x1
    hb = (_rms(x1, g_ref[...]) * (1.0 + scale_f) + shift_f).astype(BF16)
    h2_ref[...] = hb

    for s in range(x_ref.shape[0] // T):
        cols = slice(s * T, (s + 1) * T)
        row1, row2, w1, w2, cnt = _route_tile(hb[cols], wgT_ref, bg_ref, wrT_ref, br_ref)
        cnt_ref[s] = cnt.astype(I32)
        rows_ref[0:1, cols] = row1.astype(I32)
        rows_ref[1:2, cols] = row2.astype(I32)
        zero = jnp.zeros_like(w1)
        mf_ref[cols, :] = jnp.concatenate([row1, row2, w1, w2, zero, zero, zero, zero], axis=0).T


def _post_call(x2, yp2, ya2, mod4, wo, g, wgT, bg, wrT, br, T, steps_per_batch):
    N, D = x2.shape
    const = lambda shape: pl.BlockSpec(shape, lambda i: (0,) * len(shape))
    return pl.pallas_call(
        _post_kernel,
        out_shape=(jax.ShapeDtypeStruct((N, D), F32),
                   jax.ShapeDtypeStruct((N, D), BF16),
                   jax.ShapeDtypeStruct((2, N), I32),
                   jax.ShapeDtypeStruct((N, SUBLANES), F32),
                   jax.ShapeDtypeStruct((N // ROW_TILE, N_EXPERTS, 1), I32)),
        grid=(N // T,),
        in_specs=[pl.BlockSpec((T, D), lambda i: (i, 0)),
                  pl.BlockSpec((T, POOL_WIDTH), lambda i: (i, 0)),
                  pl.BlockSpec((T, MLA_HEADS * V_HEAD_DIM), lambda i: (i, 0)),
                  pl.BlockSpec((None, N_MOD, 1, D), lambda i: (i // steps_per_batch, 0, 0, 0)),
                  const(wo.shape), const((1, D)),
                  const(wgT.shape), const(bg.shape), const(wrT.shape), const(br.shape)],
        out_specs=(pl.BlockSpec((T, D), lambda i: (i, 0)),
                   pl.BlockSpec((T, D), lambda i: (i, 0)),
                   pl.BlockSpec((2, T), lambda i: (0, i)),
                   pl.BlockSpec((T, SUBLANES), lambda i: (i, 0)),
                   pl.BlockSpec((T // ROW_TILE, N_EXPERTS, 1), lambda i: (i, 0, 0))),
        compiler_params=_cparams("arbitrary"),
        name="post",
    )(x2, yp2, ya2, mod4, wo, g, wgT, bg, wrT, br)


def _for_each_strip(tile, src_ref, dst_ref, n_ref, fn):
    first = tile * MAX_STRIPS
    for b in range(STRIP_BITS):
        last = first + n_ref[tile * STRIP_BITS + b]

        def body(r, c, n=SUBLANES << b, queue=b % 2):
            fn(pl.multiple_of(src_ref[r], SUBLANES), pl.multiple_of(dst_ref[r], SUBLANES), n, queue)
            return c

        lax.fori_loop(first, last, body, 0)
        first = last


def _wait_strip_rows(n_rows, make):
    for b in range((LOCAL_ROWS // SUBLANES).bit_length()):
        n = SUBLANES << b

        @pl.when((n_rows & n) != 0)
        def _():
            make(n).wait()


def _dispatch_kernel(src_ref, dst_ref, n_ref, trows_ref, pends_ref, padlen_ref, paddst_ref, rows_ref, h_ref, xs_ref,
                     lbuf, zbuf, sems, zsem):
    T = h_ref.shape[0]
    RB = zbuf.shape[0]
    half = h_ref.shape[1] // 2
    step = pl.program_id(0)
    n_steps = pl.num_programs(0)
    slot = step % DISPATCH_SLOTS

    def zero_fill(act, part, n_parts):
        def per_expert(e, c):
            pad = padlen_ref[e]
            dst = paddst_ref[e]
            for b in reversed(range((RB // SUBLANES).bit_length() - 1)):
                n = SUBLANES << b
                done = pad & ~(2 * n - 1)

                @pl.when((pad & n) != 0)
                def _():
                    act(pltpu.make_async_copy(zbuf.at[pl.ds(0, n)],
                                              xs_ref.at[pl.ds(pl.multiple_of(dst + done, SUBLANES), n)], zsem))
            return c

        def per_block(j, c):
            b = pends_ref[N_EXPERTS - 1] // RB + part + j * n_parts
            act(pltpu.make_async_copy(zbuf, xs_ref.at[pl.ds(pl.multiple_of(b * RB, RB), RB)], zsem))
            return c

        n_tail = xs_ref.shape[0] // RB - pends_ref[N_EXPERTS - 1] // RB
        lax.fori_loop(0, (N_EXPERTS - part + n_parts - 1) // n_parts,
                      lambda j, c: per_expert(part + j * n_parts, c), 0)
        lax.fori_loop(0, (n_tail - part + n_parts - 1) // n_parts, per_block, 0)

    @pl.when(step == 0)
    def _():
        zbuf[...] = jnp.zeros(zbuf.shape, zbuf.dtype)

    zero_fill(lambda cp: cp.start(), step, n_steps)

    r = lax.broadcasted_iota(I32, (LOCAL_ROWS, T), 0)
    perm = jnp.where((r == rows_ref[0:1, :]) | (r == rows_ref[1:2, :]), 1.0, 0.0).astype(BF16)
    h = h_ref[...]
    lo = jnp.dot(perm, h[:, :half], preferred_element_type=F32)
    hi = jnp.dot(perm, h[:, half:], preferred_element_type=F32)
    lbuf[slot] = _pack_bf16_pair(lo, hi)

    def strip(tile_slot):
        def make(loc, dst, n):
            return pltpu.make_async_copy(lbuf.at[tile_slot, pl.ds(loc, n)], xs_ref.at[pl.ds(dst, n)],
                                         sems.at[tile_slot])
        return make

    def wait_tile(tile):
        _wait_strip_rows(trows_ref[tile], lambda n: strip(tile % DISPATCH_SLOTS)(0, 0, n))

    _for_each_strip(step, src_ref, dst_ref, n_ref,
                    lambda loc, dst, n, queue: strip(slot)(loc, dst, n).start(priority=queue))

    @pl.when(step >= DISPATCH_SLOTS - 1)
    def _():
        wait_tile(step - (DISPATCH_SLOTS - 1))

    @pl.when(step == n_steps - 1)
    def _():
        for back in reversed(range(DISPATCH_SLOTS - 1)):
            @pl.when(step >= back)
            def _():
                wait_tile(step - back)
        zero_fill(lambda cp: cp.wait(), 0, 1)


def _dispatch_call(strip_src, strip_dst, strip_n, tile_rows, pends, pad_len, pad_dst, rows, h2, P_pad, T, RB):
    N, D = h2.shape
    return pl.pallas_call(
        _dispatch_kernel,
        out_shape=jax.ShapeDtypeStruct((P_pad, D // 2), U32),
        grid_spec=pltpu.PrefetchScalarGridSpec(
            num_scalar_prefetch=7,
            grid=(N // T,),
            in_specs=[pl.BlockSpec((2, T), lambda i, *_: (0, i)),
                      pl.BlockSpec((T, D), lambda i, *_: (i, 0))],
            out_specs=pl.BlockSpec(memory_space=pl.ANY),
            scratch_shapes=[pltpu.VMEM((DISPATCH_SLOTS, LOCAL_ROWS, D // 2), U32),
                            pltpu.VMEM((RB, D // 2), U32),
                            pltpu.SemaphoreType.DMA((DISPATCH_SLOTS,)),
                            pltpu.SemaphoreType.DMA]),
        compiler_params=_cparams("arbitrary"),
        name="dispatch",
    )(strip_src, strip_dst, strip_n, tile_rows, pends, pad_len, pad_dst, rows, h2)


def _expert_kernel(seg_ref, sege_ref, nseg_ref, nused_ref, xs_hbm, wgu_hbm, wd_hbm, ys_hbm,
                   xbuf, ybuf, zbuf, wgu_f32, wd_f32, wgu_bf, wd_bf, xsem, ysem, wsem, zsem):
    RB = xbuf.shape[1]
    n_x = xbuf.shape[0]
    half = wgu_bf.shape[0] // 2
    n_used = nused_ref[0]
    n_blocks = ys_hbm.shape[0] // RB

    def rows(b):
        return pl.ds(pl.multiple_of(b * RB, RB), RB)

    def x_copy(b):
        return pltpu.make_async_copy(xs_hbm.at[rows(b)], xbuf.at[b % n_x], xsem.at[b % n_x])

    def y_copy(b):
        return pltpu.make_async_copy(ybuf.at[b % 2], ys_hbm.at[rows(b)], ysem.at[b % 2])

    def tail_copy(b):
        return pltpu.make_async_copy(zbuf, ys_hbm.at[rows(b)], zsem)

    def weight_copies(kk):
        slot = kk % 2
        e = sege_ref[kk]
        return (pltpu.make_async_copy(wgu_hbm.at[e], wgu_f32.at[slot], wsem.at[0, slot]),
                pltpu.make_async_copy(wd_hbm.at[e], wd_f32.at[slot], wsem.at[1, slot]))

    zbuf[...] = jnp.zeros(zbuf.shape, zbuf.dtype)
    for cp in weight_copies(0):
        cp.start()
    for b in range(n_x - 1):
        @pl.when(b < n_used)
        def _():
            x_copy(b).start()

    def body(i, c):
        k = seg_ref[i]

        @pl.when((i == 0) | (k != seg_ref[jnp.maximum(i - 1, 0)]))
        def _():
            for cp in weight_copies(k):
                cp.wait()

            @pl.when(k + 1 < nseg_ref[0])
            def _():
                for cp in weight_copies(k + 1):
                    cp.start()

            wgu_bf[...] = wgu_f32[k % 2].astype(BF16)
            wd_bf[...] = wd_f32[k % 2].astype(BF16)

        @pl.when(i + n_x - 1 < n_used)
        def _():
            x_copy(i + n_x - 1).start()

        @pl.when(n_used + i < n_blocks)
        def _():
            tail_copy(n_used + i).start()

        x_copy(i).wait()
        lo, hi = _unpack_bf16_pair(xbuf[i % n_x])
        gu = (jnp.dot(lo, wgu_bf[0:half, :], preferred_element_type=F32)
              + jnp.dot(hi, wgu_bf[half:, :], preferred_element_type=F32))
        gate = gu[:, :D_EXPERT]
        up = gu[:, D_EXPERT:]
        act = gate / (1.0 + jnp.exp(-gate)) * up
        y = jnp.dot(act.astype(BF16), wd_bf[...], preferred_element_type=F32)
        yb = y.astype(BF16).astype(F32)

        @pl.when(i >= 2)
        def _():
            y_copy(i - 2).wait()

        ybuf[i % 2] = _pack_bf16_pair(yb[:, :half], yb[:, half:])
        y_copy(i).start()
        return c

    lax.fori_loop(0, n_used, body, 0)

    for back in (2, 1):
        @pl.when(n_used >= back)
        def _():
            y_copy(n_used - back).wait()

    lax.fori_loop(2 * n_used, n_blocks, lambda b, c: (tail_copy(b).start(), c)[1], 0)
    lax.fori_loop(n_used, n_blocks, lambda b, c: (tail_copy(b).wait(), c)[1], 0)


def _expert_call(block_seg, seg_expert, n_seg, n_used, xs, wgu, wd, RB):
    P, Dh = xs.shape
    D = 2 * Dh
    any_spec = pl.BlockSpec(memory_space=pl.ANY)
    return pl.pallas_call(
        _expert_kernel,
        out_shape=jax.ShapeDtypeStruct((P, Dh), U32),
        grid_spec=pltpu.PrefetchScalarGridSpec(
            num_scalar_prefetch=4,
            grid=(1,),
            in_specs=[any_spec, any_spec, any_spec],
            out_specs=any_spec,
            scratch_shapes=[pltpu.VMEM((EXPERT_X_SLOTS, RB, Dh), U32), pltpu.VMEM((2, RB, Dh), U32),
                            pltpu.VMEM((RB, Dh), U32),
                            pltpu.VMEM((2, D, 2 * D_EXPERT), F32), pltpu.VMEM((2, D_EXPERT, D), F32),
                            pltpu.VMEM((D, 2 * D_EXPERT), BF16), pltpu.VMEM((D_EXPERT, D), BF16),
                            pltpu.SemaphoreType.DMA((EXPERT_X_SLOTS,)), pltpu.SemaphoreType.DMA((2,)),
                            pltpu.SemaphoreType.DMA((2, 2)), pltpu.SemaphoreType.DMA]),
        compiler_params=_cparams("arbitrary"),
        name="experts",
    )(block_seg, seg_expert, n_seg, n_used, xs, wgu, wd)


def _combine_kernel(src_ref, dst_ref, n_ref, trows_ref, x1_ref, tok_ref, mod_ref, g_ref, ys_ref, o_ref, ybuf, sems):
    T = x1_ref.shape[0]
    step = pl.program_id(0)
    n_steps = pl.num_programs(0)
    slot = step % 2

    def strip(tile_slot):
        def make(loc, dst, n):
            return pltpu.make_async_copy(ys_ref.at[pl.ds(dst, n)], ybuf.at[tile_slot, pl.ds(loc, n)],
                                         sems.at[tile_slot])
        return make

    @pl.when(step == 0)
    def _():
        ybuf[...] = jnp.zeros(ybuf.shape, ybuf.dtype)
        _for_each_strip(step, src_ref, dst_ref, n_ref,
                        lambda loc, dst, n, queue: strip(slot)(loc, dst, n).start(priority=queue))

    @pl.when(step + 1 < n_steps)
    def _():
        _for_each_strip(step + 1, src_ref, dst_ref, n_ref,
                        lambda loc, dst, n, queue: strip(1 - slot)(loc, dst, n).start(priority=queue))

    _wait_strip_rows(trows_ref[step], lambda n: strip(slot)(0, 0, n))

    tok = tok_ref[...]
    lo, hi = _unpack_bf16_pair(ybuf[slot])
    n_chunks = 4
    tc = T // n_chunks

    def permw_chunk(ci):
        t = tok[ci * tc:(ci + 1) * tc]
        rows = t[:, 0:2].astype(I32)
        c = lax.broadcasted_iota(I32, (tc, LOCAL_ROWS), 1)
        return jnp.where(c == rows[:, 0:1], t[:, 2:3], jnp.where(c == rows[:, 1:2], t[:, 3:4], 0.0)).astype(BF16)

    p_next = permw_chunk(0)
    parts = []
    for ci in range(n_chunks):
        permw = p_next
        if ci + 1 < n_chunks:
            p_next = permw_chunk(ci + 1)
        parts.append(jnp.concatenate([jnp.dot(permw, lo, preferred_element_type=F32),
                                      jnp.dot(permw, hi, preferred_element_type=F32)], axis=1))
    moe = jnp.concatenate(parts, axis=0)
    gate_f = mod_ref[5]
    o_ref[...] = _rms(x1_ref[...] + gate_f * moe, g_ref[...])


def _combine_call(strip_src, strip_dst, strip_n, tile_rows, x1, tok, mod4, final_g, ys, T, tiles_per_batch):
    N, D = x1.shape
    return pl.pallas_call(
        _combine_kernel,
        out_shape=jax.ShapeDtypeStruct((N, D), F32),
        grid_spec=pltpu.PrefetchScalarGridSpec(
            num_scalar_prefetch=4,
            grid=(N // T,),
            in_specs=[pl.BlockSpec((T, D), lambda i, *_: (i, 0)),
                      pl.BlockSpec((T, SUBLANES), lambda i, *_: (i, 0)),
                      pl.BlockSpec((None, N_MOD, 1, D), lambda i, *_: (i // tiles_per_batch, 0, 0, 0)),
                      pl.BlockSpec((1, D), lambda i, *_: (0, 0)),
                      pl.BlockSpec(memory_space=pl.ANY)],
            out_specs=pl.BlockSpec((T, D), lambda i, *_: (i, 0)),
            scratch_shapes=[pltpu.VMEM((2, LOCAL_ROWS, D // 2), U32), pltpu.SemaphoreType.DMA((2,))]),
        compiler_params=_cparams("arbitrary"),
        name="combine",
    )(strip_src, strip_dst, strip_n, tile_rows, x1, tok, mod4, final_g, ys)


def _round_up(v, m):
    return (v + m - 1) // m * m


def kernel(x, c, positions, w_mod, b_mod, norm_mix_g, w_in, w_pool, pool_scale, q_norm_g, w_uq, kv_norm_g, w_ukv, w_o, norm_ffn_g, w_group, b_group, w_router, b_router, w_gate_up, w_down, final_g):
    B, S, D = x.shape
    N = B * S
    depth = w_mod.shape[0]
    T = ROW_TILE
    RB = EXPERT_ROWS
    assert depth == 1, "the final RMSNorm is fused into the layer's combine step"
    assert S % T == 0 and S % min(POST_TILE, S) == 0 and min(POST_TILE, S) % T == 0 and S % min(PRE_TILE, S) == 0 and S % min(ATTN_TQ, S) == 0 and min(ATTN_TQ, S) % ATTN_TK == 0
    tiles_per_batch = S // T
    post_tile = min(POST_TILE, S)
    n_tiles = N // T
    nH = MLA_HEADS
    l = 0

    inv_freq = ROPE_THETA ** (-(jnp.arange(0, QK_ROPE_DIM, 2, dtype=F32) / QK_ROPE_DIM))
    posr = positions.reshape(B, 1, S)
    cut3 = POOL_WIDTH + Q_LORA_RANK + KV_LORA_RANK

    mod4 = _mod_call(c, w_mod[l], b_mod[l]).reshape(B, N_MOD, 1, D)

    wi = w_in[l]
    w_in_main = wi[:, :cut3].astype(BF16)
    w_krT = wi[:, cut3:].T.astype(BF16)
    wq = w_uq[l].reshape(Q_LORA_RANK, nH, QK_HEAD_DIM)
    wq_n = wq[:, :, :QK_NOPE_DIM].reshape(Q_LORA_RANK, nH * QK_NOPE_DIM)
    wq_r = wq[:, :, QK_NOPE_DIM:]
    w_uqT = jnp.concatenate([wq_n, wq_r.reshape(Q_LORA_RANK, nH * QK_ROPE_DIM)], axis=1).T.astype(BF16)
    wkv = w_ukv[l].reshape(KV_LORA_RANK, nH, QK_NOPE_DIM + V_HEAD_DIM)
    w_uk = wkv[:, :, :QK_NOPE_DIM].reshape(KV_LORA_RANK, nH * QK_NOPE_DIM).astype(BF16)
    w_uvT = wkv[:, :, QK_NOPE_DIM:].reshape(KV_LORA_RANK, nH * V_HEAD_DIM).T.astype(BF16)
    qg = (q_norm_g[l] * (QK_HEAD_DIM ** -0.5 * LOG2_E)).reshape(1, Q_LORA_RANK)
    wpool_bd = jnp.zeros((POOL_WIDTH, POOL_WIDTH), F32)
    for g in range(len(POOL_WINDOWS)):
        sl = slice(g * POOL_GROUP_DIM, (g + 1) * POOL_GROUP_DIM)
        wpool_bd = wpool_bd.at[sl, sl].set(w_pool[l, g])
    wpool_bd = wpool_bd.astype(BF16)

    qT, k, vT, yp = _pre_call(
        x, posr, mod4, norm_mix_g[l].reshape(1, D), w_in_main, w_krT, inv_freq.reshape(QK_ROPE_DIM // 2, 1),
        wpool_bd, pool_scale[l].reshape(1, POOL_WIDTH), qg, w_uqT, kv_norm_g[l].reshape(1, KV_LORA_RANK),
        w_uk, w_uvT, min(PRE_TILE, S))
    ya = _attn_call(qT, k, vT)

    wo = w_o[l].astype(BF16)
    wgT = jnp.zeros((SUBLANES, D), F32).at[:N_GROUPS].set(w_group[l].T).astype(BF16)
    bg = jnp.zeros((SUBLANES, 1), F32).at[:N_GROUPS, 0].set(b_group[l])
    x1, h2, rows, meta_f, tile_cnt = _post_call(
        x.reshape(N, D), yp.reshape(N, POOL_WIDTH), ya.reshape(N, nH * V_HEAD_DIM), mod4,
        wo, norm_ffn_g[l].reshape(1, D),
        wgT, bg, w_router[l].T.astype(BF16), b_router[l].reshape(N_EXPERTS, 1), post_tile, S // post_tile)

    run_len = _round_up(tile_cnt[:, :, 0], SUBLANES)
    run_loc = jnp.cumsum(run_len, axis=1) - run_len
    seg_rows = jnp.sum(run_len, axis=0)
    seg_len = _round_up(seg_rows, RB)
    pends = jnp.cumsum(seg_len).astype(I32)
    pstarts = pends - seg_len
    run_dst = pstarts[None, :] + jnp.cumsum(run_len, axis=0) - run_len
    pad_len = (seg_len - seg_rows).astype(I32)
    pad_dst = (pstarts + seg_rows).astype(I32)
    bits = jnp.arange(STRIP_BITS, dtype=I32)
    size = SUBLANES << bits
    valid = (run_len[:, None, :] & size[None, :, None]) != 0
    done = run_len[:, None, :] & ~(2 * size[None, :, None] - 1)
    flat = lambda a: a.reshape(n_tiles, STRIP_BITS * N_EXPERTS)
    valid_f = flat(valid)
    pos = jnp.cumsum(valid_f.astype(I32), axis=1) - valid_f.astype(I32)
    pick = valid_f[:, None, :] & (pos[:, None, :] == jnp.arange(MAX_STRIPS, dtype=I32)[None, :, None])
    gather = lambda a: jnp.sum(jnp.where(pick, flat(a)[:, None, :], 0), axis=-1).reshape(-1).astype(I32)
    strip_src = gather(run_loc[:, None, :] + done)
    strip_dst = gather(run_dst[:, None, :] + done)
    strip_n = jnp.sum(valid, axis=2).reshape(-1).astype(I32)
    tile_rows = jnp.sum(run_len, axis=1).astype(I32)
    P_pad = _round_up(2 * N + n_tiles * N_EXPERTS * (SUBLANES - 1), RB) + N_EXPERTS * RB
    n_rb = P_pad // RB
    n_used = (pends[-1:] // RB).astype(I32)
    block_start = jnp.minimum(jnp.arange(n_rb, dtype=I32), n_used - 1) * RB
    block_e = jnp.sum((pends[None, :] <= block_start[:, None]).astype(I32), axis=1)
    nonempty = seg_len > 0
    seg_index = jnp.cumsum(nonempty.astype(I32)) - 1
    n_seg = jnp.sum(nonempty.astype(I32)).reshape(1)
    experts = jnp.arange(N_EXPERTS, dtype=I32)
    block_seg = jnp.sum(jnp.where(block_e[:, None] == experts[None, :], seg_index[None, :], 0), axis=1).astype(I32)
    seg_expert = jnp.sum(jnp.where(nonempty[None, :] & (seg_index[None, :] == experts[:, None]), experts[None, :], 0),
                         axis=1).astype(I32)

    xs = _dispatch_call(strip_src, strip_dst, strip_n, tile_rows, pends, pad_len, pad_dst, rows, h2, P_pad, T, RB)
    ys = _expert_call(block_seg, seg_expert, n_seg, n_used, xs, w_gate_up[l], w_down[l], RB)
    out = _combine_call(strip_src, strip_dst, strip_n, tile_rows, x1, meta_f, mod4, final_g.reshape(1, D), ys,
                        T, tiles_per_batch)
    return out.reshape(B, S, D)
```
